```python
import math
import jax, jax.numpy as jnp
from jax import lax
import numpy as np

D_MODEL = 1024
BATCH = 2
SEQ = 8192
DEPTH = 2

GRID_W = 64
CTX_LEN = 256
HEAD_DIM = 64
H_A = D_MODEL // (4 * HEAD_DIM)
H_B = D_MODEL // (4 * HEAD_DIM)
H_C = D_MODEL // (2 * HEAD_DIM)
D_A = H_A * HEAD_DIM
D_B = H_B * HEAD_DIM
D_C = H_C * HEAD_DIM
D_MIX = D_A + D_B + D_C
PROJ_SIZES = (D_A, D_A, D_A, D_A, D_A,
              D_B, D_B, D_B, D_B,
              H_B, H_B, H_B, H_B,
              D_C, D_C, D_C)
D_PROJ = sum(PROJ_SIZES)
CHUNK = 64
CONV_K = 5
WIN_R = 8
WIN_C = 16
ROPE_BASE = 10000.0
D_FF = 3584
N_EXPERTS = 8
TOP_K = 2
MOE_BLOCK = 256
N_DENSE = (DEPTH + 1) // 2
N_MOE = DEPTH // 2
EPS = 1e-6

kernel_name = 'hybrid_hgrn2_gdn_natten_moe_dit_block'


def rms_norm(x, g):
    xf = x.astype(jnp.float32)
    y = xf * lax.rsqrt(jnp.mean(xf * xf, axis=-1, keepdims=True) + EPS)
    return (y * g.astype(jnp.float32)).astype(x.dtype)


def modulate(x, g, shift, scale):
    return rms_norm(x, g) * (1.0 + scale) + shift


def l2_normalize(x):
    return x * lax.rsqrt(jnp.sum(x * x, axis=-1, keepdims=True) + EPS)


def split_cols(a, sizes):
    return jnp.split(a, np.cumsum(sizes)[:-1].tolist(), axis=-1)


def to_heads(a, n_heads):
    b, t, _ = a.shape
    return a.reshape(b, t, n_heads, -1).transpose(0, 2, 1, 3)


def merge_heads(a):
    b, h, t, d = a.shape
    return a.transpose(0, 2, 1, 3).reshape(b, t, h * d)


def depthwise_conv(x, w):
    return lax.conv_general_dilated(
        x, w.astype(x.dtype)[:, None, :], window_strides=(1,),
        padding=[(CONV_K // 2, CONV_K // 2)],
        dimension_numbers=('NWC', 'WIO', 'NWC'), feature_group_count=x.shape[-1])


def axial_rope(n_tok):
    t = jnp.arange(n_tok)
    row = (t // GRID_W).astype(jnp.float32)
    col = (t % GRID_W).astype(jnp.float32)
    n_freq = HEAD_DIM // 4
    inv = ROPE_BASE ** (-jnp.arange(n_freq, dtype=jnp.float32) / n_freq)
    ang = jnp.concatenate([row[:, None] * inv, col[:, None] * inv], axis=-1)
    return jnp.cos(ang), jnp.sin(ang)


def apply_rope(x, cos, sin):
    x1, x2 = jnp.split(x, 2, axis=-1)
    return jnp.concatenate([x1 * cos - x2 * sin, x1 * sin + x2 * cos], axis=-1)


def masked_exp(d, mask):
    return jnp.where(mask, jnp.exp(jnp.where(mask, d, 0.0)), 0.0)


def to_chunks(a):
    b, h, t = a.shape[:3]
    return jnp.moveaxis(a.reshape(b, h, t // CHUNK, CHUNK, *a.shape[3:]), 2, 0)


def from_chunks(a):
    n, b, h, c, d = a.shape
    return jnp.moveaxis(a, 0, 2).reshape(b, h, n * c, d)


def hgrn2_chunk_scan(q, k, v, log_f, state):
    tril = jnp.tril(jnp.ones((CHUNK, CHUNK), bool))

    def step(s, inp):
        qc, kc, vc, lfc = inp
        b = jnp.cumsum(lfc, axis=2)
        diff = b[:, :, :, None, :] - b[:, :, None, :, :]
        dec = masked_exp(diff, tril[:, :, None])
        att = jnp.einsum('bhtd,bhsd,bhtsd->bhts', qc, kc, dec)
        o = jnp.einsum('bhts,bhsv->bhtv', att, vc) + jnp.einsum('bhtd,bhdv->bhtv', qc * jnp.exp(b), s)
        b_end = b[:, :, -1:, :]
        s = s * jnp.exp(b_end[:, :, 0, :, None]) + jnp.einsum('bhsd,bhsv->bhdv', kc * jnp.exp(b_end - b), vc)
        return s, o

    s, o = lax.scan(step, state, (to_chunks(q), to_chunks(k), to_chunks(v), to_chunks(log_f)))
    return from_chunks(o), s


def gdn_chunk_scan(q, k, v, log_alpha, beta, state):
    tril = jnp.tril(jnp.ones((CHUNK, CHUNK), bool))
    strict = jnp.tril(jnp.ones((CHUNK, CHUNK), bool), -1)
    eye = jnp.eye(CHUNK, dtype=jnp.float32)

    def step(s, inp):
        qc, kc, vc, lac, bc = inp
        g = jnp.cumsum(lac, axis=-1)
        gam = masked_exp(g[..., :, None] - g[..., None, :], tril)
        kk = jnp.einsum('bhtd,bhsd->bhts', kc, kc)
        m = eye + jnp.where(strict, bc[..., :, None] * kk * gam, 0.0)
        rhs = bc[..., None] * (vc - jnp.exp(g)[..., None] * jnp.einsum('bhtd,bhdv->bhtv', kc, s))
        u = lax.linalg.triangular_solve(m, rhs, left_side=True, lower=True, unit_diagonal=True)
        qk = jnp.einsum('bhtd,bhsd->bhts', qc, kc) * gam
        o = jnp.einsum('bhtd,bhdv->bhtv', qc * jnp.exp(g)[..., None], s) + jnp.einsum('bhts,bhsv->bhtv', qk, u)
        g_end = g[..., -1:]
        s = s * jnp.exp(g_end)[..., None] + jnp.einsum('bhsd,bhsv->bhdv', kc * jnp.exp(g_end - g)[..., None], u)
        return s, o

    s, o = lax.scan(step, state, (to_chunks(q), to_chunks(k), to_chunks(v),
                                  to_chunks(log_alpha), to_chunks(beta)))
    return from_chunks(o), s


def bidirectional_scan(scan_fn, lat_fwd, ctx_fwd, lat_bwd, ctx_bwd, state_shape):
    flip = lambda seqs: tuple(jnp.flip(a, axis=2) for a in seqs)
    s0 = jnp.zeros(state_shape, jnp.float32)
    o_ctx_f, s_ctx_f = scan_fn(*ctx_fwd, s0)
    o_lat_f, _ = scan_fn(*lat_fwd, s_ctx_f)
    o_ctx_b, s_ctx_b = scan_fn(*flip(ctx_bwd), s0)
    o_lat_b, _ = scan_fn(*flip(lat_bwd), s_ctx_b)
    return o_lat_f + jnp.flip(o_lat_b, axis=2), o_ctx_f + jnp.flip(o_ctx_b, axis=2)


def hgrn2_inputs(parts, lb_fwd, lb_bwd):
    q = to_heads(parts[0], H_A).astype(jnp.float32) * HEAD_DIM ** -0.5
    v = to_heads(parts[3], H_A).astype(jnp.float32)

    def forget(z, lb):
        z = to_heads(z, H_A).astype(jnp.float32)
        lb = lb.reshape(H_A, 1, HEAD_DIM)
        log_f = jax.nn.log_sigmoid(z) + jnp.log1p(lb * jnp.exp(-z))
        return (1.0 - lb) * jax.nn.sigmoid(-z), log_f

    k_f, lf_f = forget(parts[1], lb_fwd)
    k_b, lf_b = forget(parts[2], lb_bwd)
    return (q, k_f, v, lf_f), (q, k_b, v, lf_b)


def gdn_inputs(parts, conv_w, a_log, dt_bias, rope):
    qkv = jax.nn.silu(depthwise_conv(jnp.concatenate(parts[5:8], axis=-1), conv_w)).astype(jnp.float32)
    q, k, v = [to_heads(t, H_B) for t in jnp.split(qkv, 3, axis=-1)]
    q, k = l2_normalize(q), l2_normalize(k)
    if rope is not None:
        q, k = apply_rope(q, *rope), apply_rope(k, *rope)
    q = q * HEAD_DIM ** -0.5

    def gates(a, b, d):
        log_alpha = -jnp.exp(a_log[d].astype(jnp.float32)) * jax.nn.softplus(
            a.astype(jnp.float32) + dt_bias[d].astype(jnp.float32))
        beta = jax.nn.sigmoid(b.astype(jnp.float32))
        return log_alpha.transpose(0, 2, 1), beta.transpose(0, 2, 1)

    la_f, be_f = gates(parts[9], parts[11], 0)
    la_b, be_b = gates(parts[10], parts[12], 1)
    return (q, k, v, la_f, be_f), (q, k, v, la_b, be_b)


def gated_head_norm(o, g, gate):
    return merge_heads(rms_norm(o, g)).astype(gate.dtype) * jax.nn.silu(gate)


def neighbourhood_attention(q, k, v, k_ctx, v_ctx, rpb):
    b, t, h, dh = q.shape
    rows = t // GRID_W
    kr = min(WIN_R, rows)
    q = q.reshape(b, rows, GRID_W, h, dh)
    k = k.reshape(b, rows, GRID_W, h, dh)
    v = v.reshape(b, rows, GRID_W, h, dh)
    cols = jnp.arange(GRID_W)
    col_idx = jnp.clip(cols - WIN_C // 2, 0, GRID_W - WIN_C)[:, None] + jnp.arange(WIN_C)
    rpb_cols = rpb[:, :, col_idx - cols[:, None] + WIN_C - 1]
    n_loc = kr * WIN_C

    def row_block(r):
        r0 = jnp.clip(r - WIN_R // 2, 0, rows - kr)
        q_r = lax.dynamic_index_in_dim(q, r, axis=1, keepdims=False)
        k_w = lax.dynamic_slice_in_dim(k, r0, kr, axis=1)[:, :, col_idx]
        v_w = lax.dynamic_slice_in_dim(v, r0, kr, axis=1)[:, :, col_idx]
        bias = rpb_cols[:, r0 + jnp.arange(kr) - r + WIN_R - 1]
        s_loc = jnp.einsum('bjhd,bajchd->bhjac', q_r, k_w).astype(jnp.float32) \
            + bias.transpose(0, 2, 1, 3)[None].astype(jnp.float32)
        s_ctx = jnp.einsum('bjhd,blhd->bhjl', q_r, k_ctx).astype(jnp.float32)
        p = jax.nn.softmax(jnp.concatenate([s_loc.reshape(b, h, GRID_W, n_loc), s_ctx], axis=-1), axis=-1)
        p = p.astype(v.dtype)
        p_loc = p[..., :n_loc].reshape(b, h, GRID_W, kr, WIN_C)
        return jnp.einsum('bhjac,bajchd->bjhd', p_loc, v_w) + jnp.einsum('bhjl,blhd->bjhd', p[..., n_loc:], v_ctx)

    out = lax.map(row_block, jnp.arange(rows))
    return jnp.moveaxis(out, 0, 1).reshape(b, t, h, dh)


def context_attention(q, k, v):
    s = jnp.einsum('blhd,bmhd->bhlm', q, k).astype(jnp.float32)
    p = jax.nn.softmax(s, axis=-1).astype(v.dtype)
    return jnp.einsum('bhlm,bmhd->blhd', p, v)


def token_mixers(p_lat, p_ctx, rope, lb_fwd, lb_bwd, hgrn_g, conv_w, a_log, dt_bias, gdn_g,
                 qn_g, kn_g, rpb, with_ctx_out):
    b, t, _ = p_lat[0].shape
    l = p_ctx[0].shape[1]
    a_lat_f, a_lat_b = hgrn2_inputs(p_lat, lb_fwd, lb_bwd)
    a_ctx_f, a_ctx_b = hgrn2_inputs(p_ctx, lb_fwd, lb_bwd)
    oa_lat, oa_ctx = bidirectional_scan(hgrn2_chunk_scan, a_lat_f, a_ctx_f, a_lat_b, a_ctx_b,
                                        (b, H_A, HEAD_DIM, HEAD_DIM))
    g_lat_f, g_lat_b = gdn_inputs(p_lat, conv_w, a_log, dt_bias, rope)
    g_ctx_f, g_ctx_b = gdn_inputs(p_ctx, conv_w, a_log, dt_bias, None)
    ob_lat, ob_ctx = bidirectional_scan(gdn_chunk_scan, g_lat_f, g_ctx_f, g_lat_b, g_ctx_b,
                                        (b, H_B, HEAD_DIM, HEAD_DIM))

    def na_qkv(parts):
        q, k, v = [a.reshape(a.shape[0], a.shape[1], H_C, HEAD_DIM) for a in parts[13:16]]
        return rms_norm(q, qn_g) * HEAD_DIM ** -0.5, rms_norm(k, kn_g), v

    q_lat, k_lat, v_lat = na_qkv(p_lat)
    q_ctx, k_ctx, v_ctx = na_qkv(p_ctx)
    oc_lat = neighbourhood_attention(q_lat, k_lat, v_lat, k_ctx, v_ctx, rpb).reshape(b, t, D_C)
    y_lat = jnp.concatenate([gated_head_norm(oa_lat, hgrn_g, p_lat[4]),
                             gated_head_norm(ob_lat, gdn_g, p_lat[8]), oc_lat], axis=-1)
    if not with_ctx_out:
        return y_lat, None
    oc_ctx = context_attention(q_ctx, k_ctx, v_ctx).reshape(b, l, D_C)
    y_ctx = jnp.concatenate([gated_head_norm(oa_ctx, hgrn_g, p_ctx[4]),
                             gated_head_norm(ob_ctx, gdn_g, p_ctx[8]), oc_ctx], axis=-1)
    return y_lat, y_ctx


def swiglu(h, w1, w3, w2):
    return (jax.nn.silu(h @ w1) * (h @ w3)) @ w2


def moe_swiglu(h, w_router, b_router, w1, w3, w2):
    shape = h.shape
    hf = h.reshape(-1, shape[-1])
    n = hf.shape[0]
    logits = (hf @ w_router).astype(jnp.float32) + b_router.astype(jnp.float32)
    top_logit, top_e = lax.top_k(logits, TOP_K)
    gate = jax.nn.softmax(top_logit, axis=-1)
    e_flat = top_e.reshape(-1)
    tok_flat = jnp.repeat(jnp.arange(n, dtype=jnp.int32), TOP_K)
    order = jnp.argsort(e_flat)
    e_sorted = e_flat[order]
    counts = jnp.bincount(e_flat, length=N_EXPERTS)
    padded = (counts + MOE_BLOCK - 1) // MOE_BLOCK * MOE_BLOCK
    pad_end = jnp.cumsum(padded)
    pad_start = pad_end - padded
    grp_start = jnp.cumsum(counts) - counts
    slot = pad_start[e_sorted] + jnp.arange(n * TOP_K) - grp_start[e_sorted]
    n_blocks = -(-(n * TOP_K) // MOE_BLOCK) + N_EXPERTS
    n_slots = n_blocks * MOE_BLOCK
    slot_tok = jnp.full((n_slots,), n, jnp.int32).at[slot].set(tok_flat[order])
    slot_gate = jnp.zeros((n_slots,), jnp.float32).at[slot].set(gate.reshape(-1)[order])
    block_e = jnp.minimum(jnp.searchsorted(pad_end, jnp.arange(n_blocks) * MOE_BLOCK, side='right'),
                          N_EXPERTS - 1)
    h_pad = jnp.concatenate([hf, jnp.zeros((1, shape[-1]), hf.dtype)], axis=0)
    xb = h_pad[slot_tok].reshape(n_blocks, MOE_BLOCK, shape[-1])
    yb = lax.map(lambda a: swiglu(a[0], w1[a[1]], w3[a[1]], w2[a[1]]), (xb, block_e))
    y = yb.reshape(n_slots, shape[-1]) * slot_gate[:, None].astype(hf.dtype)
    out = jnp.zeros_like(h_pad).at[slot_tok].add(y)[:n]
    return out.reshape(shape)


def setup_inputs(seed: int = 0) -> dict:
    key = jax.random.key(seed)
    ks = jax.random.split(key, 27)
    f32 = jnp.float32
    d = D_MODEL

    def nrm(k, shape, scale):
        return jax.random.normal(k, shape, f32) * scale

    dt = jnp.exp(jax.random.uniform(ks[14], (DEPTH, 2, H_B), f32, math.log(1e-3), math.log(1e-1)))
    return {
        'x': nrm(ks[0], (BATCH, SEQ, d), 1.0),
        'c': nrm(ks[1], (BATCH, d), 1.0),
        'ctx': nrm(ks[2], (BATCH, CTX_LEN, d), 1.0),
        'c_ctx': nrm(ks[3], (d,), 1.0),
        'ada_w': nrm(ks[4], (DEPTH, d, 6 * d), 0.5 * d ** -0.5),
        'ada_b': nrm(ks[5], (DEPTH, 6 * d), 0.02),
        'norm1_g': 1.0 + nrm(ks[6], (DEPTH, d), 0.02),
        'norm2_g': 1.0 + nrm(ks[7], (DEPTH, d), 0.02),
        'w_in': nrm(ks[8], (DEPTH, d, D_PROJ), d ** -0.5),
        'w_out': nrm(ks[9], (DEPTH, D_MIX, d), D_MIX ** -0.5),
        'hgrn_lb_raw': nrm(ks[10], (2, DEPTH, D_A), 1.0),
        'hgrn_onorm_g': 1.0 + nrm(ks[11], (DEPTH, HEAD_DIM), 0.02),
        'gdn_conv_w': nrm(ks[12], (DEPTH, CONV_K, 3 * D_B), CONV_K ** -0.5),
        'gdn_a_log': jnp.log(jax.random.uniform(ks[13], (DEPTH, 2, H_B), f32, 1.0, 16.0)),
        'gdn_dt_bias': dt + jnp.log(-jnp.expm1(-dt)),
        'gdn_onorm_g': 1.0 + nrm(ks[15], (DEPTH, HEAD_DIM), 0.02),
        'na_qnorm_g': 1.0 + nrm(ks[16], (DEPTH, HEAD_DIM), 0.02),
        'na_knorm_g': 1.0 + nrm(ks[17], (DEPTH, HEAD_DIM), 0.02),
        'na_rpb': nrm(ks[18], (DEPTH, H_C, 2 * WIN_R - 1, 2 * WIN_C - 1), 0.1),
        'ffn_w1': nrm(ks[19], (N_DENSE, d, D_FF), d ** -0.5),
        'ffn_w3': nrm(ks[20], (N_DENSE, d, D_FF), d ** -0.5),
        'ffn_w2': nrm(ks[21], (N_DENSE, D_FF, d), D_FF ** -0.5),
        'moe_router_w': nrm(ks[22], (N_MOE, d, N_EXPERTS), d ** -0.5),
        'moe_router_b': nrm(ks[23], (N_MOE, N_EXPERTS), 0.01),
        'moe_w1': nrm(ks[24], (N_MOE, N_EXPERTS, d, D_FF), d ** -0.5),
        'moe_w3': nrm(ks[25], (N_MOE, N_EXPERTS, d, D_FF), d ** -0.5),
        'moe_w2': nrm(ks[26], (N_MOE, N_EXPERTS, D_FF, d), D_FF ** -0.5),
    }


def reference(x, c, ctx, c_ctx, ada_w, ada_b, norm1_g, norm2_g, w_in, w_out, hgrn_lb_raw,
              hgrn_onorm_g, gdn_conv_w, gdn_a_log, gdn_dt_bias, gdn_onorm_g, na_qnorm_g, na_knorm_g,
              na_rpb, ffn_w1, ffn_w3, ffn_w2, moe_router_w, moe_router_b, moe_w1, moe_w3, moe_w2):
    rope = axial_rope(x.shape[1])
    lb_soft = jax.nn.softmax(hgrn_lb_raw.astype(jnp.float32), axis=1)
    lower_bound = jnp.cumsum(lb_soft, axis=1) - lb_soft[:, :1]
    silu_c = jax.nn.silu(c)
    silu_cc = jax.nn.silu(c_ctx)[None]
    xc = ctx
    for l in range(DEPTH):
        last = l == DEPTH - 1
        sh1, sc1, g1, sh2, sc2, g2 = [m[:, None] for m in
                                      jnp.split(silu_c @ ada_w[l] + ada_b[l], 6, axis=-1)]
        sh1c, sc1c, g1c, sh2c, sc2c, g2c = [m[:, None] for m in
                                            jnp.split(silu_cc @ ada_w[l] + ada_b[l], 6, axis=-1)]
        p_lat = split_cols(modulate(x, norm1_g[l], sh1, sc1) @ w_in[l], PROJ_SIZES)
        p_ctx = split_cols(modulate(xc, norm1_g[l], sh1c, sc1c) @ w_in[l], PROJ_SIZES)
        y_lat, y_ctx = token_mixers(p_lat, p_ctx, rope, lower_bound[0, l], lower_bound[1, l],
                                    hgrn_onorm_g[l], gdn_conv_w[l], gdn_a_log[l], gdn_dt_bias[l],
                                    gdn_onorm_g[l], na_qnorm_g[l], na_knorm_g[l], na_rpb[l],
                                    not last)
        x = x + g1 * (y_lat @ w_out[l])
        i = l // 2
        if l % 2 == 0:
            ffn = lambda h: swiglu(h, ffn_w1[i], ffn_w3[i], ffn_w2[i])
        else:
            ffn = lambda h: moe_swiglu(h, moe_router_w[i], moe_router_b[i], moe_w1[i], moe_w3[i], moe_w2[i])
        x = x + g2 * ffn(modulate(x, norm2_g[l], sh2, sc2))
        if not last:
            xc = xc + g1c * (y_ctx @ w_out[l])
            xc = xc + g2c * ffn(modulate(xc, norm2_g[l], sh2c, sc2c))
    return x
```

```python
import functools
import math

import jax
import jax.numpy as jnp
import numpy as np
from jax import lax
from jax.experimental import pallas as pl
from jax.experimental.pallas import tpu as pltpu

F32 = jnp.float32
BF16 = jnp.bfloat16
HI = lax.Precision.HIGHEST

EPS = 1e-6
HEAD_DIM = 64
CHUNK = 64
TOK_BLK = 256
GRID_W = 64
WIN_R = 8
WIN_C = 16
CONV_K = 5
ROPE_BASE = 10000.0
N_EXPERTS = 8
LANES = 128
ROW_TILE = 512
FF_TILE = 512
MOE_ROWS = 512
CMB_ROWS = 256
VMEM_LIMIT = 56 * 1024 * 1024
NEG_BIG = -1e30


def _cparams(sem):
    return pltpu.CompilerParams(dimension_semantics=sem, vmem_limit_bytes=VMEM_LIMIT)


def _dot(a, b):
    return jnp.dot(a.astype(BF16), b.astype(BF16), preferred_element_type=F32)


def _dot_nt(a, b):
    return lax.dot_general(a.astype(BF16), b.astype(BF16), (((1,), (1,)), ((), ())),
                           preferred_element_type=F32)


def _dot_tn(a, b):
    return lax.dot_general(a.astype(BF16), b.astype(BF16), (((0,), (0,)), ((), ())),
                           preferred_element_type=F32)


def _dot_hi(a, b):
    return jnp.dot(a, b, precision=HI, preferred_element_type=F32)


def _iota(shape, dim):
    return lax.broadcasted_iota(jnp.int32, shape, dim)


def _head_mask(n):
    return (_iota((n, n), 0) // HEAD_DIM) == (_iota((n, n), 1) // HEAD_DIM)


def _group_mean_sq(x, gmean):
    return _dot_hi(x * x, gmean)


def _silu(x):
    return x * jax.nn.sigmoid(x)


def _modulated_norm(x, g, shift, scale):
    ms = jnp.mean(x * x, axis=-1, keepdims=True)
    y = x * lax.rsqrt(ms + EPS) * g
    return y * (1.0 + scale) + shift


def _mod_group(i, tile, n_lat, t_len, n_batch):
    return jnp.where(i * tile < n_lat, (i * tile) // t_len, n_batch)


def _adaln_kernel(c_ref, w_ref, b_ref, o_ref):
    s = _silu(c_ref[...])
    o_ref[0] = _dot(s, w_ref[0]) + b_ref[0]


def _adaln(cvec, ada_w, ada_b):
    depth, d, d6 = ada_w.shape
    tn = 1024
    return pl.pallas_call(
        _adaln_kernel,
        name="adaln",
        out_shape=jax.ShapeDtypeStruct((depth, 8, d6), F32),
        grid=(depth, d6 // tn),
        in_specs=[pl.BlockSpec((8, d), lambda l, j: (0, 0)),
                  pl.BlockSpec((1, d, tn), lambda l, j: (l, 0, j)),
                  pl.BlockSpec((1, 1, tn), lambda l, j: (l, 0, j))],
        out_specs=pl.BlockSpec((1, 8, tn), lambda l, j: (l, 0, j)),
        compiler_params=_cparams(("parallel", "parallel")),
    )(cvec, ada_w, ada_b.reshape(depth, 1, d6))


def _inproj_kernel(x_ref, m_ref, g_ref, w_ref, qg_ref, kg_ref,
                   pa_ref, pb_ref, pg_ref, qn_ref, kn_ref, vv_ref, *, d_a5, d_b4, d_c):
    h = _modulated_norm(x_ref[...], g_ref[...], m_ref[0, 0:1, :], m_ref[0, 1:2, :]).astype(BF16)
    o0 = 0
    pa_ref[...] = jnp.dot(h, w_ref[:, o0:o0 + d_a5], preferred_element_type=F32)
    o0 += d_a5
    pb_ref[...] = jnp.dot(h, w_ref[:, o0:o0 + d_b4], preferred_element_type=F32)
    o0 += d_b4
    pg_ref[...] = jnp.dot(h, w_ref[:, o0:o0 + LANES], preferred_element_type=F32)
    o0 += LANES
    gmean = jnp.where(_head_mask(d_c), 1.0 / HEAD_DIM, 0.0).astype(F32)
    q = jnp.dot(h, w_ref[:, o0:o0 + d_c], preferred_element_type=F32)
    q = q * lax.rsqrt(_group_mean_sq(q, gmean) + EPS) * qg_ref[...]
    qn_ref[...] = (q * HEAD_DIM ** -0.5).astype(BF16)
    o0 += d_c
    k = jnp.dot(h, w_ref[:, o0:o0 + d_c], preferred_element_type=F32)
    k = k * lax.rsqrt(_group_mean_sq(k, gmean) + EPS) * kg_ref[...]
    kn_ref[...] = k.astype(BF16)
    o0 += d_c
    vv_ref[...] = jnp.dot(h, w_ref[:, o0:o0 + d_c], preferred_element_type=F32).astype(BF16)


def _inproj(xall, mod, norm_g, w_pad, qn_g, kn_g, dims):
    n_all, d = xall.shape
    n_lat, t_len, n_batch, d_a, d_b, d_c = dims
    tm = ROW_TILE
    row = lambda i: (i, 0)
    const = lambda i: (0, 0)
    grp = lambda i: (_mod_group(i, tm, n_lat, t_len, n_batch), 0, 0)
    kern = functools.partial(_inproj_kernel, d_a5=5 * d_a, d_b4=4 * d_b, d_c=d_c)
    return pl.pallas_call(
        kern,
        name="inproj",
        out_shape=(jax.ShapeDtypeStruct((n_all, 5 * d_a), F32),
                   jax.ShapeDtypeStruct((n_all, 4 * d_b), F32),
                   jax.ShapeDtypeStruct((n_all, LANES), F32),
                   jax.ShapeDtypeStruct((n_all, d_c), BF16),
                   jax.ShapeDtypeStruct((n_all, d_c), BF16),
                   jax.ShapeDtypeStruct((n_all, d_c), BF16)),
        grid=(n_all // tm,),
        in_specs=[pl.BlockSpec((tm, d), row),
                  pl.BlockSpec((1, 6, d), grp),
                  pl.BlockSpec((1, d), const),
                  pl.BlockSpec(w_pad.shape, const),
                  pl.BlockSpec((1, d_c), const),
                  pl.BlockSpec((1, d_c), const)],
        out_specs=(pl.BlockSpec((tm, 5 * d_a), row), pl.BlockSpec((tm, 4 * d_b), row),
                   pl.BlockSpec((tm, LANES), row), pl.BlockSpec((tm, d_c), row),
                   pl.BlockSpec((tm, d_c), row), pl.BlockSpec((tm, d_c), row)),
        compiler_params=_cparams(("parallel",)),
    )(xall, mod, norm_g, w_pad, qn_g, kn_g)


def _scan_block_index(b, d, j, n_lat_blk, n_ctx_blk, n_batch):
    jc = jnp.where(d == 0, j, n_ctx_blk - 1 - j)
    jl = jnp.where(d == 0, j - n_ctx_blk, n_lat_blk - 1 - (j - n_ctx_blk))
    return jnp.where(j < n_ctx_blk, n_batch * n_lat_blk + b * n_ctx_blk + jc, b * n_lat_blk + jl)


def _block_diag(x, mask_bd, n_rep):
    xb = x.astype(BF16)
    return jnp.where(mask_bd, jnp.concatenate([xb] * n_rep, axis=0), jnp.zeros((), BF16))


def _scan_order(idx, d):
    return idx + d * (CHUNK - 1 - 2 * idx)


def _hgrn_kernel(q_ref, z_ref, v_ref, lb_ref, o_ref, st_ref, *, n_chunk):
    d = pl.program_id(1)
    j = pl.program_id(2)
    n = q_ref.shape[-1]
    n_rep = n // HEAD_DIM
    sub = CHUNK // 4

    @pl.when(j == 0)
    def _():
        st_ref[...] = jnp.zeros_like(st_ref)

    lb = lb_ref[0]
    mask_bd = _head_mask(n)
    gones = mask_bd.astype(BF16)
    s_r64 = _scan_order(_iota((CHUNK, CHUNK), 0), d)
    s_c64 = _scan_order(_iota((CHUNK, CHUNK), 1), d)
    m_incl = (s_c64 <= s_r64).astype(F32)
    s_row = _scan_order(_iota((CHUNK, n), 0), d)
    s_col = _scan_order(_iota((CHUNK, n), 1) % HEAD_DIM, d)
    first_half = s_row < CHUNK // 2
    odd_quarter = (s_row // sub) % 2 == 1
    same_half = (s_row // (CHUNK // 2)) == (s_col // (CHUNK // 2))
    in_sub = s_row % sub

    def row_of(x, fwd_idx, bwd_idx):
        return jnp.where(d == 0, x[fwd_idx:fwd_idx + 1, :], x[bwd_idx:bwd_idx + 1, :])

    for c in range(n_chunk):
        ci = c + d * (n_chunk - 1 - 2 * c)
        sl = pl.ds(pl.multiple_of(ci * CHUNK, CHUNK), CHUNK)
        q = q_ref[sl, :] * HEAD_DIM ** -0.5
        z = z_ref[sl, :]
        v = v_ref[sl, :]
        log_f = (jnp.minimum(z, 0.0) - jnp.log1p(jnp.exp(-jnp.abs(z)))) + jnp.log1p(lb * jnp.exp(-z))
        k = (1.0 - lb) * jax.nn.sigmoid(-z)
        bcum = _dot_hi(m_incl, log_f)
        r1 = row_of(bcum, CHUNK // 2 - 1, CHUNK // 2)
        r_a = row_of(bcum, sub - 1, CHUNK - sub)
        r_b = row_of(bcum, 3 * sub - 1, sub)
        r2 = jnp.where(first_half, r_a, r_b)
        qt1 = jnp.where(first_half, 0.0, q * jnp.exp(jnp.minimum(bcum - r1, 0.0)))
        kt1 = jnp.where(first_half, k * jnp.exp(jnp.minimum(r1 - bcum, 0.0)), 0.0)
        qt2 = jnp.where(odd_quarter, q * jnp.exp(jnp.minimum(bcum - r2, 0.0)), 0.0)
        kt2 = jnp.where(odd_quarter, 0.0, k * jnp.exp(jnp.minimum(r2 - bcum, 0.0)))
        att = _dot_nt(qt1, _block_diag(kt1, mask_bd, n_rep))
        att = att + jnp.where(same_half, _dot_nt(qt2, _block_diag(kt2, mask_bd, n_rep)), 0.0)
        o = _dot(att, _block_diag(v, mask_bd, n_rep))
        parts = []
        vs = []
        for off in range(sub):
            if off == 0:
                k_s, b_s, v_s = k, bcum, v
            else:
                shift = off + d * (CHUNK - 2 * off)
                k_s = pltpu.roll(k, shift, 0)
                b_s = pltpu.roll(bcum, shift, 0)
                v_s = pltpu.roll(v, shift, 0)
            p = q * k_s * jnp.exp(jnp.minimum(bcum - b_s, 0.0))
            parts.append(jnp.where(in_sub >= off, p, 0.0).astype(BF16))
            vs.append(v_s)
        a_all = jnp.dot(jnp.concatenate(parts, axis=0), gones, preferred_element_type=F32)
        for off in range(sub):
            o = o + a_all[off * CHUNK:(off + 1) * CHUNK, :] * vs[off]
        st = st_ref[...]
        o = o + _dot_nt(q * jnp.exp(bcum), st)
        b_end = row_of(bcum, CHUNK - 1, 0)
        k_hat = k * jnp.exp(b_end - bcum)
        st_ref[...] = st * jnp.exp(b_end) + jnp.where(mask_bd, _dot_tn(v, k_hat), 0.0)
        o_ref[0, sl, :] = o


def _hgrn_scan(pa, lb2, dims):
    n_all = pa.shape[0]
    n_lat, t_len, n_batch, d_a, d_b, d_c = dims
    n_lat_blk = t_len // TOK_BLK
    n_ctx_blk = (n_all - n_lat) // n_batch // TOK_BLK
    blk = functools.partial(_scan_block_index, n_lat_blk=n_lat_blk, n_ctx_blk=n_ctx_blk, n_batch=n_batch)
    kern = functools.partial(_hgrn_kernel, n_chunk=TOK_BLK // CHUNK)
    return pl.pallas_call(
        kern,
        name="hgrn_scan",
        out_shape=jax.ShapeDtypeStruct((2, n_all, d_a), F32),
        grid=(n_batch, 2, n_lat_blk + n_ctx_blk),
        in_specs=[pl.BlockSpec((TOK_BLK, d_a), lambda b, d, j: (blk(b, d, j), 0)),
                  pl.BlockSpec((TOK_BLK, d_a), lambda b, d, j: (blk(b, d, j), 1 + d)),
                  pl.BlockSpec((TOK_BLK, d_a), lambda b, d, j: (blk(b, d, j), 3)),
                  pl.BlockSpec((1, 1, d_a), lambda b, d, j: (d, 0, 0))],
        out_specs=pl.BlockSpec((1, TOK_BLK, d_a), lambda b, d, j: (d, blk(b, d, j), 0)),
        scratch_shapes=[pltpu.VMEM((d_a, d_a), F32)],
        compiler_params=_cparams(("parallel", "parallel", "arbitrary")),
    )(pa, pa, pa, lb2)


def _gdn_prep_kernel(x_ref, prev_ref, next_ref, pg_ref, cw_ref, alog_ref, dtb_ref, cos_ref, sin_ref,
                     q_ref, k_ref, v_ref, la_ref, be_ref, *, n_lat_blk, n_ctx_blk, n_lat_blks_total, d_b):
    i = pl.program_id(0)
    is_lat = i < n_lat_blks_total
    pos = jnp.where(is_lat, i % n_lat_blk, (i - n_lat_blks_total) % n_ctx_blk)
    n_seq_blk = jnp.where(is_lat, n_lat_blk, n_ctx_blk)
    first = pos == 0
    last = pos == n_seq_blk - 1
    x = x_ref[...]
    halo = prev_ref.shape[0]
    prev = jnp.where(first, 0.0, prev_ref[...])
    nxt = jnp.where(last, 0.0, next_ref[...])
    rows = _iota(x.shape, 0)
    tb = x.shape[0]
    half = CONV_K // 2
    acc = x * cw_ref[half:half + 1, :]
    for s in range(1, half + 1):
        xs = pltpu.roll(x, s, 0)
        for r in range(s):
            xs = jnp.where(rows == r, prev[halo - s + r:halo - s + r + 1, :], xs)
        acc = acc + xs * cw_ref[half - s:half - s + 1, :]
        xs = pltpu.roll(x, tb - s, 0)
        for r in range(s):
            xs = jnp.where(rows == tb - s + r, nxt[r:r + 1, :], xs)
        acc = acc + xs * cw_ref[half + s:half + s + 1, :]
    y = _silu(acc)
    gsum = _head_mask(d_b).astype(F32)

    def l2n(a):
        return a * lax.rsqrt(_dot_hi(a * a, gsum) + EPS)

    lane = _iota((tb, d_b), 1) % HEAD_DIM
    lo = lane < HEAD_DIM // 2

    def rope(a):
        partner = jnp.where(lo, pltpu.roll(a, d_b - HEAD_DIM // 2, 1), pltpu.roll(a, HEAD_DIM // 2, 1))
        return jnp.where(is_lat, a * cos_ref[...] + partner * sin_ref[...], a)

    q_ref[...] = rope(l2n(y[:, 0:d_b])) * HEAD_DIM ** -0.5
    k_ref[...] = rope(l2n(y[:, d_b:2 * d_b]))
    v_ref[...] = y[:, 2 * d_b:3 * d_b]
    n_h = d_b // HEAD_DIM
    g = pg_ref[...]
    e_r = _iota((LANES, d_b), 0)
    e_c = _iota((LANES, d_b), 1) // HEAD_DIM
    for dd in range(2):
        a_x = _dot_hi(g, (e_r == dd * n_h + e_c).astype(F32))
        b_x = _dot_hi(g, (e_r == (2 + dd) * n_h + e_c).astype(F32))
        t = a_x + dtb_ref[dd]
        softplus = jnp.maximum(t, 0.0) + jnp.log1p(jnp.exp(-jnp.abs(t)))
        la_ref[dd] = -jnp.exp(alog_ref[dd]) * softplus
        be_ref[dd] = jax.nn.sigmoid(b_x)


def _gdn_prep(pb, pg, conv_w, alog_x, dtb_x, cos_t, sin_t, dims):
    n_all = pb.shape[0]
    n_lat, t_len, n_batch, d_a, d_b, d_c = dims
    n_lat_blk = t_len // TOK_BLK
    n_ctx_blk = (n_all - n_lat) // n_batch // TOK_BLK
    n_blk = n_all // TOK_BLK
    halo = 8
    per = TOK_BLK // halo
    kern = functools.partial(_gdn_prep_kernel, n_lat_blk=n_lat_blk, n_ctx_blk=n_ctx_blk,
                             n_lat_blks_total=n_lat // TOK_BLK, d_b=d_b)
    row = lambda i: (i, 0)
    const2 = lambda i: (0, 0)
    const3 = lambda i: (0, 0, 0)
    tab = lambda i: (jnp.where(i < n_lat // TOK_BLK, i % n_lat_blk, 0), 0)
    sds = jax.ShapeDtypeStruct
    return pl.pallas_call(
        kern,
        name="gdn_prep",
        out_shape=(sds((n_all, d_b), F32), sds((n_all, d_b), F32), sds((n_all, d_b), F32),
                   sds((2, n_all, d_b), F32), sds((2, n_all, d_b), F32)),
        grid=(n_blk,),
        in_specs=[pl.BlockSpec((TOK_BLK, 3 * d_b), row),
                  pl.BlockSpec((halo, 3 * d_b), lambda i: (jnp.maximum(i * per - 1, 0), 0)),
                  pl.BlockSpec((halo, 3 * d_b), lambda i: (jnp.minimum((i + 1) * per, n_blk * per - 1), 0)),
                  pl.BlockSpec((TOK_BLK, LANES), row),
                  pl.BlockSpec((8, 3 * d_b), const2),
                  pl.BlockSpec((2, 1, d_b), const3),
                  pl.BlockSpec((2, 1, d_b), const3),
                  pl.BlockSpec((TOK_BLK, d_b), tab),
                  pl.BlockSpec((TOK_BLK, d_b), tab)],
        out_specs=(pl.BlockSpec((TOK_BLK, d_b), row), pl.BlockSpec((TOK_BLK, d_b), row),
                   pl.BlockSpec((TOK_BLK, d_b), row),
                   pl.BlockSpec((2, TOK_BLK, d_b), lambda i: (0, i, 0)),
                   pl.BlockSpec((2, TOK_BLK, d_b), lambda i: (0, i, 0))),
        compiler_params=_cparams(("parallel",)),
    )(pb, pb, pb, pg, conv_w, alog_x, dtb_x, cos_t, sin_t)


def _gdn_kernel(q_ref, k_ref, v_ref, la_ref, be_ref, o_ref, st_ref, *, n_chunk):
    d = pl.program_id(1)
    j = pl.program_id(2)
    n = q_ref.shape[-1]
    n_rep = n // HEAD_DIM

    @pl.when(j == 0)
    def _():
        st_ref[...] = jnp.zeros_like(st_ref)

    mask_bd = _head_mask(n)
    s_r64 = _scan_order(_iota((CHUNK, CHUNK), 0), d)
    s_c64 = _scan_order(_iota((CHUNK, CHUNK), 1), d)
    m_incl = (s_c64 <= s_r64).astype(F32)
    ones64 = jnp.ones((CHUNK, CHUNK), F32)
    s_row = _scan_order(_iota((CHUNK, n), 0), d)
    s_col = _scan_order(_iota((CHUNK, n), 1) % HEAD_DIM, d)
    incl = s_col <= s_row
    strict = s_col < s_row
    m_before = (s_row <= s_col).astype(F32)
    eye = s_col == s_row
    bd = lambda a: _block_diag(a, mask_bd, n_rep)

    def row_of(x, fwd_idx, bwd_idx):
        return jnp.where(d == 0, x[fwd_idx:fwd_idx + 1, :], x[bwd_idx:bwd_idx + 1, :])

    pre = []
    for c in range(n_chunk):
        ci = c + d * (n_chunk - 1 - 2 * c)
        sl = pl.ds(pl.multiple_of(ci * CHUNK, CHUNK), CHUNK)
        q = q_ref[sl, :]
        k = k_ref[sl, :]
        v = v_ref[sl, :]
        la = la_ref[0, sl, :]
        be = be_ref[0, sl, :]
        g_t = _dot_hi(m_incl, la)
        g_s = _dot_hi(ones64, la * m_before)
        gam = jnp.where(incl, jnp.exp(jnp.minimum(g_t - g_s, 0.0)), 0.0)
        k_bd = bd(k)
        kk = _dot_nt(k, k_bd)
        qk = _dot_nt(q, k_bd)
        a = jnp.where(strict, be * kk * gam, 0.0)
        t_inv = jnp.where(eye, 1.0, 0.0) - jnp.where((s_row // 2) == (s_col // 2), a, 0.0)
        m = 4
        while m <= CHUNK:
            off = jnp.where(((s_row // m) == (s_col // m)) & ((s_row // (m // 2)) != (s_col // (m // 2))), a, 0.0)
            t_inv = t_inv - _dot(t_inv, bd(_dot(off, bd(t_inv))))
            m *= 2
        eg = jnp.exp(g_t)
        w = _dot(t_inv, bd(be * eg * k))
        u0 = _dot(t_inv, bd(be * v))
        g_end = row_of(g_t, CHUNK - 1, 0)
        pre.append((sl, q * eg, qk * gam, w, u0, k * jnp.exp(g_end - g_t), jnp.exp(g_end)))

    st = st_ref[...]
    for sl, q_hat, qk_g, w, u0, k_hat, e_end in pre:
        u = u0 - _dot_nt(w, st)
        o_ref[0, sl, :] = _dot_nt(q_hat, st) + _dot(qk_g, bd(u))
        st = st * e_end + jnp.where(mask_bd, _dot_tn(u, k_hat), 0.0)
    st_ref[...] = st


def _gdn_scan(qb, kb, vb, la, be, dims):
    n_all = qb.shape[0]
    n_lat, t_len, n_batch, d_a, d_b, d_c = dims
    n_lat_blk = t_len // TOK_BLK
    n_ctx_blk = (n_all - n_lat) // n_batch // TOK_BLK
    blk = functools.partial(_scan_block_index, n_lat_blk=n_lat_blk, n_ctx_blk=n_ctx_blk, n_batch=n_batch)
    kern = functools.partial(_gdn_kernel, n_chunk=TOK_BLK // CHUNK)
    tok = pl.BlockSpec((TOK_BLK, d_b), lambda b, d, j: (blk(b, d, j), 0))
    dtok = pl.BlockSpec((1, TOK_BLK, d_b), lambda b, d, j: (d, blk(b, d, j), 0))
    return pl.pallas_call(
        kern,
        name="gdn_scan",
        out_shape=jax.ShapeDtypeStruct((2, n_all, d_b), F32),
        grid=(n_batch, 2, n_lat_blk + n_ctx_blk),
        in_specs=[tok, tok, tok, dtok, dtok],
        out_specs=dtok,
        scratch_shapes=[pltpu.VMEM((d_b, d_b), F32)],
        compiler_params=_cparams(("parallel", "parallel", "arbitrary")),
    )(qb, kb, vb, la, be)


def _stack_heads(q, n_rep):
    m, n = q.shape
    keep = (_iota((n_rep * m, n), 0) // m) == (_iota((n_rep * m, n), 1) // HEAD_DIM)
    return jnp.where(keep, jnp.concatenate([q] * n_rep, axis=0), jnp.zeros((), q.dtype))


def _fold_heads(o, m, n_rep):
    n = o.shape[1]
    lane_h = _iota((m, n), 1) // HEAD_DIM
    acc = jnp.zeros((m, n), F32)
    for h in range(n_rep):
        acc = acc + jnp.where(lane_h == h, o[h * m:(h + 1) * m, :], 0.0)
    return acc


def _natten_kernel(q_ref, k_ref, v_ref, kc_ref, vc_ref, bias_ref, o_ref, *, n_rows):
    r = pl.program_id(1)
    n = q_ref.shape[-1]
    n_rep = n // HEAD_DIM
    kr = min(WIN_R, n_rows)
    r0 = jnp.clip(r - WIN_R // 2, 0, n_rows - kr)
    sl = pl.ds(pl.multiple_of(r0 * GRID_W, GRID_W), kr * GRID_W)
    qs = _stack_heads(q_ref[...], n_rep)
    dn = (((1,), (1,)), ((), ()))
    s_loc = lax.dot_general(qs, k_ref[sl, :], dn, preferred_element_type=F32) + bias_ref[0]
    s_ctx = lax.dot_general(qs, kc_ref[...], dn, preferred_element_type=F32)
    m = jnp.maximum(jnp.max(s_loc, axis=-1, keepdims=True), jnp.max(s_ctx, axis=-1, keepdims=True))
    p_loc = jnp.exp(s_loc - m)
    p_ctx = jnp.exp(s_ctx - m)
    inv = 1.0 / (jnp.sum(p_loc, axis=-1, keepdims=True) + jnp.sum(p_ctx, axis=-1, keepdims=True))
    o = jnp.dot((p_loc * inv).astype(BF16), v_ref[sl, :], preferred_element_type=F32)
    o = o + jnp.dot((p_ctx * inv).astype(BF16), vc_ref[...], preferred_element_type=F32)
    o_ref[...] = _fold_heads(o, q_ref.shape[0], n_rep)


def _natten(qn, kn, vv, bias_tab, dims):
    n_lat, t_len, n_batch, d_a, d_b, d_c = dims
    n_all = qn.shape[0]
    l_ctx = (n_all - n_lat) // n_batch
    n_rows = t_len // GRID_W
    kr = min(WIN_R, n_rows)
    ctx0 = n_lat // l_ctx

    def cfg(b, r):
        r0 = jnp.clip(r - WIN_R // 2, 0, n_rows - kr)
        return (r - r0, 0, 0)

    kern = functools.partial(_natten_kernel, n_rows=n_rows)
    return pl.pallas_call(
        kern,
        name="natten",
        out_shape=jax.ShapeDtypeStruct((n_lat, d_c), F32),
        grid=(n_batch, n_rows),
        in_specs=[pl.BlockSpec((GRID_W, d_c), lambda b, r: (b * n_rows + r, 0)),
                  pl.BlockSpec((t_len, d_c), lambda b, r: (b, 0)),
                  pl.BlockSpec((t_len, d_c), lambda b, r: (b, 0)),
                  pl.BlockSpec((l_ctx, d_c), lambda b, r: (ctx0 + b, 0)),
                  pl.BlockSpec((l_ctx, d_c), lambda b, r: (ctx0 + b, 0)),
                  pl.BlockSpec((1,) + bias_tab.shape[1:], cfg)],
        out_specs=pl.BlockSpec((GRID_W, d_c), lambda b, r: (b * n_rows + r, 0)),
        compiler_params=_cparams(("parallel", "arbitrary")),
    )(qn, kn, vv, kn, vv, bias_tab)


def _ctx_attn_kernel(q_ref, k_ref, v_ref, o_ref):
    n = q_ref.shape[-1]
    n_rep = n // HEAD_DIM
    qs = _stack_heads(q_ref[...], n_rep)
    s = lax.dot_general(qs, k_ref[...], (((1,), (1,)), ((), ())), preferred_element_type=F32)
    p = jnp.exp(s - jnp.max(s, axis=-1, keepdims=True))
    p = p * (1.0 / jnp.sum(p, axis=-1, keepdims=True))
    o = jnp.dot(p.astype(BF16), v_ref[...], preferred_element_type=F32)
    o_ref[...] = _fold_heads(o, q_ref.shape[0], n_rep)


def _ctx_attn(qn, kn, vv, dims):
    n_lat, t_len, n_batch, d_a, d_b, d_c = dims
    n_all = qn.shape[0]
    l_ctx = (n_all - n_lat) // n_batch
    tq = 64
    per = l_ctx // tq
    q0 = n_lat // tq
    c0 = n_lat // l_ctx
    return pl.pallas_call(
        _ctx_attn_kernel,
        name="ctx_attn",
        out_shape=jax.ShapeDtypeStruct((n_all - n_lat, d_c), F32),
        grid=(n_batch, per),
        in_specs=[pl.BlockSpec((tq, d_c), lambda b, i: (q0 + b * per + i, 0)),
                  pl.BlockSpec((l_ctx, d_c), lambda b, i: (c0 + b, 0)),
                  pl.BlockSpec((l_ctx, d_c), lambda b, i: (c0 + b, 0))],
        out_specs=pl.BlockSpec((tq, d_c), lambda b, i: (b * per + i, 0)),
        compiler_params=_cparams(("parallel", "arbitrary")),
    )(qn, kn, vv)


def _natten_bias(rpb, n_rows):
    n_h = rpb.shape[0]
    kr = min(WIN_R, n_rows)
    cols = jnp.arange(GRID_W)
    c0 = jnp.clip(cols - WIN_C // 2, 0, GRID_W - WIN_C)
    kc = jnp.arange(GRID_W)
    in_win = (kc[None, :] >= c0[:, None]) & (kc[None, :] < c0[:, None] + WIN_C)
    dcol = jnp.clip(kc[None, :] - cols[:, None] + WIN_C - 1, 0, 2 * WIN_C - 2)
    tabs = []
    for delta in range(kr):
        drow = jnp.arange(kr) - delta + WIN_R - 1
        b = rpb[:, drow][:, :, dcol]
        b = jnp.where(in_win[None, None], b.astype(F32), NEG_BIG)
        tabs.append(b.transpose(0, 2, 1, 3).reshape(n_h * GRID_W, kr * GRID_W))
    return jnp.stack(tabs)


def _outproj_kernel(x_ref, m_ref, oa_ref, ga_ref, ob_ref, gb_ref, oc_ref, na_ref, nb_ref, w_ref, o_ref,
                    *, d_a, d_b):
    gmean_a = jnp.where(_head_mask(d_a), 1.0 / HEAD_DIM, 0.0).astype(F32)
    oa = oa_ref[0] + oa_ref[1]
    ya = oa * lax.rsqrt(_group_mean_sq(oa, gmean_a) + EPS) * na_ref[...] * _silu(ga_ref[...])
    gmean_b = jnp.where(_head_mask(d_b), 1.0 / HEAD_DIM, 0.0).astype(F32)
    ob = ob_ref[0] + ob_ref[1]
    yb = ob * lax.rsqrt(_group_mean_sq(ob, gmean_b) + EPS) * nb_ref[...] * _silu(gb_ref[...])
    acc = _dot(ya, w_ref[0:d_a, :])
    acc = acc + _dot(yb, w_ref[d_a:d_a + d_b, :])
    acc = acc + _dot(oc_ref[...], w_ref[d_a + d_b:, :])
    o_ref[...] = x_ref[...] + m_ref[0, 2:3, :] * acc


def _outproj(xall, mod, oa, pa, ob, pb, oc, na_g, nb_g, w_out, n_rows_out, dims):
    n_lat, t_len, n_batch, d_a, d_b, d_c = dims
    d = xall.shape[1]
    tm = ROW_TILE
    row = lambda i: (i, 0)
    const = lambda i: (0, 0)
    grp = lambda i: (_mod_group(i, tm, n_lat, t_len, n_batch), 0, 0)
    kern = functools.partial(_outproj_kernel, d_a=d_a, d_b=d_b)
    return pl.pallas_call(
        kern,
        name="outproj",
        out_shape=jax.ShapeDtypeStruct((n_rows_out, d), F32),
        grid=(n_rows_out // tm,),
        in_specs=[pl.BlockSpec((tm, d), row),
                  pl.BlockSpec((1, 6, d), grp),
                  pl.BlockSpec((2, tm, d_a), lambda i: (0, i, 0)),
                  pl.BlockSpec((tm, d_a), lambda i: (i, 4)),
                  pl.BlockSpec((2, tm, d_b), lambda i: (0, i, 0)),
                  pl.BlockSpec((tm, d_b), lambda i: (i, 3)),
                  pl.BlockSpec((tm, d_c), row),
                  pl.BlockSpec((1, d_a), const),
                  pl.BlockSpec((1, d_b), const),
                  pl.BlockSpec(w_out.shape, const)],
        out_specs=pl.BlockSpec((tm, d), row),
        compiler_params=_cparams(("parallel",)),
    )(xall, mod, oa, pa, ob, pb, oc, na_g, nb_g, w_out)


def _ffn_kernel(x_ref, m_ref, g_ref, w1_ref, w3_ref, w2_ref, o_ref, h_ref, acc_ref):
    j = pl.program_id(1)

    @pl.when(j == 0)
    def _():
        h_ref[...] = _modulated_norm(x_ref[...], g_ref[...], m_ref[0, 3:4, :], m_ref[0, 4:5, :]).astype(BF16)
        acc_ref[...] = jnp.zeros_like(acc_ref)

    h = h_ref[...]
    a = jnp.dot(h, w1_ref[...], preferred_element_type=F32)
    b = jnp.dot(h, w3_ref[...], preferred_element_type=F32)
    acc_ref[...] += _dot(_silu(a) * b, w2_ref[...])

    @pl.when(j == pl.num_programs(1) - 1)
    def _():
        o_ref[...] = x_ref[...] + m_ref[0, 5:6, :] * acc_ref[...]


def _ffn(xall, mod, norm_g, w1, w3, w2, dims):
    n_lat, t_len, n_batch, d_a, d_b, d_c = dims
    n_rows, d = xall.shape
    d_ff = w1.shape[1]
    tm, tf = ROW_TILE, FF_TILE
    grp = lambda i, j: (_mod_group(i, tm, n_lat, t_len, n_batch), 0, 0)
    return pl.pallas_call(
        _ffn_kernel,
        name="ffn",
        out_shape=jax.ShapeDtypeStruct((n_rows, d), F32),
        grid=(n_rows // tm, d_ff // tf),
        in_specs=[pl.BlockSpec((tm, d), lambda i, j: (i, 0)),
                  pl.BlockSpec((1, 6, d), grp),
                  pl.BlockSpec((1, d), lambda i, j: (0, 0)),
                  pl.BlockSpec((d, tf), lambda i, j: (0, j)),
                  pl.BlockSpec((d, tf), lambda i, j: (0, j)),
                  pl.BlockSpec((tf, d), lambda i, j: (j, 0))],
        out_specs=pl.BlockSpec((tm, d), lambda i, j: (i, 0)),
        scratch_shapes=[pltpu.VMEM((tm, d), BF16), pltpu.VMEM((tm, d), F32)],
        compiler_params=_cparams(("parallel", "arbitrary")),
    )(xall, mod, norm_g, w1, w3, w2)


def _route_kernel(x_ref, m_ref, g_ref, wr_ref, br_ref, h_ref, e_ref, gt_ref):
    h = _modulated_norm(x_ref[...], g_ref[...], m_ref[0, 3:4, :], m_ref[0, 4:5, :])
    h_ref[...] = h
    lane = _iota((h.shape[0], LANES), 1)
    logits = jnp.where(lane < N_EXPERTS, _dot_hi(h, wr_ref[...]) + br_ref[...], -jnp.inf)
    m1 = jnp.max(logits, axis=-1, keepdims=True)
    lane_f = lane.astype(F32)
    i1 = jnp.min(jnp.where(logits == m1, lane_f, float(LANES)), axis=-1, keepdims=True).astype(jnp.int32)
    rest = jnp.where(lane == i1, -jnp.inf, logits)
    m2 = jnp.max(rest, axis=-1, keepdims=True)
    i2 = jnp.min(jnp.where(rest == m2, lane_f, float(LANES)), axis=-1, keepdims=True).astype(jnp.int32)
    e2 = jnp.exp(m2 - m1)
    g1 = 1.0 / (1.0 + e2)
    g2 = e2 / (1.0 + e2)
    e_ref[...] = jnp.where(lane == 0, i1, jnp.where(lane == 1, i2, 0))
    gt_ref[...] = jnp.where(lane == 0, g1, jnp.where(lane == 1, g2, 0.0))


def _route(x, mod, norm_g, wr_pad, br_pad, dims):
    n_lat, t_len, n_batch, d_a, d_b, d_c = dims
    n, d = x.shape
    tm = ROW_TILE
    row = lambda i: (i, 0)
    const = lambda i: (0, 0)
    grp = lambda i: (_mod_group(i, tm, n_lat, t_len, n_batch), 0, 0)
    return pl.pallas_call(
        _route_kernel,
        name="moe_route",
        out_shape=(jax.ShapeDtypeStruct((n, d), F32), jax.ShapeDtypeStruct((n, LANES), jnp.int32),
                   jax.ShapeDtypeStruct((n, LANES), F32)),
        grid=(n // tm,),
        in_specs=[pl.BlockSpec((tm, d), row), pl.BlockSpec((1, 6, d), grp), pl.BlockSpec((1, d), const),
                  pl.BlockSpec((d, LANES), const), pl.BlockSpec((1, LANES), const)],
        out_specs=(pl.BlockSpec((tm, d), row), pl.BlockSpec((tm, LANES), row), pl.BlockSpec((tm, LANES), row)),
        compiler_params=_cparams(("parallel",)),
    )(x, mod, norm_g, wr_pad, br_pad)


def _row_copy(src_hbm, row, dst_vmem, r, sem):
    return pltpu.make_async_copy(src_hbm.at[pl.ds(row, 1)], dst_vmem.at[pl.ds(r, 1)], sem)


def _gather_kernel(idx_ref, h_hbm, o_ref, sem):
    n_rows = o_ref.shape[0]

    def start(r, carry):
        _row_copy(h_hbm, idx_ref[0, 0, r], o_ref, r, sem).start()
        return carry

    lax.fori_loop(0, n_rows, start, 0)

    def wait(r, carry):
        _row_copy(h_hbm, 0, o_ref, r, sem).wait()
        return carry

    lax.fori_loop(0, n_rows, wait, 0)


def _gather_rows(h, slot_tok, n_blocks):
    d = h.shape[1]
    bm = MOE_ROWS
    return pl.pallas_call(
        _gather_kernel,
        name="moe_gather",
        out_shape=jax.ShapeDtypeStruct((n_blocks * bm, d), h.dtype),
        grid=(n_blocks,),
        in_specs=[pl.BlockSpec((1, 1, bm), lambda i: (i, 0, 0), memory_space=pltpu.SMEM),
                  pl.BlockSpec(memory_space=pl.ANY)],
        out_specs=pl.BlockSpec((bm, d), lambda i: (i, 0)),
        scratch_shapes=[pltpu.SemaphoreType.DMA(())],
        compiler_params=_cparams(("arbitrary",)),
    )(slot_tok.reshape(n_blocks, 1, bm), h)


def _expert_kernel(be_ref, nv_ref, x_ref, w1_ref, w3_ref, w2_ref, o_ref, h_ref, acc_ref):
    i = pl.program_id(0)
    j = pl.program_id(1)
    live = i < nv_ref[0]

    @pl.when(live & (j == 0))
    def _():
        h_ref[...] = x_ref[...].astype(BF16)
        acc_ref[...] = jnp.zeros_like(acc_ref)

    @pl.when(live)
    def _():
        h = h_ref[...]
        a = _dot(h, w1_ref[0])
        b = _dot(h, w3_ref[0])
        acc_ref[...] += _dot(_silu(a) * b, w2_ref[0])

    last = j == pl.num_programs(1) - 1

    @pl.when(live & last)
    def _():
        o_ref[...] = acc_ref[...]

    @pl.when(jnp.logical_not(live) & last)
    def _():
        o_ref[...] = jnp.zeros_like(o_ref)


def _experts(xg, block_e, n_live, w1, w3, w2):
    n_slots, d = xg.shape
    d_ff = w1.shape[2]
    bm, tf = MOE_ROWS, FF_TILE
    grid_spec = pltpu.PrefetchScalarGridSpec(
        num_scalar_prefetch=2,
        grid=(n_slots // bm, d_ff // tf),
        in_specs=[pl.BlockSpec((bm, d), lambda i, j, be, nv: (i, 0)),
                  pl.BlockSpec((1, d, tf), lambda i, j, be, nv: (be[i], 0, j)),
                  pl.BlockSpec((1, d, tf), lambda i, j, be, nv: (be[i], 0, j)),
                  pl.BlockSpec((1, tf, d), lambda i, j, be, nv: (be[i], j, 0))],
        out_specs=pl.BlockSpec((bm, d), lambda i, j, be, nv: (i, 0)),
        scratch_shapes=[pltpu.VMEM((bm, d), BF16), pltpu.VMEM((bm, d), F32)])
    return pl.pallas_call(
        _expert_kernel,
        name="moe_experts",
        out_shape=jax.ShapeDtypeStruct((n_slots, d), F32),
        grid_spec=grid_spec,
        compiler_params=_cparams(("arbitrary", "arbitrary")),
    )(block_e, n_live, xg, w1, w3, w2)


def _combine_kernel(idx_ref, x_ref, m_ref, gt_ref, y_hbm, o_ref, buf_ref, sem):
    tm = x_ref.shape[0]

    def start(r, carry):
        _row_copy(y_hbm, idx_ref[0, 0, 2 * r], buf_ref.at[0], r, sem).start()
        _row_copy(y_hbm, idx_ref[0, 0, 2 * r + 1], buf_ref.at[1], r, sem).start()
        return carry

    lax.fori_loop(0, tm, start, 0)

    def wait(r, carry):
        _row_copy(y_hbm, 0, buf_ref.at[0], r, sem).wait()
        _row_copy(y_hbm, 0, buf_ref.at[1], r, sem).wait()
        return carry

    lax.fori_loop(0, tm, wait, 0)
    gt = gt_ref[...]
    y = gt[:, 0:1] * buf_ref[0] + gt[:, 1:2] * buf_ref[1]
    o_ref[...] = x_ref[...] + m_ref[0, 5:6, :] * y


def _combine(x, mod, gates, slots, y, dims):
    n_lat, t_len, n_batch, d_a, d_b, d_c = dims
    n, d = x.shape
    tm = CMB_ROWS
    grp = lambda i: (_mod_group(i, tm, n_lat, t_len, n_batch), 0, 0)
    return pl.pallas_call(
        _combine_kernel,
        name="moe_combine",
        out_shape=jax.ShapeDtypeStruct((n, d), F32),
        grid=(n // tm,),
        in_specs=[pl.BlockSpec((1, 1, 2 * tm), lambda i: (i, 0, 0), memory_space=pltpu.SMEM),
                  pl.BlockSpec((tm, d), lambda i: (i, 0)),
                  pl.BlockSpec((1, 6, d), grp),
                  pl.BlockSpec((tm, LANES), lambda i: (i, 0)),
                  pl.BlockSpec(memory_space=pl.ANY)],
        out_specs=pl.BlockSpec((tm, d), lambda i: (i, 0)),
        scratch_shapes=[pltpu.VMEM((2, tm, d), F32), pltpu.SemaphoreType.DMA(())],
        compiler_params=_cparams(("arbitrary",)),
    )(slots.reshape(n // tm, 1, 2 * tm), x, mod, gates, y)


def _moe(x, mod, norm_g, w_router, b_router, w1, w3, w2, dims):
    n, d = x.shape
    wr_pad = jnp.zeros((d, LANES), F32).at[:, :N_EXPERTS].set(w_router.astype(F32))
    br_pad = jnp.zeros((1, LANES), F32).at[0, :N_EXPERTS].set(b_router.astype(F32))
    h, e_tile, g_tile = _route(x, mod, norm_g, wr_pad, br_pad, dims)
    bm = MOE_ROWS
    e_flat = e_tile[:, :2].reshape(-1)
    onehot = (e_flat[:, None] == jnp.arange(N_EXPERTS, dtype=jnp.int32)[None, :]).astype(jnp.int32)
    csum = jnp.cumsum(onehot, axis=0)
    counts = csum[-1]
    rank = jnp.sum(csum * onehot, axis=1) - 1
    padded = (counts + bm - 1) // bm * bm
    pad_end = jnp.cumsum(padded)
    pad_start = pad_end - padded
    slot = pad_start[e_flat] + rank
    n_blocks = (2 * n) // bm + N_EXPERTS
    tok_flat = jnp.arange(2 * n, dtype=jnp.int32) // 2
    slot_tok = jnp.zeros((n_blocks * bm,), jnp.int32).at[slot].set(tok_flat)
    block_e = jnp.minimum(jnp.searchsorted(pad_end, jnp.arange(n_blocks, dtype=jnp.int32) * bm, side='right'),
                          N_EXPERTS - 1).astype(jnp.int32)
    n_live = (pad_end[-1] // bm).astype(jnp.int32).reshape(1)
    xg = _gather_rows(h, slot_tok, n_blocks)
    y = _experts(xg, block_e, n_live, w1, w3, w2)
    return _combine(x, mod, g_tile, slot.astype(jnp.int32), y, dims)


def _rope_tables(t_len, n_heads):
    t = jnp.arange(t_len)
    row = (t // GRID_W).astype(F32)
    col = (t % GRID_W).astype(F32)
    n_freq = HEAD_DIM // 4
    inv = ROPE_BASE ** (-jnp.arange(n_freq, dtype=F32) / n_freq)
    ang = jnp.concatenate([row[:, None] * inv, col[:, None] * inv], axis=-1)
    cos, sin = jnp.cos(ang), jnp.sin(ang)
    cos_h = jnp.concatenate([cos, cos], axis=-1)
    sin_h = jnp.concatenate([-sin, sin], axis=-1)
    return jnp.tile(cos_h, (1, n_heads)), jnp.tile(sin_h, (1, n_heads))


def kernel(x, c, ctx, c_ctx, ada_w, ada_b, norm1_g, norm2_g, w_in, w_out, hgrn_lb_raw, hgrn_onorm_g,
           gdn_conv_w, gdn_a_log, gdn_dt_bias, gdn_onorm_g, na_qnorm_g, na_knorm_g, na_rpb, ffn_w1, ffn_w3,
           ffn_w2, moe_router_w, moe_router_b, moe_w1, moe_w3, moe_w2):
    n_batch, t_len, d = x.shape
    l_ctx = ctx.shape[1]
    depth = w_in.shape[0]
    d_a = hgrn_lb_raw.shape[-1]
    d_b = gdn_conv_w.shape[-1] // 3
    n_hb = gdn_a_log.shape[-1]
    n_hc = na_rpb.shape[1]
    d_c = n_hc * HEAD_DIM
    n_lat = n_batch * t_len
    n_all = n_lat + n_batch * l_ctx
    dims = (n_lat, t_len, n_batch, d_a, d_b, d_c)
    assert d_a % HEAD_DIM == 0 and d_b == n_hb * HEAD_DIM
    assert t_len % ROW_TILE == 0 and (n_batch * l_ctx) % ROW_TILE == 0 and l_ctx % TOK_BLK == 0
    assert t_len % GRID_W == 0 and 4 * n_hb <= LANES

    cvec = jnp.zeros((8, d), F32).at[:n_batch].set(c.astype(F32)).at[n_batch].set(c_ctx.astype(F32))
    mod_all = _adaln(cvec, ada_w, ada_b)

    lb_soft = jax.nn.softmax(hgrn_lb_raw.astype(F32), axis=1)
    lower_bound = jnp.cumsum(lb_soft, axis=1) - lb_soft[:, :1]
    cos_t, sin_t = _rope_tables(t_len, n_hb)
    n_gate = 4 * n_hb
    sizes_a, sizes_b = 5 * d_a, 4 * d_b

    xall = jnp.concatenate([x.reshape(n_lat, d), ctx.reshape(n_batch * l_ctx, d)], axis=0).astype(F32)
    for l in range(depth):
        last = l == depth - 1
        mod = mod_all[l, :n_batch + 1].reshape(n_batch + 1, 6, d)
        w = w_in[l]
        gate_cols = jnp.pad(w[:, sizes_a + sizes_b:sizes_a + sizes_b + n_gate], ((0, 0), (0, LANES - n_gate)))
        w_pad = jnp.concatenate([w[:, :sizes_a + sizes_b], gate_cols, w[:, sizes_a + sizes_b + n_gate:]],
                                axis=1).astype(BF16)
        qn_g = jnp.tile(na_qnorm_g[l].astype(F32), n_hc)[None]
        kn_g = jnp.tile(na_knorm_g[l].astype(F32), n_hc)[None]
        pa, pb, pg, qn, kn, vv = _inproj(xall, mod, norm1_g[l][None].astype(F32), w_pad, qn_g, kn_g, dims)

        oa = _hgrn_scan(pa, lower_bound[:, l][:, None, :], dims)
        conv_w = jnp.zeros((8, 3 * d_b), F32).at[:CONV_K].set(gdn_conv_w[l].astype(F32))
        alog_x = jnp.repeat(gdn_a_log[l].astype(F32), HEAD_DIM, axis=-1)[:, None, :]
        dtb_x = jnp.repeat(gdn_dt_bias[l].astype(F32), HEAD_DIM, axis=-1)[:, None, :]
        qb, kb, vb, la, be = _gdn_prep(pb, pg, conv_w, alog_x, dtb_x, cos_t, sin_t, dims)
        ob = _gdn_scan(qb, kb, vb, la, be, dims)
        bias_tab = _natten_bias(na_rpb[l], t_len // GRID_W)
        oc = _natten(qn, kn, vv, bias_tab, dims)
        n_out = n_lat if last else n_all
        if not last:
            oc = jnp.concatenate([oc, _ctx_attn(qn, kn, vv, dims)], axis=0)
        na_g = jnp.tile(hgrn_onorm_g[l].astype(F32), d_a // HEAD_DIM)[None]
        nb_g = jnp.tile(gdn_onorm_g[l].astype(F32), n_hb)[None]
        xall_mid = _outproj(xall, mod, oa, pa, ob, pb, oc, na_g, nb_g, w_out[l].astype(BF16), n_out, dims)
        i = l // 2
        if l % 2 == 0:
            xall = _ffn(xall_mid, mod, norm2_g[l][None].astype(F32), ffn_w1[i].astype(BF16),
                        ffn_w3[i].astype(BF16), ffn_w2[i].astype(BF16), dims)
        else:
            xall = _moe(xall_mid, mod, norm2_g[l][None].astype(F32), moe_router_w[i], moe_router_b[i],
                        moe_w1[i], moe_w3[i], moe_w2[i], dims)
    return xall[:n_lat].reshape(n_batch, t_len, d).astype(x.dtype)
```

```python
import functools
import math

import jax
import jax.numpy as jnp
import numpy as np
from jax import lax
from jax.experimental import pallas as pl
from jax.experimental.pallas import tpu as pltpu

F32 = jnp.float32
BF16 = jnp.bfloat16
HI = lax.Precision.HIGHEST

EPS = 1e-6
HEAD_DIM = 64
CHUNK = 64
TOK_BLK = 256
GRID_W = 64
WIN_R = 8
WIN_C = 16
CONV_K = 5
ROPE_BASE = 10000.0
N_EXPERTS = 8
LANES = 128
ROW_TILE = 512
FF_TILE = 512
MOE_ROWS = 512
CMB_ROWS = 256
VMEM_LIMIT = 56 * 1024 * 1024
NEG_BIG = -1e30


def _cparams(sem):
    return pltpu.CompilerParams(dimension_semantics=sem, vmem_limit_bytes=VMEM_LIMIT)


def _dot(a, b):
    return jnp.dot(a.astype(BF16), b.astype(BF16), preferred_element_type=F32)


def _dot_nt(a, b):
    return lax.dot_general(a.astype(BF16), b.astype(BF16), (((1,), (1,)), ((), ())),
                           preferred_element_type=F32)


def _dot_tn(a, b):
    return lax.dot_general(a.astype(BF16), b.astype(BF16), (((0,), (0,)), ((), ())),
                           preferred_element_type=F32)


def _dot_hi(a, b):
    return jnp.dot(a, b, precision=HI, preferred_element_type=F32)


def _iota(shape, dim):
    return lax.broadcasted_iota(jnp.int32, shape, dim)


def _head_mask(n):
    return (_iota((n, n), 0) // HEAD_DIM) == (_iota((n, n), 1) // HEAD_DIM)


def _group_mean_sq(x, gmean):
    return _dot_hi(x * x, gmean)


def _silu(x):
    return x * jax.nn.sigmoid(x)


def _modulated_norm(x, g, shift, scale):
    ms = jnp.mean(x * x, axis=-1, keepdims=True)
    y = x * lax.rsqrt(ms + EPS) * g
    return y * (1.0 + scale) + shift


def _mod_group(i, tile, n_lat, t_len, n_batch):
    return jnp.where(i * tile < n_lat, (i * tile) // t_len, n_batch)


def _adaln_kernel(c_ref, w_ref, b_ref, o_ref):
    s = _silu(c_ref[...])
    o_ref[0] = _dot(s, w_ref[0]) + b_ref[0]


def _adaln(cvec, ada_w, ada_b):
    depth, d, d6 = ada_w.shape
    tn = 1024
    return pl.pallas_call(
        _adaln_kernel,
        name="adaln",
        out_shape=jax.ShapeDtypeStruct((depth, 8, d6), F32),
        grid=(depth, d6 // tn),
        in_specs=[pl.BlockSpec((8, d), lambda l, j: (0, 0)),
                  pl.BlockSpec((1, d, tn), lambda l, j: (l, 0, j)),
                  pl.BlockSpec((1, 1, tn), lambda l, j: (l, 0, j))],
        out_specs=pl.BlockSpec((1, 8, tn), lambda l, j: (l, 0, j)),
        compiler_params=_cparams(("parallel", "parallel")),
    )(cvec, ada_w, ada_b.reshape(depth, 1, d6))


def _inproj_kernel(x_ref, m_ref, g_ref, w_ref, qg_ref, kg_ref,
                   pa_ref, pb_ref, pg_ref, qn_ref, kn_ref, vv_ref, *, d_a5, d_b4, d_c):
    h = _modulated_norm(x_ref[...], g_ref[...], m_ref[0, 0:1, :], m_ref[0, 1:2, :]).astype(BF16)
    o0 = 0
    pa_ref[...] = jnp.dot(h, w_ref[:, o0:o0 + d_a5], preferred_element_type=F32)
    o0 += d_a5
    pb_ref[...] = jnp.dot(h, w_ref[:, o0:o0 + d_b4], preferred_element_type=F32)
    o0 += d_b4
    pg_ref[...] = jnp.dot(h, w_ref[:, o0:o0 + LANES], preferred_element_type=F32)
    o0 += LANES
    gmean = jnp.where(_head_mask(d_c), 1.0 / HEAD_DIM, 0.0).astype(F32)
    q = jnp.dot(h, w_ref[:, o0:o0 + d_c], preferred_element_type=F32)
    q = q * lax.rsqrt(_group_mean_sq(q, gmean) + EPS) * qg_ref[...]
    qn_ref[...] = (q * HEAD_DIM ** -0.5).astype(BF16)
    o0 += d_c
    k = jnp.dot(h, w_ref[:, o0:o0 + d_c], preferred_element_type=F32)
    k = k * lax.rsqrt(_group_mean_sq(k, gmean) + EPS) * kg_ref[...]
    kn_ref[...] = k.astype(BF16)
    o0 += d_c
    vv_ref[...] = jnp.dot(h, w_ref[:, o0:o0 + d_c], preferred_element_type=F32).astype(BF16)


def _inproj(xall, mod, norm_g, w_pad, qn_g, kn_g, dims):
    n_all, d = xall.shape
    n_lat, t_len, n_batch, d_a, d_b, d_c = dims
    tm = ROW_TILE
    row = lambda i: (i, 0)
    const = lambda i: (0, 0)
    grp = lambda i: (_mod_group(i, tm, n_lat, t_len, n_batch), 0, 0)
    kern = functools.partial(_inproj_kernel, d_a5=5 * d_a, d_b4=4 * d_b, d_c=d_c)
    return pl.pallas_call(
        kern,
        name="inproj",
        out_shape=(jax.ShapeDtypeStruct((n_all, 5 * d_a), F32),
                   jax.ShapeDtypeStruct((n_all, 4 * d_b), F32),
                   jax.ShapeDtypeStruct((n_all, LANES), F32),
                   jax.ShapeDtypeStruct((n_all, d_c), BF16),
                   jax.ShapeDtypeStruct((n_all, d_c), BF16),
                   jax.ShapeDtypeStruct((n_all, d_c), BF16)),
        grid=(n_all // tm,),
        in_specs=[pl.BlockSpec((tm, d), row),
                  pl.BlockSpec((1, 6, d), grp),
                  pl.BlockSpec((1, d), const),
                  pl.BlockSpec(w_pad.shape, const),
                  pl.BlockSpec((1, d_c), const),
                  pl.BlockSpec((1, d_c), const)],
        out_specs=(pl.BlockSpec((tm, 5 * d_a), row), pl.BlockSpec((tm, 4 * d_b), row),
                   pl.BlockSpec((tm, LANES), row), pl.BlockSpec((tm, d_c), row),
                   pl.BlockSpec((tm, d_c), row), pl.BlockSpec((tm, d_c), row)),
        compiler_params=_cparams(("parallel",)),
    )(xall, mod, norm_g, w_pad, qn_g, kn_g)


def _scan_block_index(b, d, j, n_lat_blk, n_ctx_blk, n_batch):
    jc = jnp.where(d == 0, j, n_ctx_blk - 1 - j)
    jl = jnp.where(d == 0, j - n_ctx_blk, n_lat_blk - 1 - (j - n_ctx_blk))
    return jnp.where(j < n_ctx_blk, n_batch * n_lat_blk + b * n_ctx_blk + jc, b * n_lat_blk + jl)


def _block_diag(x, mask_bd, n_rep):
    xb = x.astype(BF16)
    return jnp.where(mask_bd, jnp.concatenate([xb] * n_rep, axis=0), jnp.zeros((), BF16))


def _scan_order(idx, d):
    return idx + d * (CHUNK - 1 - 2 * idx)


def _split3(x):
    hi = x.astype(BF16)
    r = x - hi.astype(F32)
    mid = r.astype(BF16)
    lo = (r - mid.astype(F32)).astype(BF16)
    return hi, mid, lo


def _sel_dot(sel, x):
    hi, mid, lo = _split3(x)
    dot = lambda a: jnp.dot(sel, a, preferred_element_type=F32)
    return (dot(lo) + dot(mid)) + dot(hi)


HGRN_LEVELS = tuple(CHUNK >> (i + 1) for i in range(int(math.log2(CHUNK))))


def _hgrn_kernel(q_ref, z_ref, v_ref, lb_ref, o_ref, st_ref, *, n_chunk):
    d = pl.program_id(1)
    j = pl.program_id(2)
    n = q_ref.shape[-1]
    n_rep = n // HEAD_DIM

    @pl.when(j == 0)
    def _():
        st_ref[...] = jnp.zeros_like(st_ref)

    lb = lb_ref[0]
    mask_bd = _head_mask(n)
    gones = mask_bd.astype(BF16)
    bd = lambda a: _block_diag(a, mask_bd, n_rep)
    s_t = _scan_order(_iota((CHUNK, CHUNK), 0), d)
    s_u = _scan_order(_iota((CHUNK, CHUNK), 1), d)
    sets = [s_u <= s_t, s_u > s_t]
    for m in HGRN_LEVELS:
        ref = (s_t // (2 * m)) * (2 * m) + m - 1
        sets.append((s_u > jnp.minimum(s_t, ref)) & (s_u <= jnp.maximum(s_t, ref)))
    sel = jnp.concatenate([jnp.where(a, 1.0, 0.0).astype(BF16) for a in sets], axis=0)
    s_row = _scan_order(_iota((CHUNK, n), 0), d)
    s_col = _scan_order(_iota((CHUNK, n), 1) % HEAD_DIM, d)
    eye = s_row == s_col
    second_half = [(s_row // m) % 2 == 1 for m in HGRN_LEVELS]
    same_block = [(s_row // (2 * m)) == (s_col // (2 * m)) for m in HGRN_LEVELS]

    work = []
    for c in range(n_chunk):
        ci = c + d * (n_chunk - 1 - 2 * c)
        sl = pl.ds(pl.multiple_of(ci * CHUNK, CHUNK), CHUNK)
        q = q_ref[sl, :] * HEAD_DIM ** -0.5
        z = z_ref[sl, :]
        v = v_ref[sl, :]
        log_f = (jnp.minimum(z, 0.0) - jnp.log1p(jnp.exp(-jnp.abs(z)))) + jnp.log1p(lb * jnp.exp(-z))
        k = (1.0 - lb) * jax.nn.sigmoid(-z)
        work.append((sl, q, k, v, _sel_dot(sel, log_f)))
    atts = [jnp.where(eye, jnp.dot((q * k).astype(BF16), gones, preferred_element_type=F32), 0.0)
            for _, q, k, _, _ in work]
    for lvl, m in enumerate(HGRN_LEVELS):
        second = second_half[lvl]
        for i, (_, q, k, _, sums) in enumerate(work):
            e = jnp.exp(sums[(2 + lvl) * CHUNK:(3 + lvl) * CHUNK])
            a_m = _dot_nt(jnp.where(second, q * e, 0.0), bd(jnp.where(second, 0.0, k * e)))
            if 2 * m < CHUNK:
                a_m = jnp.where(same_block[lvl], a_m, 0.0)
            atts[i] = atts[i] + a_m
    pre = []
    for (sl, q, k, v, sums), att in zip(work, atts):
        b_incl = sums[0:CHUNK]
        b_after = sums[CHUNK:2 * CHUNK]
        e_end = jnp.exp(b_incl[0:1, :] + b_after[0:1, :])
        upd = jnp.where(mask_bd, _dot_tn(v, k * jnp.exp(b_after)), 0.0)
        pre.append((sl, _dot(att, bd(v)), (q * jnp.exp(b_incl)).astype(BF16), e_end, upd))

    st = st_ref[...]
    for sl, o_intra, q_hat, e_end, upd in pre:
        o_ref[0, sl, :] = o_intra + _dot_nt(q_hat, st)
        st = st * e_end + upd
    st_ref[...] = st


def _hgrn_scan(pa, lb2, dims):
    n_all = pa.shape[0]
    n_lat, t_len, n_batch, d_a, d_b, d_c = dims
    n_lat_blk = t_len // TOK_BLK
    n_ctx_blk = (n_all - n_lat) // n_batch // TOK_BLK
    blk = functools.partial(_scan_block_index, n_lat_blk=n_lat_blk, n_ctx_blk=n_ctx_blk, n_batch=n_batch)
    kern = functools.partial(_hgrn_kernel, n_chunk=TOK_BLK // CHUNK)
    return pl.pallas_call(
        kern,
        name="hgrn_scan",
        out_shape=jax.ShapeDtypeStruct((2, n_all, d_a), F32),
        grid=(n_batch, 2, n_lat_blk + n_ctx_blk),
        in_specs=[pl.BlockSpec((TOK_BLK, d_a), lambda b, d, j: (blk(b, d, j), 0)),
                  pl.BlockSpec((TOK_BLK, d_a), lambda b, d, j: (blk(b, d, j), 1 + d)),
                  pl.BlockSpec((TOK_BLK, d_a), lambda b, d, j: (blk(b, d, j), 3)),
                  pl.BlockSpec((1, 1, d_a), lambda b, d, j: (d, 0, 0))],
        out_specs=pl.BlockSpec((1, TOK_BLK, d_a), lambda b, d, j: (d, blk(b, d, j), 0)),
        scratch_shapes=[pltpu.VMEM((d_a, d_a), F32)],
        compiler_params=_cparams(("parallel", "parallel", "arbitrary")),
    )(pa, pa, pa, lb2)


def _gdn_prep_kernel(x_ref, prev_ref, next_ref, pg_ref, cw_ref, alog_ref, dtb_ref, cos_ref, sin_ref,
                     w_ref, u0_ref, qh_ref, qg_ref, kh_ref, ee_ref,
                     *, n_lat_blk, n_ctx_blk, n_lat_blks_total, d_b):
    i = pl.program_id(0)
    is_lat = i < n_lat_blks_total
    pos = jnp.where(is_lat, i % n_lat_blk, (i - n_lat_blks_total) % n_ctx_blk)
    n_seq_blk = jnp.where(is_lat, n_lat_blk, n_ctx_blk)
    first = pos == 0
    last = pos == n_seq_blk - 1
    x = x_ref[...]
    halo = prev_ref.shape[0]
    prev = jnp.where(first, 0.0, prev_ref[...])
    nxt = jnp.where(last, 0.0, next_ref[...])
    rows = _iota(x.shape, 0)
    tb = x.shape[0]
    half = CONV_K // 2
    acc = x * cw_ref[half:half + 1, :]
    for s in range(1, half + 1):
        xs = pltpu.roll(x, s, 0)
        for r in range(s):
            xs = jnp.where(rows == r, prev[halo - s + r:halo - s + r + 1, :], xs)
        acc = acc + xs * cw_ref[half - s:half - s + 1, :]
        xs = pltpu.roll(x, tb - s, 0)
        for r in range(s):
            xs = jnp.where(rows == tb - s + r, nxt[r:r + 1, :], xs)
        acc = acc + xs * cw_ref[half + s:half + s + 1, :]
    y = _silu(acc)
    gsum = _head_mask(d_b).astype(F32)

    def l2n(a):
        return a * lax.rsqrt(_dot_hi(a * a, gsum) + EPS)

    lane = _iota((tb, d_b), 1) % HEAD_DIM
    lo = lane < HEAD_DIM // 2

    def rope(a):
        partner = jnp.where(lo, pltpu.roll(a, d_b - HEAD_DIM // 2, 1), pltpu.roll(a, HEAD_DIM // 2, 1))
        return jnp.where(is_lat, a * cos_ref[...] + partner * sin_ref[...], a)

    q_all = rope(l2n(y[:, 0:d_b])) * HEAD_DIM ** -0.5
    k_all = rope(l2n(y[:, d_b:2 * d_b]))
    v_all = y[:, 2 * d_b:3 * d_b]
    n_h = d_b // HEAD_DIM
    g = pg_ref[...]
    e_r = _iota((LANES, d_b), 0)
    e_c = _iota((LANES, d_b), 1) // HEAD_DIM
    mask_bd = _head_mask(d_b)
    bd = lambda a: _block_diag(a, mask_bd, n_h)
    ones_cc = jnp.ones((CHUNK, CHUNK), BF16)
    ee_ref[...] = jnp.zeros_like(ee_ref)
    for dd in range(2):
        a_x = _dot_hi(g, (e_r == dd * n_h + e_c).astype(F32))
        b_x = _dot_hi(g, (e_r == (2 + dd) * n_h + e_c).astype(F32))
        t = a_x + dtb_ref[dd]
        softplus = jnp.maximum(t, 0.0) + jnp.log1p(jnp.exp(-jnp.abs(t)))
        la_all = -jnp.exp(alog_ref[dd]) * softplus
        be_all = jax.nn.sigmoid(b_x)
        s_t = _scan_order(_iota((CHUNK, CHUNK), 0), dd)
        s_u = _scan_order(_iota((CHUNK, CHUNK), 1), dd)
        m_incl = jnp.where(s_u <= s_t, 1.0, 0.0).astype(BF16)
        s_row = _scan_order(_iota((CHUNK, d_b), 0), dd)
        s_col = _scan_order(_iota((CHUNK, d_b), 1) % HEAD_DIM, dd)
        incl = s_col <= s_row
        strict = s_col < s_row
        m_before = jnp.where(s_row <= s_col, 1.0, 0.0)
        eye_f = jnp.where(s_col == s_row, 1.0, 0.0)
        pair = (s_row // 2) == (s_col // 2)
        levels = []
        m = 4
        while m <= CHUNK:
            levels.append(((s_row // m) == (s_col // m)) & ((s_row // (m // 2)) != (s_col // (m // 2))))
            m *= 2
        end_row = CHUNK - 1 if dd == 0 else 0
        work = []
        for c in range(tb // CHUNK):
            r = slice(c * CHUNK, (c + 1) * CHUNK)
            q, k, v, la, be = q_all[r], k_all[r], v_all[r], la_all[r], be_all[r]
            g_t = _sel_dot(m_incl, la)
            g_s = _sel_dot(ones_cc, la * m_before)
            gam = jnp.where(incl, jnp.exp(jnp.minimum(g_t - g_s, 0.0)), 0.0)
            k_bd = bd(k)
            kk = _dot_nt(k, k_bd)
            qk = _dot_nt(q, k_bd)
            a = jnp.where(strict, be * kk * gam, 0.0)
            eg = jnp.exp(g_t)
            g_end = g_t[end_row:end_row + 1, :]
            qh_ref[dd, r, :] = (q * eg).astype(BF16)
            qg_ref[dd, r, :] = (qk * gam).astype(BF16)
            kh_ref[dd, r, :] = (k * jnp.exp(g_end - g_t)).astype(BF16)
            ee_ref[dd, 0, c:c + 1, :] = jnp.exp(g_end)
            work.append((r, a, bd(be * eg * k), bd(be * v)))
        t_invs = [eye_f - jnp.where(pair, a, 0.0) for _, a, _, _ in work]
        for lvl in levels:
            xs = [_dot(jnp.where(lvl, wk[1], 0.0), bd(t)) for wk, t in zip(work, t_invs)]
            t_invs = [t - _dot(t, bd(x)) for t, x in zip(t_invs, xs)]
        for (r, _, wk_bd, vb_bd), t in zip(work, t_invs):
            w_ref[dd, r, :] = _dot(t, wk_bd).astype(BF16)
            u0_ref[dd, r, :] = _dot(t, vb_bd)


def _gdn_prep(pb, pg, conv_w, alog_x, dtb_x, cos_t, sin_t, dims):
    n_all = pb.shape[0]
    n_lat, t_len, n_batch, d_a, d_b, d_c = dims
    n_lat_blk = t_len // TOK_BLK
    n_ctx_blk = (n_all - n_lat) // n_batch // TOK_BLK
    n_blk = n_all // TOK_BLK
    halo = 8
    per = TOK_BLK // halo
    kern = functools.partial(_gdn_prep_kernel, n_lat_blk=n_lat_blk, n_ctx_blk=n_ctx_blk,
                             n_lat_blks_total=n_lat // TOK_BLK, d_b=d_b)
    row = lambda i: (i, 0)
    const2 = lambda i: (0, 0)
    const3 = lambda i: (0, 0, 0)
    tab = lambda i: (jnp.where(i < n_lat // TOK_BLK, i % n_lat_blk, 0), 0)
    both = pl.BlockSpec((2, TOK_BLK, d_b), lambda i: (0, i, 0))
    sds = jax.ShapeDtypeStruct
    return pl.pallas_call(
        kern,
        name="gdn_prep",
        out_shape=(sds((2, n_all, d_b), BF16), sds((2, n_all, d_b), F32), sds((2, n_all, d_b), BF16),
                   sds((2, n_all, d_b), BF16), sds((2, n_all, d_b), BF16), sds((2, n_blk, 8, d_b), F32)),
        grid=(n_blk,),
        in_specs=[pl.BlockSpec((TOK_BLK, 3 * d_b), row),
                  pl.BlockSpec((halo, 3 * d_b), lambda i: (jnp.maximum(i * per - 1, 0), 0)),
                  pl.BlockSpec((halo, 3 * d_b), lambda i: (jnp.minimum((i + 1) * per, n_blk * per - 1), 0)),
                  pl.BlockSpec((TOK_BLK, LANES), row),
                  pl.BlockSpec((8, 3 * d_b), const2),
                  pl.BlockSpec((2, 1, d_b), const3),
                  pl.BlockSpec((2, 1, d_b), const3),
                  pl.BlockSpec((TOK_BLK, d_b), tab),
                  pl.BlockSpec((TOK_BLK, d_b), tab)],
        out_specs=(both, both, both, both, both, pl.BlockSpec((2, 1, 8, d_b), lambda i: (0, i, 0, 0))),
        compiler_params=_cparams(("parallel",)),
    )(pb, pb, pb, pg, conv_w, alog_x, dtb_x, cos_t, sin_t)


def _gdn_kernel(*refs, n_chunk):
    ins = (refs[0:6], refs[6:12])
    outs = refs[12:14]
    st_ref = refs[14]
    j = pl.program_id(1)
    n = outs[0].shape[-1]
    n_rep = n // HEAD_DIM

    @pl.when(j == 0)
    def _():
        st_ref[...] = jnp.zeros_like(st_ref)

    mask_bd = _head_mask(n)
    bd = lambda a: _block_diag(a, mask_bd, n_rep)
    sts = [st_ref[0], st_ref[1]]
    for c in range(n_chunk):
        for dd in range(2):
            w_ref, u0_ref, qh_ref, qg_ref, kh_ref, ee_ref = ins[dd]
            cc = c if dd == 0 else n_chunk - 1 - c
            r = slice(cc * CHUNK, (cc + 1) * CHUNK)
            st = sts[dd]
            u = u0_ref[0, r, :] - _dot_nt(w_ref[0, r, :], st)
            outs[dd][r, :] = _dot_nt(qh_ref[0, r, :], st) + _dot(qg_ref[0, r, :], bd(u))
            sts[dd] = st * ee_ref[0, 0, cc:cc + 1, :] + jnp.where(mask_bd, _dot_tn(u, kh_ref[0, r, :]), 0.0)
    st_ref[0] = sts[0]
    st_ref[1] = sts[1]


def _gdn_scan(wy, dims):
    n_all = wy[0].shape[1]
    n_lat, t_len, n_batch, d_a, d_b, d_c = dims
    n_lat_blk = t_len // TOK_BLK
    n_ctx_blk = (n_all - n_lat) // n_batch // TOK_BLK
    blk = functools.partial(_scan_block_index, n_lat_blk=n_lat_blk, n_ctx_blk=n_ctx_blk, n_batch=n_batch)
    kern = functools.partial(_gdn_kernel, n_chunk=TOK_BLK // CHUNK)
    in_specs = []
    for dd in range(2):
        tok = pl.BlockSpec((1, TOK_BLK, d_b), lambda b, j, dd=dd: (dd, blk(b, dd, j), 0))
        in_specs += [tok] * 5 + [pl.BlockSpec((1, 1, 8, d_b), lambda b, j, dd=dd: (dd, blk(b, dd, j), 0, 0))]
    out_specs = tuple(pl.BlockSpec((TOK_BLK, d_b), lambda b, j, dd=dd: (blk(b, dd, j), 0)) for dd in range(2))
    return pl.pallas_call(
        kern,
        name="gdn_scan",
        out_shape=(jax.ShapeDtypeStruct((n_all, d_b), F32), jax.ShapeDtypeStruct((n_all, d_b), F32)),
        grid=(n_batch, n_lat_blk + n_ctx_blk),
        in_specs=in_specs,
        out_specs=out_specs,
        scratch_shapes=[pltpu.VMEM((2, d_b, d_b), F32)],
        compiler_params=_cparams(("parallel", "arbitrary")),
    )(*wy, *wy)


def _stack_heads(q, n_rep):
    m, n = q.shape
    keep = (_iota((n_rep * m, n), 0) // m) == (_iota((n_rep * m, n), 1) // HEAD_DIM)
    return jnp.where(keep, jnp.concatenate([q] * n_rep, axis=0), jnp.zeros((), q.dtype))


def _fold_heads(o, m, n_rep):
    n = o.shape[1]
    lane_h = _iota((m, n), 1) // HEAD_DIM
    acc = jnp.zeros((m, n), F32)
    for h in range(n_rep):
        acc = acc + jnp.where(lane_h == h, o[h * m:(h + 1) * m, :], 0.0)
    return acc


def _natten_kernel(q_ref, k_ref, v_ref, kc_ref, vc_ref, bias_ref, o_ref, *, n_rows):
    r = pl.program_id(1)
    n = q_ref.shape[-1]
    n_rep = n // HEAD_DIM
    kr = min(WIN_R, n_rows)
    r0 = jnp.clip(r - WIN_R // 2, 0, n_rows - kr)
    sl = pl.ds(pl.multiple_of(r0 * GRID_W, GRID_W), kr * GRID_W)
    qs = _stack_heads(q_ref[...], n_rep)
    dn = (((1,), (1,)), ((), ()))
    s_loc = lax.dot_general(qs, k_ref[sl, :], dn, preferred_element_type=F32) + bias_ref[0]
    s_ctx = lax.dot_general(qs, kc_ref[...], dn, preferred_element_type=F32)
    m = jnp.maximum(jnp.max(s_loc, axis=-1, keepdims=True), jnp.max(s_ctx, axis=-1, keepdims=True))
    p_loc = jnp.exp(s_loc - m)
    p_ctx = jnp.exp(s_ctx - m)
    inv = 1.0 / (jnp.sum(p_loc, axis=-1, keepdims=True) + jnp.sum(p_ctx, axis=-1, keepdims=True))
    o = jnp.dot((p_loc * inv).astype(BF16), v_ref[sl, :], preferred_element_type=F32)
    o = o + jnp.dot((p_ctx * inv).astype(BF16), vc_ref[...], preferred_element_type=F32)
    o_ref[...] = _fold_heads(o, q_ref.shape[0], n_rep)


def _natten(qn, kn, vv, bias_tab, dims):
    n_lat, t_len, n_batch, d_a, d_b, d_c = dims
    n_all = qn.shape[0]
    l_ctx = (n_all - n_lat) // n_batch
    n_rows = t_len // GRID_W
    kr = min(WIN_R, n_rows)
    ctx0 = n_lat // l_ctx

    def cfg(b, r):
        r0 = jnp.clip(r - WIN_R // 2, 0, n_rows - kr)
        return (r - r0, 0, 0)

    kern = functools.partial(_natten_kernel, n_rows=n_rows)
    return pl.pallas_call(
        kern,
        name="natten",
        out_shape=jax.ShapeDtypeStruct((n_lat, d_c), F32),
        grid=(n_batch, n_rows),
        in_specs=[pl.BlockSpec((GRID_W, d_c), lambda b, r: (b * n_rows + r, 0)),
                  pl.BlockSpec((t_len, d_c), lambda b, r: (b, 0)),
                  pl.BlockSpec((t_len, d_c), lambda b, r: (b, 0)),
                  pl.BlockSpec((l_ctx, d_c), lambda b, r: (ctx0 + b, 0)),
                  pl.BlockSpec((l_ctx, d_c), lambda b, r: (ctx0 + b, 0)),
                  pl.BlockSpec((1,) + bias_tab.shape[1:], cfg)],
        out_specs=pl.BlockSpec((GRID_W, d_c), lambda b, r: (b * n_rows + r, 0)),
        compiler_params=_cparams(("parallel", "arbitrary")),
    )(qn, kn, vv, kn, vv, bias_tab)


def _ctx_attn_kernel(q_ref, k_ref, v_ref, o_ref):
    n = q_ref.shape[-1]
    n_rep = n // HEAD_DIM
    qs = _stack_heads(q_ref[...], n_rep)
    s = lax.dot_general(qs, k_ref[...], (((1,), (1,)), ((), ())), preferred_element_type=F32)
    p = jnp.exp(s - jnp.max(s, axis=-1, keepdims=True))
    p = p * (1.0 / jnp.sum(p, axis=-1, keepdims=True))
    o = jnp.dot(p.astype(BF16), v_ref[...], preferred_element_type=F32)
    o_ref[...] = _fold_heads(o, q_ref.shape[0], n_rep)


def _ctx_attn(qn, kn, vv, dims):
    n_lat, t_len, n_batch, d_a, d_b, d_c = dims
    n_all = qn.shape[0]
    l_ctx = (n_all - n_lat) // n_batch
    tq = 64
    per = l_ctx // tq
    q0 = n_lat // tq
    c0 = n_lat // l_ctx
    return pl.pallas_call(
        _ctx_attn_kernel,
        name="ctx_attn",
        out_shape=jax.ShapeDtypeStruct((n_all - n_lat, d_c), F32),
        grid=(n_batch, per),
        in_specs=[pl.BlockSpec((tq, d_c), lambda b, i: (q0 + b * per + i, 0)),
                  pl.BlockSpec((l_ctx, d_c), lambda b, i: (c0 + b, 0)),
                  pl.BlockSpec((l_ctx, d_c), lambda b, i: (c0 + b, 0))],
        out_specs=pl.BlockSpec((tq, d_c), lambda b, i: (b * per + i, 0)),
        compiler_params=_cparams(("parallel", "arbitrary")),
    )(qn, kn, vv)


def _natten_bias(rpb, n_rows):
    n_h = rpb.shape[0]
    kr = min(WIN_R, n_rows)
    cols = jnp.arange(GRID_W)
    c0 = jnp.clip(cols - WIN_C // 2, 0, GRID_W - WIN_C)
    kc = jnp.arange(GRID_W)
    in_win = (kc[None, :] >= c0[:, None]) & (kc[None, :] < c0[:, None] + WIN_C)
    dcol = jnp.clip(kc[None, :] - cols[:, None] + WIN_C - 1, 0, 2 * WIN_C - 2)
    toe = jnp.where(in_win[None, None], rpb.astype(F32)[:, :, dcol], NEG_BIG)
    tabs = jnp.stack([toe[:, WIN_R - 1 - delta:WIN_R - 1 - delta + kr] for delta in range(kr)])
    return tabs.transpose(0, 1, 3, 2, 4).reshape(kr, n_h * GRID_W, kr * GRID_W)


def _outproj_kernel(x_ref, m_ref, oa_ref, ga_ref, obf_ref, obb_ref, gb_ref, oc_ref, na_ref, nb_ref, w_ref, o_ref,
                    *, d_a, d_b):
    gmean_a = jnp.where(_head_mask(d_a), 1.0 / HEAD_DIM, 0.0).astype(F32)
    oa = oa_ref[0] + oa_ref[1]
    ya = oa * lax.rsqrt(_group_mean_sq(oa, gmean_a) + EPS) * na_ref[...] * _silu(ga_ref[...])
    gmean_b = jnp.where(_head_mask(d_b), 1.0 / HEAD_DIM, 0.0).astype(F32)
    ob = obf_ref[...] + obb_ref[...]
    yb = ob * lax.rsqrt(_group_mean_sq(ob, gmean_b) + EPS) * nb_ref[...] * _silu(gb_ref[...])
    acc = _dot(ya, w_ref[0:d_a, :])
    acc = acc + _dot(yb, w_ref[d_a:d_a + d_b, :])
    acc = acc + _dot(oc_ref[...], w_ref[d_a + d_b:, :])
    o_ref[...] = x_ref[...] + m_ref[0, 2:3, :] * acc


def _outproj(xall, mod, oa, pa, obf, obb, pb, oc, na_g, nb_g, w_out, n_rows_out, dims):
    n_lat, t_len, n_batch, d_a, d_b, d_c = dims
    d = xall.shape[1]
    tm = ROW_TILE
    row = lambda i: (i, 0)
    const = lambda i: (0, 0)
    grp = lambda i: (_mod_group(i, tm, n_lat, t_len, n_batch), 0, 0)
    kern = functools.partial(_outproj_kernel, d_a=d_a, d_b=d_b)
    return pl.pallas_call(
        kern,
        name="outproj",
        out_shape=jax.ShapeDtypeStruct((n_rows_out, d), F32),
        grid=(n_rows_out // tm,),
        in_specs=[pl.BlockSpec((tm, d), row),
                  pl.BlockSpec((1, 6, d), grp),
                  pl.BlockSpec((2, tm, d_a), lambda i: (0, i, 0)),
                  pl.BlockSpec((tm, d_a), lambda i: (i, 4)),
                  pl.BlockSpec((tm, d_b), row),
                  pl.BlockSpec((tm, d_b), row),
                  pl.BlockSpec((tm, d_b), lambda i: (i, 3)),
                  pl.BlockSpec((tm, d_c), row),
                  pl.BlockSpec((1, d_a), const),
                  pl.BlockSpec((1, d_b), const),
                  pl.BlockSpec(w_out.shape, const)],
        out_specs=pl.BlockSpec((tm, d), row),
        compiler_params=_cparams(("parallel",)),
    )(xall, mod, oa, pa, obf, obb, pb, oc, na_g, nb_g, w_out)


def _ffn_kernel(x_ref, m_ref, g_ref, w1_ref, w3_ref, w2_ref, o_ref, h_ref, acc_ref):
    j = pl.program_id(1)

    @pl.when(j == 0)
    def _():
        h_ref[...] = _modulated_norm(x_ref[...], g_ref[...], m_ref[0, 3:4, :], m_ref[0, 4:5, :]).astype(BF16)
        acc_ref[...] = jnp.zeros_like(acc_ref)

    h = h_ref[...]
    a = jnp.dot(h, w1_ref[...], preferred_element_type=F32)
    b = jnp.dot(h, w3_ref[...], preferred_element_type=F32)
    acc_ref[...] += _dot(_silu(a) * b, w2_ref[...])

    @pl.when(j == pl.num_programs(1) - 1)
    def _():
        o_ref[...] = x_ref[...] + m_ref[0, 5:6, :] * acc_ref[...]


def _ffn(xall, mod, norm_g, w1, w3, w2, dims):
    n_lat, t_len, n_batch, d_a, d_b, d_c = dims
    n_rows, d = xall.shape
    d_ff = w1.shape[1]
    tm, tf = ROW_TILE, FF_TILE
    grp = lambda i, j: (_mod_group(i, tm, n_lat, t_len, n_batch), 0, 0)
    return pl.pallas_call(
        _ffn_kernel,
        name="ffn",
        out_shape=jax.ShapeDtypeStruct((n_rows, d), F32),
        grid=(n_rows // tm, d_ff // tf),
        in_specs=[pl.BlockSpec((tm, d), lambda i, j: (i, 0)),
                  pl.BlockSpec((1, 6, d), grp),
                  pl.BlockSpec((1, d), lambda i, j: (0, 0)),
                  pl.BlockSpec((d, tf), lambda i, j: (0, j)),
                  pl.BlockSpec((d, tf), lambda i, j: (0, j)),
                  pl.BlockSpec((tf, d), lambda i, j: (j, 0))],
        out_specs=pl.BlockSpec((tm, d), lambda i, j: (i, 0)),
        scratch_shapes=[pltpu.VMEM((tm, d), BF16), pltpu.VMEM((tm, d), F32)],
        compiler_params=_cparams(("parallel", "arbitrary")),
    )(xall, mod, norm_g, w1, w3, w2)


def _route_kernel(x_ref, m_ref, g_ref, wr_ref, br_ref, h_ref, e_ref, gt_ref):
    h = _modulated_norm(x_ref[...], g_ref[...], m_ref[0, 3:4, :], m_ref[0, 4:5, :])
    h_ref[...] = h
    lane = _iota((h.shape[0], LANES), 1)
    logits = jnp.where(lane < N_EXPERTS, _dot_hi(h, wr_ref[...]) + br_ref[...], -jnp.inf)
    m1 = jnp.max(logits, axis=-1, keepdims=True)
    lane_f = lane.astype(F32)
    i1 = jnp.min(jnp.where(logits == m1, lane_f, float(LANES)), axis=-1, keepdims=True).astype(jnp.int32)
    rest = jnp.where(lane == i1, -jnp.inf, logits)
    m2 = jnp.max(rest, axis=-1, keepdims=True)
    i2 = jnp.min(jnp.where(rest == m2, lane_f, float(LANES)), axis=-1, keepdims=True).astype(jnp.int32)
    e2 = jnp.exp(m2 - m1)
    g1 = 1.0 / (1.0 + e2)
    g2 = e2 / (1.0 + e2)
    e_ref[...] = jnp.where(lane == 0, i1, jnp.where(lane == 1, i2, 0))
    gt_ref[...] = jnp.where(lane == 0, g1, jnp.where(lane == 1, g2, 0.0))


def _route(x, mod, norm_g, wr_pad, br_pad, dims):
    n_lat, t_len, n_batch, d_a, d_b, d_c = dims
    n, d = x.shape
    tm = ROW_TILE
    row = lambda i: (i, 0)
    const = lambda i: (0, 0)
    grp = lambda i: (_mod_group(i, tm, n_lat, t_len, n_batch), 0, 0)
    return pl.pallas_call(
        _route_kernel,
        name="moe_route",
        out_shape=(jax.ShapeDtypeStruct((n, d), F32), jax.ShapeDtypeStruct((n, LANES), jnp.int32),
                   jax.ShapeDtypeStruct((n, LANES), F32)),
        grid=(n // tm,),
        in_specs=[pl.BlockSpec((tm, d), row), pl.BlockSpec((1, 6, d), grp), pl.BlockSpec((1, d), const),
                  pl.BlockSpec((d, LANES), const), pl.BlockSpec((1, LANES), const)],
        out_specs=(pl.BlockSpec((tm, d), row), pl.BlockSpec((tm, LANES), row), pl.BlockSpec((tm, LANES), row)),
        compiler_params=_cparams(("parallel",)),
    )(x, mod, norm_g, wr_pad, br_pad)


def _row_copy(src_hbm, row, dst_vmem, r, sem):
    return pltpu.make_async_copy(src_hbm.at[pl.ds(row, 1)], dst_vmem.at[pl.ds(r, 1)], sem)


def _gather_kernel(idx_ref, h_hbm, o_ref, sem):
    n_rows = o_ref.shape[0]

    def start(r, carry):
        _row_copy(h_hbm, idx_ref[0, 0, r], o_ref, r, sem).start()
        return carry

    lax.fori_loop(0, n_rows, start, 0)

    def wait(r, carry):
        _row_copy(h_hbm, 0, o_ref, r, sem).wait()
        return carry

    lax.fori_loop(0, n_rows, wait, 0)


def _gather_rows(h, slot_tok, n_blocks):
    d = h.shape[1]
    bm = MOE_ROWS
    return pl.pallas_call(
        _gather_kernel,
        name="moe_gather",
        out_shape=jax.ShapeDtypeStruct((n_blocks * bm, d), h.dtype),
        grid=(n_blocks,),
        in_specs=[pl.BlockSpec((1, 1, bm), lambda i: (i, 0, 0), memory_space=pltpu.SMEM),
                  pl.BlockSpec(memory_space=pl.ANY)],
        out_specs=pl.BlockSpec((bm, d), lambda i: (i, 0)),
        scratch_shapes=[pltpu.SemaphoreType.DMA(())],
        compiler_params=_cparams(("arbitrary",)),
    )(slot_tok.reshape(n_blocks, 1, bm), h)


def _expert_kernel(be_ref, nv_ref, x_ref, w1_ref, w3_ref, w2_ref, o_ref, h_ref, acc_ref):
    i = pl.program_id(0)
    j = pl.program_id(1)
    live = i < nv_ref[0]

    @pl.when(live & (j == 0))
    def _():
        h_ref[...] = x_ref[...].astype(BF16)
        acc_ref[...] = jnp.zeros_like(acc_ref)

    @pl.when(live)
    def _():
        h = h_ref[...]
        a = _dot(h, w1_ref[0])
        b = _dot(h, w3_ref[0])
        acc_ref[...] += _dot(_silu(a) * b, w2_ref[0])

    last = j == pl.num_programs(1) - 1

    @pl.when(live & last)
    def _():
        o_ref[...] = acc_ref[...]

    @pl.when(jnp.logical_not(live) & last)
    def _():
        o_ref[...] = jnp.zeros_like(o_ref)


def _experts(xg, block_e, n_live, w1, w3, w2):
    n_slots, d = xg.shape
    d_ff = w1.shape[2]
    bm, tf = MOE_ROWS, FF_TILE
    grid_spec = pltpu.PrefetchScalarGridSpec(
        num_scalar_prefetch=2,
        grid=(n_slots // bm, d_ff // tf),
        in_specs=[pl.BlockSpec((bm, d), lambda i, j, be, nv: (i, 0)),
                  pl.BlockSpec((1, d, tf), lambda i, j, be, nv: (be[i], 0, j)),
                  pl.BlockSpec((1, d, tf), lambda i, j, be, nv: (be[i], 0, j)),
                  pl.BlockSpec((1, tf, d), lambda i, j, be, nv: (be[i], j, 0))],
        out_specs=pl.BlockSpec((bm, d), lambda i, j, be, nv: (i, 0)),
        scratch_shapes=[pltpu.VMEM((bm, d), BF16), pltpu.VMEM((bm, d), F32)])
    return pl.pallas_call(
        _expert_kernel,
        name="moe_experts",
        out_shape=jax.ShapeDtypeStruct((n_slots, d), F32),
        grid_spec=grid_spec,
        compiler_params=_cparams(("arbitrary", "arbitrary")),
    )(block_e, n_live, xg, w1, w3, w2)


def _combine_kernel(idx_ref, x_ref, m_ref, gt_ref, y_hbm, o_ref, buf_ref, sem):
    tm = x_ref.shape[0]

    def start(r, carry):
        _row_copy(y_hbm, idx_ref[0, 0, 2 * r], buf_ref.at[0], r, sem).start()
        _row_copy(y_hbm, idx_ref[0, 0, 2 * r + 1], buf_ref.at[1], r, sem).start()
        return carry

    lax.fori_loop(0, tm, start, 0)

    def wait(r, carry):
        _row_copy(y_hbm, 0, buf_ref.at[0], r, sem).wait()
        _row_copy(y_hbm, 0, buf_ref.at[1], r, sem).wait()
        return carry

    lax.fori_loop(0, tm, wait, 0)
    gt = gt_ref[...]
    y = gt[:, 0:1] * buf_ref[0] + gt[:, 1:2] * buf_ref[1]
    o_ref[...] = x_ref[...] + m_ref[0, 5:6, :] * y


def _combine(x, mod, gates, slots, y, dims):
    n_lat, t_len, n_batch, d_a, d_b, d_c = dims
    n, d = x.shape
    tm = CMB_ROWS
    grp = lambda i: (_mod_group(i, tm, n_lat, t_len, n_batch), 0, 0)
    return pl.pallas_call(
        _combine_kernel,
        name="moe_combine",
        out_shape=jax.ShapeDtypeStruct((n, d), F32),
        grid=(n // tm,),
        in_specs=[pl.BlockSpec((1, 1, 2 * tm), lambda i: (i, 0, 0), memory_space=pltpu.SMEM),
                  pl.BlockSpec((tm, d), lambda i: (i, 0)),
                  pl.BlockSpec((1, 6, d), grp),
                  pl.BlockSpec((tm, LANES), lambda i: (i, 0)),
                  pl.BlockSpec(memory_space=pl.ANY)],
        out_specs=pl.BlockSpec((tm, d), lambda i: (i, 0)),
        scratch_shapes=[pltpu.VMEM((2, tm, d), F32), pltpu.SemaphoreType.DMA(())],
        compiler_params=_cparams(("arbitrary",)),
    )(slots.reshape(n // tm, 1, 2 * tm), x, mod, gates, y)


def _moe(x, mod, norm_g, w_router, b_router, w1, w3, w2, dims):
    n, d = x.shape
    wr_pad = jnp.zeros((d, LANES), F32).at[:, :N_EXPERTS].set(w_router.astype(F32))
    br_pad = jnp.zeros((1, LANES), F32).at[0, :N_EXPERTS].set(b_router.astype(F32))
    h, e_tile, g_tile = _route(x, mod, norm_g, wr_pad, br_pad, dims)
    bm = MOE_ROWS
    e_flat = e_tile[:, :2].reshape(-1)
    onehot = (e_flat[:, None] == jnp.arange(N_EXPERTS, dtype=jnp.int32)[None, :]).astype(jnp.int32)
    csum = jnp.cumsum(onehot, axis=0)
    counts = csum[-1]
    rank = jnp.sum(csum * onehot, axis=1) - 1
    padded = (counts + bm - 1) // bm * bm
    pad_end = jnp.cumsum(padded)
    pad_start = pad_end - padded
    slot = pad_start[e_flat] + rank
    n_blocks = (2 * n) // bm + N_EXPERTS
    tok_flat = jnp.arange(2 * n, dtype=jnp.int32) // 2
    slot_tok = jnp.zeros((n_blocks * bm,), jnp.int32).at[slot].set(tok_flat)
    blk_start = jnp.arange(n_blocks, dtype=jnp.int32) * bm
    block_e = jnp.minimum(jnp.sum((pad_end[None, :] <= blk_start[:, None]).astype(jnp.int32), axis=1),
                          N_EXPERTS - 1).astype(jnp.int32)
    n_live = (pad_end[-1] // bm).astype(jnp.int32).reshape(1)
    xg = _gather_rows(h, slot_tok, n_blocks)
    y = _experts(xg, block_e, n_live, w1, w3, w2)
    return _combine(x, mod, g_tile, slot.astype(jnp.int32), y, dims)


def _rope_tables(t_len, n_heads):
    t = jnp.arange(t_len)
    row = (t // GRID_W).astype(F32)
    col = (t % GRID_W).astype(F32)
    n_freq = HEAD_DIM // 4
    inv = ROPE_BASE ** (-jnp.arange(n_freq, dtype=F32) / n_freq)
    ang = jnp.concatenate([row[:, None] * inv, col[:, None] * inv], axis=-1)
    cos, sin = jnp.cos(ang), jnp.sin(ang)
    cos_h = jnp.concatenate([cos, cos], axis=-1)
    sin_h = jnp.concatenate([-sin, sin], axis=-1)
    return jnp.tile(cos_h, (1, n_heads)), jnp.tile(sin_h, (1, n_heads))


def kernel(x, c, ctx, c_ctx, ada_w, ada_b, norm1_g, norm2_g, w_in, w_out, hgrn_lb_raw, hgrn_onorm_g,
           gdn_conv_w, gdn_a_log, gdn_dt_bias, gdn_onorm_g, na_qnorm_g, na_knorm_g, na_rpb, ffn_w1, ffn_w3,
           ffn_w2, moe_router_w, moe_router_b, moe_w1, moe_w3, moe_w2):
    n_batch, t_len, d = x.shape
    l_ctx = ctx.shape[1]
    depth = w_in.shape[0]
    d_a = hgrn_lb_raw.shape[-1]
    d_b = gdn_conv_w.shape[-1] // 3
    n_hb = gdn_a_log.shape[-1]
    n_hc = na_rpb.shape[1]
    d_c = n_hc * HEAD_DIM
    n_lat = n_batch * t_len
    n_all = n_lat + n_batch * l_ctx
    dims = (n_lat, t_len, n_batch, d_a, d_b, d_c)
    assert d_a % HEAD_DIM == 0 and d_b == n_hb * HEAD_DIM
    assert t_len % ROW_TILE == 0 and (n_batch * l_ctx) % ROW_TILE == 0 and l_ctx % TOK_BLK == 0
    assert t_len % GRID_W == 0 and 4 * n_hb <= LANES

    cvec = jnp.zeros((8, d), F32).at[:n_batch].set(c.astype(F32)).at[n_batch].set(c_ctx.astype(F32))
    mod_all = _adaln(cvec, ada_w, ada_b)

    lb_soft = jax.nn.softmax(hgrn_lb_raw.astype(F32), axis=1)
    lower_bound = jnp.cumsum(lb_soft, axis=1) - lb_soft[:, :1]
    cos_t, sin_t = _rope_tables(t_len, n_hb)
    n_gate = 4 * n_hb
    sizes_a, sizes_b = 5 * d_a, 4 * d_b

    xall = jnp.concatenate([x.reshape(n_lat, d), ctx.reshape(n_batch * l_ctx, d)], axis=0).astype(F32)
    for l in range(depth):
        last = l == depth - 1
        mod = mod_all[l, :n_batch + 1].reshape(n_batch + 1, 6, d)
        w = w_in[l]
        gate_cols = jnp.pad(w[:, sizes_a + sizes_b:sizes_a + sizes_b + n_gate], ((0, 0), (0, LANES - n_gate)))
        w_pad = jnp.concatenate([w[:, :sizes_a + sizes_b], gate_cols, w[:, sizes_a + sizes_b + n_gate:]],
                                axis=1).astype(BF16)
        qn_g = jnp.tile(na_qnorm_g[l].astype(F32), n_hc)[None]
        kn_g = jnp.tile(na_knorm_g[l].astype(F32), n_hc)[None]
        pa, pb, pg, qn, kn, vv = _inproj(xall, mod, norm1_g[l][None].astype(F32), w_pad, qn_g, kn_g, dims)

        oa = _hgrn_scan(pa, lower_bound[:, l][:, None, :], dims)
        conv_w = jnp.zeros((8, 3 * d_b), F32).at[:CONV_K].set(gdn_conv_w[l].astype(F32))
        alog_x = jnp.repeat(gdn_a_log[l].astype(F32), HEAD_DIM, axis=-1)[:, None, :]
        dtb_x = jnp.repeat(gdn_dt_bias[l].astype(F32), HEAD_DIM, axis=-1)[:, None, :]
        obf, obb = _gdn_scan(_gdn_prep(pb, pg, conv_w, alog_x, dtb_x, cos_t, sin_t, dims), dims)
        bias_tab = _natten_bias(na_rpb[l], t_len // GRID_W)
        oc = _natten(qn, kn, vv, bias_tab, dims)
        n_out = n_lat if last else n_all
        if not last:
            oc = jnp.concatenate([oc, _ctx_attn(qn, kn, vv, dims)], axis=0)
        na_g = jnp.tile(hgrn_onorm_g[l].astype(F32), d_a // HEAD_DIM)[None]
        nb_g = jnp.tile(gdn_onorm_g[l].astype(F32), n_hb)[None]
        xall_mid = _outproj(xall, mod, oa, pa, obf, obb, pb, oc, na_g, nb_g, w_out[l].astype(BF16), n_out, dims)
        i = l // 2
        if l % 2 == 0:
            xall = _ffn(xall_mid, mod, norm2_g[l][None].astype(F32), ffn_w1[i].astype(BF16),
                        ffn_w3[i].astype(BF16), ffn_w2[i].astype(BF16), dims)
        else:
            xall = _moe(xall_mid, mod, norm2_g[l][None].astype(F32), moe_router_w[i], moe_router_b[i],
                        moe_w1[i], moe_w3[i], moe_w2[i], dims)
    return xall[:n_lat].reshape(n_batch, t_len, d).astype(x.dtype)
```

```python
import functools
import math

import jax
import jax.numpy as jnp
import numpy as np
from jax import lax
from jax.experimental import pallas as pl
from jax.experimental.pallas import tpu as pltpu

F32 = jnp.float32
BF16 = jnp.bfloat16
HI = lax.Precision.HIGHEST

EPS = 1e-6
HEAD_DIM = 64
CHUNK = 64
TOK_BLK = 256
GRID_W = 64
WIN_R = 8
WIN_C = 16
CONV_K = 5
ROPE_BASE = 10000.0
N_EXPERTS = 8
LANES = 128
ROW_TILE = 512
FFN_ROWS = 1024
FF_TILE = 512
MOE_ROWS = 1024
CMB_ROWS = 256
DMA_UNROLL = 8
NA_HEADS_PER_GROUP = 4
VMEM_LIMIT = 56 * 1024 * 1024
MXU_DEPTH = 256
NEG_BIG = -1e30


def _cparams(sem):
    return pltpu.CompilerParams(dimension_semantics=sem, vmem_limit_bytes=VMEM_LIMIT)


def _dot(a, b):
    return jnp.dot(a.astype(BF16), b.astype(BF16), preferred_element_type=F32)


def _dot_nt(a, b):
    return lax.dot_general(a.astype(BF16), b.astype(BF16), (((1,), (1,)), ((), ())),
                           preferred_element_type=F32)


def _dot_tn(a, b):
    return lax.dot_general(a.astype(BF16), b.astype(BF16), (((0,), (0,)), ((), ())),
                           preferred_element_type=F32)


def _dot_hi(a, b):
    return jnp.dot(a, b, precision=HI, preferred_element_type=F32)


def _iota(shape, dim):
    return lax.broadcasted_iota(jnp.int32, shape, dim)


def _head_mask(n):
    return (_iota((n, n), 0) // HEAD_DIM) == (_iota((n, n), 1) // HEAD_DIM)


def _head_sums(x, scale):
    n = x.shape[-1]
    w = min(n, MXU_DEPTH)
    gm = jnp.where(_head_mask(w), scale, 0.0).astype(BF16)
    hi = x.astype(BF16)
    lo = (x - hi.astype(F32)).astype(BF16)
    dot = lambda a: jnp.dot(a, gm, preferred_element_type=F32)
    parts = [dot(lo[:, s:s + w]) + dot(hi[:, s:s + w]) for s in range(0, n, w)]
    return parts[0] if len(parts) == 1 else jnp.concatenate(parts, axis=1)


def _group_mean_sq(x):
    return _head_sums(x * x, 1.0 / HEAD_DIM)


def _silu(x):
    return x * jax.nn.sigmoid(x)


def _modulated_norm(x, g, shift, scale):
    ms = jnp.mean(x * x, axis=-1, keepdims=True)
    y = x * lax.rsqrt(ms + EPS) * g
    return y * (1.0 + scale) + shift


def _mod_group(i, tile, n_lat, t_len, n_batch):
    return jnp.where(i * tile < n_lat, (i * tile) // t_len, n_batch)


def _adaln_kernel(c_ref, w_ref, b_ref, o_ref):
    s = _silu(c_ref[...])
    o_ref[0] = _dot(s, w_ref[0]) + b_ref[0]


def _adaln(cvec, ada_w, ada_b):
    depth, d, d6 = ada_w.shape
    tn = 1024
    return pl.pallas_call(
        _adaln_kernel,
        name="adaln",
        out_shape=jax.ShapeDtypeStruct((depth, 8, d6), F32),
        grid=(depth, d6 // tn),
        in_specs=[pl.BlockSpec((8, d), lambda l, j: (0, 0)),
                  pl.BlockSpec((1, d, tn), lambda l, j: (l, 0, j)),
                  pl.BlockSpec((1, 1, tn), lambda l, j: (l, 0, j))],
        out_specs=pl.BlockSpec((1, 8, tn), lambda l, j: (l, 0, j)),
        compiler_params=_cparams(("parallel", "parallel")),
    )(cvec, ada_w, ada_b.reshape(depth, 1, d6))


def _inproj_kernel(x_ref, m_ref, g_ref, w_ref, qg_ref, kg_ref,
                   pa_ref, pb_ref, pg_ref, qn_ref, kn_ref, vv_ref, *, d_a5, d_b4, d_c):
    h = _modulated_norm(x_ref[...], g_ref[...], m_ref[0, 0:1, :], m_ref[0, 1:2, :]).astype(BF16)
    o0 = 0
    pa_ref[...] = jnp.dot(h, w_ref[:, o0:o0 + d_a5], preferred_element_type=F32)
    o0 += d_a5
    pb_ref[...] = jnp.dot(h, w_ref[:, o0:o0 + d_b4], preferred_element_type=F32)
    o0 += d_b4
    pg_ref[...] = jnp.dot(h, w_ref[:, o0:o0 + LANES], preferred_element_type=F32)
    o0 += LANES
    q = jnp.dot(h, w_ref[:, o0:o0 + d_c], preferred_element_type=F32)
    q = q * lax.rsqrt(_group_mean_sq(q) + EPS) * qg_ref[...]
    qn_ref[...] = (q * HEAD_DIM ** -0.5).astype(BF16)
    o0 += d_c
    k = jnp.dot(h, w_ref[:, o0:o0 + d_c], preferred_element_type=F32)
    k = k * lax.rsqrt(_group_mean_sq(k) + EPS) * kg_ref[...]
    kn_ref[...] = k.astype(BF16)
    o0 += d_c
    vv_ref[...] = jnp.dot(h, w_ref[:, o0:o0 + d_c], preferred_element_type=F32).astype(BF16)


def _inproj(xall, mod, norm_g, w_pad, qn_g, kn_g, dims):
    n_all, d = xall.shape
    n_lat, t_len, n_batch, d_a, d_b, d_c = dims
    tm = ROW_TILE
    row = lambda i: (i, 0)
    const = lambda i: (0, 0)
    grp = lambda i: (_mod_group(i, tm, n_lat, t_len, n_batch), 0, 0)
    kern = functools.partial(_inproj_kernel, d_a5=5 * d_a, d_b4=4 * d_b, d_c=d_c)
    return pl.pallas_call(
        kern,
        name="inproj",
        out_shape=(jax.ShapeDtypeStruct((n_all, 5 * d_a), F32),
                   jax.ShapeDtypeStruct((n_all, 4 * d_b), F32),
                   jax.ShapeDtypeStruct((n_all, LANES), F32),
                   jax.ShapeDtypeStruct((n_all, d_c), BF16),
                   jax.ShapeDtypeStruct((n_all, d_c), BF16),
                   jax.ShapeDtypeStruct((n_all, d_c), BF16)),
        grid=(n_all // tm,),
        in_specs=[pl.BlockSpec((tm, d), row),
                  pl.BlockSpec((1, 6, d), grp),
                  pl.BlockSpec((1, d), const),
                  pl.BlockSpec(w_pad.shape, const),
                  pl.BlockSpec((1, d_c), const),
                  pl.BlockSpec((1, d_c), const)],
        out_specs=(pl.BlockSpec((tm, 5 * d_a), row), pl.BlockSpec((tm, 4 * d_b), row),
                   pl.BlockSpec((tm, LANES), row), pl.BlockSpec((tm, d_c), row),
                   pl.BlockSpec((tm, d_c), row), pl.BlockSpec((tm, d_c), row)),
        compiler_params=_cparams(("parallel",)),
    )(xall, mod, norm_g, w_pad, qn_g, kn_g)


def _scan_block_index(b, d, j, n_lat_blk, n_ctx_blk, n_batch):
    jc = jnp.where(d == 0, j, n_ctx_blk - 1 - j)
    jl = jnp.where(d == 0, j - n_ctx_blk, n_lat_blk - 1 - (j - n_ctx_blk))
    return jnp.where(j < n_ctx_blk, n_batch * n_lat_blk + b * n_ctx_blk + jc, b * n_lat_blk + jl)


def _block_diag(x, mask_bd, n_rep):
    xb = x.astype(BF16)
    return jnp.where(mask_bd, jnp.concatenate([xb] * n_rep, axis=0), jnp.zeros((), BF16))


def _scan_order(idx, d):
    return idx + d * (CHUNK - 1 - 2 * idx)


def _split3(x):
    hi = x.astype(BF16)
    r = x - hi.astype(F32)
    mid = r.astype(BF16)
    lo = (r - mid.astype(F32)).astype(BF16)
    return hi, mid, lo


def _sel_dot(sel, x):
    hi, mid, lo = _split3(x)
    dot = lambda a: jnp.dot(sel, a, preferred_element_type=F32)
    return (dot(lo) + dot(mid)) + dot(hi)


def _dot_sel(x, sel):
    hi, mid, lo = _split3(x)
    dot = lambda a: jnp.dot(a, sel, preferred_element_type=F32)
    return (dot(lo) + dot(mid)) + dot(hi)


HGRN_LEVELS = tuple(CHUNK >> (i + 1) for i in range(int(math.log2(CHUNK))))


def _hgrn_kernel(q_ref, z_ref, v_ref, lb_ref, o_ref, st_ref, *, n_chunk):
    d = pl.program_id(1)
    j = pl.program_id(2)
    n = q_ref.shape[-1]
    n_rep = n // HEAD_DIM

    @pl.when(j == 0)
    def _():
        st_ref[...] = jnp.zeros_like(st_ref)

    lb = lb_ref[0]
    mask_bd = _head_mask(n)
    gones = mask_bd.astype(BF16)
    bd = lambda a: _block_diag(a, mask_bd, n_rep)
    s_t = _scan_order(_iota((CHUNK, CHUNK), 0), d)
    s_u = _scan_order(_iota((CHUNK, CHUNK), 1), d)
    sets = [s_u <= s_t, s_u > s_t]
    for m in HGRN_LEVELS:
        ref = (s_t // (2 * m)) * (2 * m) + m - 1
        sets.append((s_u > jnp.minimum(s_t, ref)) & (s_u <= jnp.maximum(s_t, ref)))
    sel = jnp.concatenate([jnp.where(a, 1.0, 0.0).astype(BF16) for a in sets], axis=0)
    s_row = _scan_order(_iota((CHUNK, n), 0), d)
    s_col = _scan_order(_iota((CHUNK, n), 1) % HEAD_DIM, d)
    eye = s_row == s_col
    second_half = [(s_row // m) % 2 == 1 for m in HGRN_LEVELS]
    same_block = [(s_row // (2 * m)) == (s_col // (2 * m)) for m in HGRN_LEVELS]

    work = []
    for c in range(n_chunk):
        ci = c + d * (n_chunk - 1 - 2 * c)
        sl = pl.ds(pl.multiple_of(ci * CHUNK, CHUNK), CHUNK)
        q = q_ref[sl, :] * HEAD_DIM ** -0.5
        z = z_ref[sl, :]
        v = v_ref[sl, :]
        log_f = (jnp.minimum(z, 0.0) - jnp.log1p(jnp.exp(-jnp.abs(z)))) + jnp.log1p(lb * jnp.exp(-z))
        k = (1.0 - lb) * jax.nn.sigmoid(-z)
        work.append((sl, q, k, v, _sel_dot(sel, log_f)))
    atts = [jnp.where(eye, jnp.dot((q * k).astype(BF16), gones, preferred_element_type=F32), 0.0)
            for _, q, k, _, _ in work]
    for lvl, m in enumerate(HGRN_LEVELS):
        second = second_half[lvl]
        for i, (_, q, k, _, sums) in enumerate(work):
            e = jnp.exp(sums[(2 + lvl) * CHUNK:(3 + lvl) * CHUNK])
            a_m = _dot_nt(jnp.where(second, q * e, 0.0), bd(jnp.where(second, 0.0, k * e)))
            if 2 * m < CHUNK:
                a_m = jnp.where(same_block[lvl], a_m, 0.0)
            atts[i] = atts[i] + a_m
    pre = []
    for (sl, q, k, v, sums), att in zip(work, atts):
        b_incl = sums[0:CHUNK]
        b_after = sums[CHUNK:2 * CHUNK]
        e_end = jnp.exp(b_incl[0:1, :] + b_after[0:1, :])
        upd = jnp.where(mask_bd, _dot_tn(v, k * jnp.exp(b_after)), 0.0)
        pre.append((sl, _dot(att, bd(v)), (q * jnp.exp(b_incl)).astype(BF16), e_end, upd))

    st = st_ref[...]
    for sl, o_intra, q_hat, e_end, upd in pre:
        o_ref[0, sl, :] = o_intra + _dot_nt(q_hat, st)
        st = st * e_end + upd
    st_ref[...] = st


def _hgrn_scan(pa, lb2, dims):
    n_all = pa.shape[0]
    n_lat, t_len, n_batch, d_a, d_b, d_c = dims
    n_lat_blk = t_len // TOK_BLK
    n_ctx_blk = (n_all - n_lat) // n_batch // TOK_BLK
    blk = functools.partial(_scan_block_index, n_lat_blk=n_lat_blk, n_ctx_blk=n_ctx_blk, n_batch=n_batch)
    kern = functools.partial(_hgrn_kernel, n_chunk=TOK_BLK // CHUNK)
    return pl.pallas_call(
        kern,
        name="hgrn_scan",
        out_shape=jax.ShapeDtypeStruct((2, n_all, d_a), F32),
        grid=(n_batch, 2, n_lat_blk + n_ctx_blk),
        in_specs=[pl.BlockSpec((TOK_BLK, d_a), lambda b, d, j: (blk(b, d, j), 0)),
                  pl.BlockSpec((TOK_BLK, d_a), lambda b, d, j: (blk(b, d, j), 1 + d)),
                  pl.BlockSpec((TOK_BLK, d_a), lambda b, d, j: (blk(b, d, j), 3)),
                  pl.BlockSpec((1, 1, d_a), lambda b, d, j: (d, 0, 0))],
        out_specs=pl.BlockSpec((1, TOK_BLK, d_a), lambda b, d, j: (d, blk(b, d, j), 0)),
        scratch_shapes=[pltpu.VMEM((d_a, d_a), F32)],
        compiler_params=_cparams(("parallel", "parallel", "arbitrary")),
    )(pa, pa, pa, lb2)


def _gdn_prep_kernel(x_ref, prev_ref, next_ref, pg_ref, cw_ref, alog_ref, dtb_ref, cos_ref, sin_ref,
                     w_ref, u0_ref, qh_ref, qg_ref, kh_ref, ee_ref,
                     *, n_lat_blk, n_ctx_blk, n_lat_blks_total, d_b):
    i = pl.program_id(0)
    is_lat = i < n_lat_blks_total
    pos = jnp.where(is_lat, i % n_lat_blk, (i - n_lat_blks_total) % n_ctx_blk)
    n_seq_blk = jnp.where(is_lat, n_lat_blk, n_ctx_blk)
    first = pos == 0
    last = pos == n_seq_blk - 1
    x = x_ref[...]
    halo = prev_ref.shape[0]
    prev = jnp.where(first, 0.0, prev_ref[...])
    nxt = jnp.where(last, 0.0, next_ref[...])
    rows = _iota(x.shape, 0)
    tb = x.shape[0]
    half = CONV_K // 2
    acc = x * cw_ref[half:half + 1, :]
    for s in range(1, half + 1):
        xs = pltpu.roll(x, s, 0)
        for r in range(s):
            xs = jnp.where(rows == r, prev[halo - s + r:halo - s + r + 1, :], xs)
        acc = acc + xs * cw_ref[half - s:half - s + 1, :]
        xs = pltpu.roll(x, tb - s, 0)
        for r in range(s):
            xs = jnp.where(rows == tb - s + r, nxt[r:r + 1, :], xs)
        acc = acc + xs * cw_ref[half + s:half + s + 1, :]
    y = _silu(acc)
    def l2n(a):
        return a * lax.rsqrt(_head_sums(a * a, 1.0) + EPS)

    lane = _iota((tb, d_b), 1) % HEAD_DIM
    lo = lane < HEAD_DIM // 2

    def rope(a):
        partner = jnp.where(lo, pltpu.roll(a, d_b - HEAD_DIM // 2, 1), pltpu.roll(a, HEAD_DIM // 2, 1))
        return jnp.where(is_lat, a * cos_ref[...] + partner * sin_ref[...], a)

    q_all = rope(l2n(y[:, 0:d_b])) * HEAD_DIM ** -0.5
    k_all = rope(l2n(y[:, d_b:2 * d_b]))
    v_all = y[:, 2 * d_b:3 * d_b]
    n_h = d_b // HEAD_DIM
    g = pg_ref[...]
    e_r = _iota((LANES, d_b), 0)
    e_c = _iota((LANES, d_b), 1) // HEAD_DIM
    mask_bd = _head_mask(d_b)
    bd = lambda a: _block_diag(a, mask_bd, n_h)
    ones_cc = jnp.ones((CHUNK, CHUNK), BF16)
    ee_ref[...] = jnp.zeros_like(ee_ref)
    work = []
    t_invs = []
    for dd in range(2):
        a_x = _dot_sel(g, jnp.where(e_r == dd * n_h + e_c, 1.0, 0.0).astype(BF16))
        b_x = _dot_sel(g, jnp.where(e_r == (2 + dd) * n_h + e_c, 1.0, 0.0).astype(BF16))
        t = a_x + dtb_ref[dd]
        softplus = jnp.maximum(t, 0.0) + jnp.log1p(jnp.exp(-jnp.abs(t)))
        la_all = -jnp.exp(alog_ref[dd]) * softplus
        be_all = jax.nn.sigmoid(b_x)
        s_t = _scan_order(_iota((CHUNK, CHUNK), 0), dd)
        s_u = _scan_order(_iota((CHUNK, CHUNK), 1), dd)
        m_incl = jnp.where(s_u <= s_t, 1.0, 0.0).astype(BF16)
        s_row = _scan_order(_iota((CHUNK, d_b), 0), dd)
        s_col = _scan_order(_iota((CHUNK, d_b), 1) % HEAD_DIM, dd)
        incl = s_col <= s_row
        strict = s_col < s_row
        m_before = jnp.where(s_row <= s_col, 1.0, 0.0)
        eye_f = jnp.where(s_col == s_row, 1.0, 0.0)
        pair = (s_row // 2) == (s_col // 2)
        levels = []
        m = 4
        while m <= CHUNK:
            levels.append(((s_row // m) == (s_col // m)) & ((s_row // (m // 2)) != (s_col // (m // 2))))
            m *= 2
        end_row = CHUNK - 1 if dd == 0 else 0
        for c in range(tb // CHUNK):
            r = slice(c * CHUNK, (c + 1) * CHUNK)
            q, k, v, la, be = q_all[r], k_all[r], v_all[r], la_all[r], be_all[r]
            g_t = _sel_dot(m_incl, la)
            g_s = _sel_dot(ones_cc, la * m_before)
            gam = jnp.where(incl, jnp.exp(jnp.minimum(g_t - g_s, 0.0)), 0.0)
            k_bd = bd(k)
            kk = _dot_nt(k, k_bd)
            qk = _dot_nt(q, k_bd)
            a = jnp.where(strict, be * kk * gam, 0.0)
            eg = jnp.exp(g_t)
            g_end = g_t[end_row:end_row + 1, :]
            qh_ref[dd, r, :] = (q * eg).astype(BF16)
            qg_ref[dd, r, :] = (qk * gam).astype(BF16)
            kh_ref[dd, r, :] = (k * jnp.exp(g_end - g_t)).astype(BF16)
            ee_ref[dd, 0, c:c + 1, :] = jnp.exp(g_end)
            work.append((dd, r, a, bd(be * eg * k), bd(be * v), levels))
            t_invs.append(eye_f - jnp.where(pair, a, 0.0))
    for lv in range(int(math.log2(CHUNK)) - 1):
        xs = [_dot(jnp.where(wk[5][lv], wk[2], 0.0), bd(t)) for wk, t in zip(work, t_invs)]
        t_invs = [t - _dot(t, bd(x)) for t, x in zip(t_invs, xs)]
    for (dd, r, _, wk_bd, vb_bd, _), t in zip(work, t_invs):
        w_ref[dd, r, :] = _dot(t, wk_bd).astype(BF16)
        u0_ref[dd, r, :] = _dot(t, vb_bd)


def _gdn_prep(pb, pg, conv_w, alog_x, dtb_x, cos_t, sin_t, dims):
    n_all = pb.shape[0]
    n_lat, t_len, n_batch, d_a, d_b, d_c = dims
    n_lat_blk = t_len // TOK_BLK
    n_ctx_blk = (n_all - n_lat) // n_batch // TOK_BLK
    n_blk = n_all // TOK_BLK
    halo = 8
    per = TOK_BLK // halo
    kern = functools.partial(_gdn_prep_kernel, n_lat_blk=n_lat_blk, n_ctx_blk=n_ctx_blk,
                             n_lat_blks_total=n_lat // TOK_BLK, d_b=d_b)
    row = lambda i: (i, 0)
    const2 = lambda i: (0, 0)
    const3 = lambda i: (0, 0, 0)
    tab = lambda i: (jnp.where(i < n_lat // TOK_BLK, i % n_lat_blk, 0), 0)
    both = pl.BlockSpec((2, TOK_BLK, d_b), lambda i: (0, i, 0))
    sds = jax.ShapeDtypeStruct
    return pl.pallas_call(
        kern,
        name="gdn_prep",
        out_shape=(sds((2, n_all, d_b), BF16), sds((2, n_all, d_b), F32), sds((2, n_all, d_b), BF16),
                   sds((2, n_all, d_b), BF16), sds((2, n_all, d_b), BF16), sds((2, n_blk, 8, d_b), F32)),
        grid=(n_blk,),
        in_specs=[pl.BlockSpec((TOK_BLK, 3 * d_b), row),
                  pl.BlockSpec((halo, 3 * d_b), lambda i: (jnp.maximum(i * per - 1, 0), 0)),
                  pl.BlockSpec((halo, 3 * d_b), lambda i: (jnp.minimum((i + 1) * per, n_blk * per - 1), 0)),
                  pl.BlockSpec((TOK_BLK, LANES), row),
                  pl.BlockSpec((8, 3 * d_b), const2),
                  pl.BlockSpec((2, 1, d_b), const3),
                  pl.BlockSpec((2, 1, d_b), const3),
                  pl.BlockSpec((TOK_BLK, d_b), tab),
                  pl.BlockSpec((TOK_BLK, d_b), tab)],
        out_specs=(both, both, both, both, both, pl.BlockSpec((2, 1, 8, d_b), lambda i: (0, i, 0, 0))),
        compiler_params=_cparams(("parallel",)),
    )(pb, pb, pb, pg, conv_w, alog_x, dtb_x, cos_t, sin_t)


def _gdn_kernel(*refs, n_chunk):
    ins = (refs[0:6], refs[6:12])
    outs = refs[12:14]
    st_ref = refs[14]
    j = pl.program_id(1)
    n = outs[0].shape[-1]
    n_rep = n // HEAD_DIM

    @pl.when(j == 0)
    def _():
        st_ref[...] = jnp.zeros_like(st_ref)

    mask_bd = _head_mask(n)
    bd = lambda a: _block_diag(a, mask_bd, n_rep)
    sts = [st_ref[0], st_ref[1]]
    for c in range(n_chunk):
        for dd in range(2):
            w_ref, u0_ref, qh_ref, qg_ref, kh_ref, ee_ref = ins[dd]
            cc = c if dd == 0 else n_chunk - 1 - c
            r = slice(cc * CHUNK, (cc + 1) * CHUNK)
            st = sts[dd]
            u = u0_ref[0, r, :] - _dot_nt(w_ref[0, r, :], st)
            outs[dd][r, :] = _dot_nt(qh_ref[0, r, :], st) + _dot(qg_ref[0, r, :], bd(u))
            sts[dd] = st * ee_ref[0, 0, cc:cc + 1, :] + jnp.where(mask_bd, _dot_tn(u, kh_ref[0, r, :]), 0.0)
    st_ref[0] = sts[0]
    st_ref[1] = sts[1]


def _gdn_scan(wy, dims):
    n_all = wy[0].shape[1]
    n_lat, t_len, n_batch, d_a, d_b, d_c = dims
    n_lat_blk = t_len // TOK_BLK
    n_ctx_blk = (n_all - n_lat) // n_batch // TOK_BLK
    blk = functools.partial(_scan_block_index, n_lat_blk=n_lat_blk, n_ctx_blk=n_ctx_blk, n_batch=n_batch)
    kern = functools.partial(_gdn_kernel, n_chunk=TOK_BLK // CHUNK)
    in_specs = []
    for dd in range(2):
        tok = pl.BlockSpec((1, TOK_BLK, d_b), lambda b, j, dd=dd: (dd, blk(b, dd, j), 0))
        in_specs += [tok] * 5 + [pl.BlockSpec((1, 1, 8, d_b), lambda b, j, dd=dd: (dd, blk(b, dd, j), 0, 0))]
    out_specs = tuple(pl.BlockSpec((TOK_BLK, d_b), lambda b, j, dd=dd: (blk(b, dd, j), 0)) for dd in range(2))
    return pl.pallas_call(
        kern,
        name="gdn_scan",
        out_shape=(jax.ShapeDtypeStruct((n_all, d_b), F32), jax.ShapeDtypeStruct((n_all, d_b), F32)),
        grid=(n_batch, n_lat_blk + n_ctx_blk),
        in_specs=in_specs,
        out_specs=out_specs,
        scratch_shapes=[pltpu.VMEM((2, d_b, d_b), F32)],
        compiler_params=_cparams(("parallel", "arbitrary")),
    )(*wy, *wy)


def _stack_heads(q, n_rep):
    m, n = q.shape
    keep = (_iota((n_rep * m, n), 0) // m) == (_iota((n_rep * m, n), 1) // HEAD_DIM)
    return jnp.where(keep, jnp.concatenate([q] * n_rep, axis=0), jnp.zeros((), q.dtype))


def _fold_heads(o, m, n_rep):
    n = o.shape[1]
    lane_h = _iota((m, n), 1) // HEAD_DIM
    acc = jnp.zeros((m, n), F32)
    for h in range(n_rep):
        acc = acc + jnp.where(lane_h == h, o[h * m:(h + 1) * m, :], 0.0)
    return acc


def _natten_kernel(q_ref, k_ref, v_ref, kc_ref, vc_ref, bias_ref, o_ref, *, n_rows):
    r = pl.program_id(1)
    n = q_ref.shape[-1]
    n_rep = n // HEAD_DIM
    kr = min(WIN_R, n_rows)
    r0 = jnp.clip(r - WIN_R // 2, 0, n_rows - kr)
    sl = pl.ds(pl.multiple_of(r0 * GRID_W, GRID_W), kr * GRID_W)
    dn = (((1,), (1,)), ((), ()))
    n_q = q_ref.shape[0]
    gw = NA_HEADS_PER_GROUP * HEAD_DIM
    for g in range(n // gw):
        lanes = slice(g * gw, (g + 1) * gw)
        qs = _stack_heads(q_ref[:, lanes], NA_HEADS_PER_GROUP)
        rows = slice(g * NA_HEADS_PER_GROUP * n_q, (g + 1) * NA_HEADS_PER_GROUP * n_q)
        s_loc = lax.dot_general(qs, k_ref[sl, lanes], dn, preferred_element_type=F32) + bias_ref[0, rows, :]
        s_ctx = lax.dot_general(qs, kc_ref[:, lanes], dn, preferred_element_type=F32)
        m = jnp.maximum(jnp.max(s_loc, axis=-1, keepdims=True), jnp.max(s_ctx, axis=-1, keepdims=True))
        p_loc = jnp.exp(s_loc - m)
        p_ctx = jnp.exp(s_ctx - m)
        inv = 1.0 / (jnp.sum(p_loc, axis=-1, keepdims=True) + jnp.sum(p_ctx, axis=-1, keepdims=True))
        o = jnp.dot((p_loc * inv).astype(BF16), v_ref[sl, lanes], preferred_element_type=F32)
        o = o + jnp.dot((p_ctx * inv).astype(BF16), vc_ref[:, lanes], preferred_element_type=F32)
        o_ref[:, lanes] = _fold_heads(o, n_q, NA_HEADS_PER_GROUP)


def _natten(qn, kn, vv, bias_tab, dims):
    n_lat, t_len, n_batch, d_a, d_b, d_c = dims
    n_all = qn.shape[0]
    l_ctx = (n_all - n_lat) // n_batch
    n_rows = t_len // GRID_W
    kr = min(WIN_R, n_rows)
    ctx0 = n_lat // l_ctx

    def cfg(b, r):
        r0 = jnp.clip(r - WIN_R // 2, 0, n_rows - kr)
        return (r - r0, 0, 0)

    kern = functools.partial(_natten_kernel, n_rows=n_rows)
    return pl.pallas_call(
        kern,
        name="natten",
        out_shape=jax.ShapeDtypeStruct((n_lat, d_c), F32),
        grid=(n_batch, n_rows),
        in_specs=[pl.BlockSpec((GRID_W, d_c), lambda b, r: (b * n_rows + r, 0)),
                  pl.BlockSpec((t_len, d_c), lambda b, r: (b, 0)),
                  pl.BlockSpec((t_len, d_c), lambda b, r: (b, 0)),
                  pl.BlockSpec((l_ctx, d_c), lambda b, r: (ctx0 + b, 0)),
                  pl.BlockSpec((l_ctx, d_c), lambda b, r: (ctx0 + b, 0)),
                  pl.BlockSpec((1,) + bias_tab.shape[1:], cfg)],
        out_specs=pl.BlockSpec((GRID_W, d_c), lambda b, r: (b * n_rows + r, 0)),
        compiler_params=_cparams(("parallel", "arbitrary")),
    )(qn, kn, vv, kn, vv, bias_tab)


def _ctx_attn_kernel(q_ref, k_ref, v_ref, o_ref):
    n = q_ref.shape[-1]
    n_rep = n // HEAD_DIM
    qs = _stack_heads(q_ref[...], n_rep)
    s = lax.dot_general(qs, k_ref[...], (((1,), (1,)), ((), ())), preferred_element_type=F32)
    p = jnp.exp(s - jnp.max(s, axis=-1, keepdims=True))
    p = p * (1.0 / jnp.sum(p, axis=-1, keepdims=True))
    o = jnp.dot(p.astype(BF16), v_ref[...], preferred_element_type=F32)
    o_ref[...] = _fold_heads(o, q_ref.shape[0], n_rep)


def _ctx_attn(qn, kn, vv, dims):
    n_lat, t_len, n_batch, d_a, d_b, d_c = dims
    n_all = qn.shape[0]
    l_ctx = (n_all - n_lat) // n_batch
    tq = 64
    per = l_ctx // tq
    q0 = n_lat // tq
    c0 = n_lat // l_ctx
    return pl.pallas_call(
        _ctx_attn_kernel,
        name="ctx_attn",
        out_shape=jax.ShapeDtypeStruct((n_all - n_lat, d_c), F32),
        grid=(n_batch, per),
        in_specs=[pl.BlockSpec((tq, d_c), lambda b, i: (q0 + b * per + i, 0)),
                  pl.BlockSpec((l_ctx, d_c), lambda b, i: (c0 + b, 0)),
                  pl.BlockSpec((l_ctx, d_c), lambda b, i: (c0 + b, 0))],
        out_specs=pl.BlockSpec((tq, d_c), lambda b, i: (b * per + i, 0)),
        compiler_params=_cparams(("parallel", "arbitrary")),
    )(qn, kn, vv)


def _natten_bias(rpb, n_rows):
    n_h = rpb.shape[0]
    kr = min(WIN_R, n_rows)
    cols = jnp.arange(GRID_W)
    c0 = jnp.clip(cols - WIN_C // 2, 0, GRID_W - WIN_C)
    kc = jnp.arange(GRID_W)
    in_win = (kc[None, :] >= c0[:, None]) & (kc[None, :] < c0[:, None] + WIN_C)
    dcol = jnp.clip(kc[None, :] - cols[:, None] + WIN_C - 1, 0, 2 * WIN_C - 2)
    toe = jnp.where(in_win[None, None], rpb.astype(F32)[:, :, dcol], NEG_BIG)
    tabs = jnp.stack([toe[:, WIN_R - 1 - delta:WIN_R - 1 - delta + kr] for delta in range(kr)])
    return tabs.transpose(0, 1, 3, 2, 4).reshape(kr, n_h * GRID_W, kr * GRID_W)


def _outproj_kernel(x_ref, m_ref, oa_ref, ga_ref, obf_ref, obb_ref, gb_ref, oc_ref, na_ref, nb_ref, w_ref, o_ref,
                    *, d_a, d_b):
    oa = oa_ref[0] + oa_ref[1]
    ya = oa * lax.rsqrt(_group_mean_sq(oa) + EPS) * na_ref[...] * _silu(ga_ref[...])
    ob = obf_ref[...] + obb_ref[...]
    yb = ob * lax.rsqrt(_group_mean_sq(ob) + EPS) * nb_ref[...] * _silu(gb_ref[...])
    acc = _dot(ya, w_ref[0:d_a, :])
    acc = acc + _dot(yb, w_ref[d_a:d_a + d_b, :])
    acc = acc + _dot(oc_ref[...], w_ref[d_a + d_b:, :])
    o_ref[...] = x_ref[...] + m_ref[0, 2:3, :] * acc


def _outproj(xall, mod, oa, pa, obf, obb, pb, oc, na_g, nb_g, w_out, n_rows_out, dims):
    n_lat, t_len, n_batch, d_a, d_b, d_c = dims
    d = xall.shape[1]
    tm = ROW_TILE
    row = lambda i: (i, 0)
    const = lambda i: (0, 0)
    grp = lambda i: (_mod_group(i, tm, n_lat, t_len, n_batch), 0, 0)
    kern = functools.partial(_outproj_kernel, d_a=d_a, d_b=d_b)
    return pl.pallas_call(
        kern,
        name="outproj",
        out_shape=jax.ShapeDtypeStruct((n_rows_out, d), F32),
        grid=(n_rows_out // tm,),
        in_specs=[pl.BlockSpec((tm, d), row),
                  pl.BlockSpec((1, 6, d), grp),
                  pl.BlockSpec((2, tm, d_a), lambda i: (0, i, 0)),
                  pl.BlockSpec((tm, d_a), lambda i: (i, 4)),
                  pl.BlockSpec((tm, d_b), row),
                  pl.BlockSpec((tm, d_b), row),
                  pl.BlockSpec((tm, d_b), lambda i: (i, 3)),
                  pl.BlockSpec((tm, d_c), row),
                  pl.BlockSpec((1, d_a), const),
                  pl.BlockSpec((1, d_b), const),
                  pl.BlockSpec(w_out.shape, const)],
        out_specs=pl.BlockSpec((tm, d), row),
        compiler_params=_cparams(("parallel",)),
    )(xall, mod, oa, pa, obf, obb, pb, oc, na_g, nb_g, w_out)


def _ffn_kernel(x_ref, m_ref, g_ref, w1_ref, w3_ref, w2_ref, o_ref, h_ref, acc_ref):
    j = pl.program_id(1)

    @pl.when(j == 0)
    def _():
        h_ref[...] = _modulated_norm(x_ref[...], g_ref[...], m_ref[0, 3:4, :], m_ref[0, 4:5, :]).astype(BF16)
        acc_ref[...] = jnp.zeros_like(acc_ref)

    h = h_ref[...]
    a = jnp.dot(h, w1_ref[...], preferred_element_type=F32)
    b = jnp.dot(h, w3_ref[...], preferred_element_type=F32)
    acc_ref[...] += _dot(_silu(a) * b, w2_ref[...])

    @pl.when(j == pl.num_programs(1) - 1)
    def _():
        o_ref[...] = x_ref[...] + m_ref[0, 5:6, :] * acc_ref[...]


def _ffn(xall, mod, norm_g, w1, w3, w2, dims):
    n_lat, t_len, n_batch, d_a, d_b, d_c = dims
    n_rows, d = xall.shape
    d_ff = w1.shape[1]
    tm, tf = FFN_ROWS, FF_TILE
    assert t_len % tm == 0
    grp = lambda i, j: (_mod_group(i, tm, n_lat, t_len, n_batch), 0, 0)
    return pl.pallas_call(
        _ffn_kernel,
        name="ffn",
        out_shape=jax.ShapeDtypeStruct((n_rows, d), F32),
        grid=(pl.cdiv(n_rows, tm), d_ff // tf),
        in_specs=[pl.BlockSpec((tm, d), lambda i, j: (i, 0)),
                  pl.BlockSpec((1, 6, d), grp),
                  pl.BlockSpec((1, d), lambda i, j: (0, 0)),
                  pl.BlockSpec((d, tf), lambda i, j: (0, j)),
                  pl.BlockSpec((d, tf), lambda i, j: (0, j)),
                  pl.BlockSpec((tf, d), lambda i, j: (j, 0))],
        out_specs=pl.BlockSpec((tm, d), lambda i, j: (i, 0)),
        scratch_shapes=[pltpu.VMEM((tm, d), BF16), pltpu.VMEM((tm, d), F32)],
        compiler_params=_cparams(("parallel", "arbitrary")),
    )(xall, mod, norm_g, w1, w3, w2)


def _route_kernel(x_ref, m_ref, g_ref, wr_ref, br_ref, h_ref, e_ref, gt_ref):
    h = _modulated_norm(x_ref[...], g_ref[...], m_ref[0, 3:4, :], m_ref[0, 4:5, :])
    h_ref[...] = h
    lane = _iota((h.shape[0], LANES), 1)
    logits = jnp.where(lane < N_EXPERTS, _dot_hi(h, wr_ref[...]) + br_ref[...], -jnp.inf)
    m1 = jnp.max(logits, axis=-1, keepdims=True)
    lane_f = lane.astype(F32)
    i1 = jnp.min(jnp.where(logits == m1, lane_f, float(LANES)), axis=-1, keepdims=True).astype(jnp.int32)
    rest = jnp.where(lane == i1, -jnp.inf, logits)
    m2 = jnp.max(rest, axis=-1, keepdims=True)
    i2 = jnp.min(jnp.where(rest == m2, lane_f, float(LANES)), axis=-1, keepdims=True).astype(jnp.int32)
    e2 = jnp.exp(m2 - m1)
    g1 = 1.0 / (1.0 + e2)
    g2 = e2 / (1.0 + e2)
    e_ref[...] = jnp.where(lane == 0, i1, jnp.where(lane == 1, i2, 0))
    gt_ref[...] = jnp.where(lane == 0, g1, jnp.where(lane == 1, g2, 0.0))


def _route(x, mod, norm_g, wr_pad, br_pad, dims):
    n_lat, t_len, n_batch, d_a, d_b, d_c = dims
    n, d = x.shape
    tm = ROW_TILE
    row = lambda i: (i, 0)
    const = lambda i: (0, 0)
    grp = lambda i: (_mod_group(i, tm, n_lat, t_len, n_batch), 0, 0)
    return pl.pallas_call(
        _route_kernel,
        name="moe_route",
        out_shape=(jax.ShapeDtypeStruct((n, d), F32), jax.ShapeDtypeStruct((n, LANES), jnp.int32),
                   jax.ShapeDtypeStruct((n, LANES), F32)),
        grid=(n // tm,),
        in_specs=[pl.BlockSpec((tm, d), row), pl.BlockSpec((1, 6, d), grp), pl.BlockSpec((1, d), const),
                  pl.BlockSpec((d, LANES), const), pl.BlockSpec((1, LANES), const)],
        out_specs=(pl.BlockSpec((tm, d), row), pl.BlockSpec((tm, LANES), row), pl.BlockSpec((tm, LANES), row)),
        compiler_params=_cparams(("parallel",)),
    )(x, mod, norm_g, wr_pad, br_pad)


def _row_copy(src_hbm, row, dst_vmem, r, sem):
    return pltpu.make_async_copy(src_hbm.at[pl.ds(row, 1)], dst_vmem.at[pl.ds(r, 1)], sem)


def _gather_kernel(idx_ref, h_hbm, o_ref, sem):
    n_rows = o_ref.shape[0]

    def start(g, carry):
        for u in range(DMA_UNROLL):
            r = g * DMA_UNROLL + u
            _row_copy(h_hbm, idx_ref[0, 0, r], o_ref, r, sem).start(priority=u % 2)
        return carry

    lax.fori_loop(0, n_rows // DMA_UNROLL, start, 0)
    pltpu.make_async_copy(h_hbm.at[pl.ds(0, n_rows)], o_ref, sem).wait()


def _gather_rows(h, slot_tok, n_blocks):
    d = h.shape[1]
    bm = MOE_ROWS
    return pl.pallas_call(
        _gather_kernel,
        name="moe_gather",
        out_shape=jax.ShapeDtypeStruct((n_blocks * bm, d), h.dtype),
        grid=(n_blocks,),
        in_specs=[pl.BlockSpec((1, 1, bm), lambda i: (i, 0, 0), memory_space=pltpu.SMEM),
                  pl.BlockSpec(memory_space=pl.ANY)],
        out_specs=pl.BlockSpec((bm, d), lambda i: (i, 0)),
        scratch_shapes=[pltpu.SemaphoreType.DMA(())],
        compiler_params=_cparams(("arbitrary",)),
    )(slot_tok.reshape(n_blocks, 1, bm), h)


def _expert_kernel(be_ref, nv_ref, x_ref, w1_ref, w3_ref, w2_ref, o_ref, h_ref, acc_ref):
    i = pl.program_id(0)
    j = pl.program_id(1)
    live = i < nv_ref[0]

    @pl.when(live & (j == 0))
    def _():
        h_ref[...] = x_ref[...].astype(BF16)
        acc_ref[...] = jnp.zeros_like(acc_ref)

    @pl.when(live)
    def _():
        h = h_ref[...]
        a = _dot(h, w1_ref[0])
        b = _dot(h, w3_ref[0])
        acc_ref[...] += _dot(_silu(a) * b, w2_ref[0])

    last = j == pl.num_programs(1) - 1

    @pl.when(live & last)
    def _():
        o_ref[...] = acc_ref[...]

    @pl.when(jnp.logical_not(live) & last)
    def _():
        o_ref[...] = jnp.zeros_like(o_ref)


def _experts(xg, block_e, n_live, w1, w3, w2):
    n_slots, d = xg.shape
    d_ff = w1.shape[2]
    bm, tf = MOE_ROWS, FF_TILE
    grid_spec = pltpu.PrefetchScalarGridSpec(
        num_scalar_prefetch=2,
        grid=(n_slots // bm, d_ff // tf),
        in_specs=[pl.BlockSpec((bm, d), lambda i, j, be, nv: (i, 0)),
                  pl.BlockSpec((1, d, tf), lambda i, j, be, nv: (be[i], 0, j)),
                  pl.BlockSpec((1, d, tf), lambda i, j, be, nv: (be[i], 0, j)),
                  pl.BlockSpec((1, tf, d), lambda i, j, be, nv: (be[i], j, 0))],
        out_specs=pl.BlockSpec((bm, d), lambda i, j, be, nv: (i, 0)),
        scratch_shapes=[pltpu.VMEM((bm, d), BF16), pltpu.VMEM((bm, d), F32)])
    return pl.pallas_call(
        _expert_kernel,
        name="moe_experts",
        out_shape=jax.ShapeDtypeStruct((n_slots, d), F32),
        grid_spec=grid_spec,
        compiler_params=_cparams(("arbitrary", "arbitrary")),
    )(block_e, n_live, xg, w1, w3, w2)


def _combine_kernel(idx_ref, x_ref, m_ref, gt_ref, y_hbm, o_ref, buf_ref, sem):
    tm = x_ref.shape[0]

    def start(g, carry):
        for u in range(DMA_UNROLL):
            r = g * DMA_UNROLL + u
            _row_copy(y_hbm, idx_ref[0, 0, 2 * r], buf_ref.at[0], r, sem).start(priority=0)
            _row_copy(y_hbm, idx_ref[0, 0, 2 * r + 1], buf_ref.at[1], r, sem).start(priority=1)
        return carry

    lax.fori_loop(0, tm // DMA_UNROLL, start, 0)
    for k in range(2):
        pltpu.make_async_copy(y_hbm.at[pl.ds(0, tm)], buf_ref.at[k], sem).wait()
    gt = gt_ref[...]
    y = gt[:, 0:1] * buf_ref[0] + gt[:, 1:2] * buf_ref[1]
    o_ref[...] = x_ref[...] + m_ref[0, 5:6, :] * y


def _combine(x, mod, gates, slots, y, dims):
    n_lat, t_len, n_batch, d_a, d_b, d_c = dims
    n, d = x.shape
    tm = CMB_ROWS
    grp = lambda i: (_mod_group(i, tm, n_lat, t_len, n_batch), 0, 0)
    return pl.pallas_call(
        _combine_kernel,
        name="moe_combine",
        out_shape=jax.ShapeDtypeStruct((n, d), F32),
        grid=(n // tm,),
        in_specs=[pl.BlockSpec((1, 1, 2 * tm), lambda i: (i, 0, 0), memory_space=pltpu.SMEM),
                  pl.BlockSpec((tm, d), lambda i: (i, 0)),
                  pl.BlockSpec((1, 6, d), grp),
                  pl.BlockSpec((tm, LANES), lambda i: (i, 0)),
                  pl.BlockSpec(memory_space=pl.ANY)],
        out_specs=pl.BlockSpec((tm, d), lambda i: (i, 0)),
        scratch_shapes=[pltpu.VMEM((2, tm, d), F32), pltpu.SemaphoreType.DMA(())],
        compiler_params=_cparams(("arbitrary",)),
    )(slots.reshape(n // tm, 1, 2 * tm), x, mod, gates, y)


def _moe(x, mod, norm_g, w_router, b_router, w1, w3, w2, dims):
    n, d = x.shape
    wr_pad = jnp.zeros((d, LANES), F32).at[:, :N_EXPERTS].set(w_router.astype(F32))
    br_pad = jnp.zeros((1, LANES), F32).at[0, :N_EXPERTS].set(b_router.astype(F32))
    h, e_tile, g_tile = _route(x, mod, norm_g, wr_pad, br_pad, dims)
    bm = MOE_ROWS
    e_flat = e_tile[:, :2].reshape(-1)
    onehot = (e_flat[:, None] == jnp.arange(N_EXPERTS, dtype=jnp.int32)[None, :]).astype(jnp.int32)
    csum = jnp.cumsum(onehot, axis=0)
    counts = csum[-1]
    rank = jnp.sum(csum * onehot, axis=1) - 1
    padded = (counts + bm - 1) // bm * bm
    pad_end = jnp.cumsum(padded)
    pad_start = pad_end - padded
    slot = pad_start[e_flat] + rank
    n_blocks = (2 * n) // bm + N_EXPERTS
    tok_flat = jnp.arange(2 * n, dtype=jnp.int32) // 2
    slot_tok = jnp.zeros((n_blocks * bm,), jnp.int32).at[slot].set(tok_flat)
    blk_start = jnp.arange(n_blocks, dtype=jnp.int32) * bm
    block_e = jnp.minimum(jnp.sum((pad_end[None, :] <= blk_start[:, None]).astype(jnp.int32), axis=1),
                          N_EXPERTS - 1).astype(jnp.int32)
    n_live = (pad_end[-1] // bm).astype(jnp.int32).reshape(1)
    xg = _gather_rows(h, slot_tok, n_blocks)
    y = _experts(xg, block_e, n_live, w1, w3, w2)
    return _combine(x, mod, g_tile, slot.astype(jnp.int32), y, dims)


def _rope_tables(t_len, n_heads):
    t = jnp.arange(t_len)
    row = (t // GRID_W).astype(F32)
    col = (t % GRID_W).astype(F32)
    n_freq = HEAD_DIM // 4
    inv = ROPE_BASE ** (-jnp.arange(n_freq, dtype=F32) / n_freq)
    ang = jnp.concatenate([row[:, None] * inv, col[:, None] * inv], axis=-1)
    cos, sin = jnp.cos(ang), jnp.sin(ang)
    cos_h = jnp.concatenate([cos, cos], axis=-1)
    sin_h = jnp.concatenate([-sin, sin], axis=-1)
    return jnp.tile(cos_h, (1, n_heads)), jnp.tile(sin_h, (1, n_heads))


def kernel(x, c, ctx, c_ctx, ada_w, ada_b, norm1_g, norm2_g, w_in, w_out, hgrn_lb_raw, hgrn_onorm_g,
           gdn_conv_w, gdn_a_log, gdn_dt_bias, gdn_onorm_g, na_qnorm_g, na_knorm_g, na_rpb, ffn_w1, ffn_w3,
           ffn_w2, moe_router_w, moe_router_b, moe_w1, moe_w3, moe_w2):
    n_batch, t_len, d = x.shape
    l_ctx = ctx.shape[1]
    depth = w_in.shape[0]
    d_a = hgrn_lb_raw.shape[-1]
    d_b = gdn_conv_w.shape[-1] // 3
    n_hb = gdn_a_log.shape[-1]
    n_hc = na_rpb.shape[1]
    d_c = n_hc * HEAD_DIM
    n_lat = n_batch * t_len
    n_all = n_lat + n_batch * l_ctx
    dims = (n_lat, t_len, n_batch, d_a, d_b, d_c)
    assert d_a % HEAD_DIM == 0 and d_b == n_hb * HEAD_DIM
    assert t_len % ROW_TILE == 0 and (n_batch * l_ctx) % ROW_TILE == 0 and l_ctx % TOK_BLK == 0
    assert t_len % GRID_W == 0 and 4 * n_hb <= LANES

    cvec = jnp.zeros((8, d), F32).at[:n_batch].set(c.astype(F32)).at[n_batch].set(c_ctx.astype(F32))
    mod_all = _adaln(cvec, ada_w, ada_b)

    lb_soft = jax.nn.softmax(hgrn_lb_raw.astype(F32), axis=1)
    lower_bound = jnp.cumsum(lb_soft, axis=1) - lb_soft[:, :1]
    cos_t, sin_t = _rope_tables(t_len, n_hb)
    n_gate = 4 * n_hb
    sizes_a, sizes_b = 5 * d_a, 4 * d_b

    xall = jnp.concatenate([x.reshape(n_lat, d), ctx.reshape(n_batch * l_ctx, d)], axis=0).astype(F32)
    for l in range(depth):
        last = l == depth - 1
        mod = mod_all[l, :n_batch + 1].reshape(n_batch + 1, 6, d)
        w = w_in[l]
        gate_cols = jnp.pad(w[:, sizes_a + sizes_b:sizes_a + sizes_b + n_gate], ((0, 0), (0, LANES - n_gate)))
        w_pad = jnp.concatenate([w[:, :sizes_a + sizes_b], gate_cols, w[:, sizes_a + sizes_b + n_gate:]],
                                axis=1).astype(BF16)
        qn_g = jnp.tile(na_qnorm_g[l].astype(F32), n_hc)[None]
        kn_g = jnp.tile(na_knorm_g[l].astype(F32), n_hc)[None]
        pa, pb, pg, qn, kn, vv = _inproj(xall, mod, norm1_g[l][None].astype(F32), w_pad, qn_g, kn_g, dims)

        oa = _hgrn_scan(pa, lower_bound[:, l][:, None, :], dims)
        conv_w = jnp.zeros((8, 3 * d_b), F32).at[:CONV_K].set(gdn_conv_w[l].astype(F32))
        alog_x = jnp.repeat(gdn_a_log[l].astype(F32), HEAD_DIM, axis=-1)[:, None, :]
        dtb_x = jnp.repeat(gdn_dt_bias[l].astype(F32), HEAD_DIM, axis=-1)[:, None, :]
        obf, obb = _gdn_scan(_gdn_prep(pb, pg, conv_w, alog_x, dtb_x, cos_t, sin_t, dims), dims)
        bias_tab = _natten_bias(na_rpb[l], t_len // GRID_W)
        oc = _natten(qn, kn, vv, bias_tab, dims)
        n_out = n_lat if last else n_all
        if not last:
            oc = jnp.concatenate([oc, _ctx_attn(qn, kn, vv, dims)], axis=0)
        na_g = jnp.tile(hgrn_onorm_g[l].astype(F32), d_a // HEAD_DIM)[None]
        nb_g = jnp.tile(gdn_onorm_g[l].astype(F32), n_hb)[None]
        xall_mid = _outproj(xall, mod, oa, pa, obf, obb, pb, oc, na_g, nb_g, w_out[l].astype(BF16), n_out, dims)
        i = l // 2
        if l % 2 == 0:
            xall = _ffn(xall_mid, mod, norm2_g[l][None].astype(F32), ffn_w1[i].astype(BF16),
                        ffn_w3[i].astype(BF16), ffn_w2[i].astype(BF16), dims)
        else:
            xall = _moe(xall_mid, mod, norm2_g[l][None].astype(F32), moe_router_w[i], moe_router_b[i],
                        moe_w1[i], moe_w3[i], moe_w2[i], dims)
    return xall[:n_lat].reshape(n_batch, t_len, d).astype(x.dtype)
```

```python
import functools
import math

import jax
import jax.numpy as jnp
import numpy as np
from jax import lax
from jax.experimental import pallas as pl
from jax.experimental.pallas import tpu as pltpu

F32 = jnp.float32
BF16 = jnp.bfloat16
HI = lax.Precision.HIGHEST

EPS = 1e-6
HEAD_DIM = 64
CHUNK = 64
TOK_BLK = 256
GRID_W = 64
WIN_R = 8
WIN_C = 16
CONV_K = 5
ROPE_BASE = 10000.0
N_EXPERTS = 8
LANES = 128
SUBLANES = 8
ROW_TILE = 512
FFN_ROWS = 1024
FF_TILE = 512
MOE_ROWS = 1024
GATHER_PARTS = 4
NA_HEADS_PER_GROUP = 4
VMEM_LIMIT = 56 * 1024 * 1024
MXU_DEPTH = 256
NEG_BIG = -1e30


def _cparams(sem):
    return pltpu.CompilerParams(dimension_semantics=sem, vmem_limit_bytes=VMEM_LIMIT)


def _dot(a, b):
    return jnp.dot(a.astype(BF16), b.astype(BF16), preferred_element_type=F32)


def _dot_nt(a, b):
    return lax.dot_general(a.astype(BF16), b.astype(BF16), (((1,), (1,)), ((), ())),
                           preferred_element_type=F32)


def _dot_tn(a, b):
    return lax.dot_general(a.astype(BF16), b.astype(BF16), (((0,), (0,)), ((), ())),
                           preferred_element_type=F32)


def _dot_hi(a, b):
    return jnp.dot(a, b, precision=HI, preferred_element_type=F32)


def _iota(shape, dim):
    return lax.broadcasted_iota(jnp.int32, shape, dim)


def _head_mask(n):
    return (_iota((n, n), 0) // HEAD_DIM) == (_iota((n, n), 1) // HEAD_DIM)


def _head_sums(x, scale):
    n = x.shape[-1]
    w = min(n, MXU_DEPTH)
    gm = jnp.where(_head_mask(w), scale, 0.0).astype(BF16)
    hi = x.astype(BF16)
    lo = (x - hi.astype(F32)).astype(BF16)
    dot = lambda a: jnp.dot(a, gm, preferred_element_type=F32)
    parts = [dot(lo[:, s:s + w]) + dot(hi[:, s:s + w]) for s in range(0, n, w)]
    return parts[0] if len(parts) == 1 else jnp.concatenate(parts, axis=1)


def _group_mean_sq(x):
    return _head_sums(x * x, 1.0 / HEAD_DIM)


def _silu(x):
    return x * jax.nn.sigmoid(x)


def _modulated_norm(x, g, shift, scale):
    ms = jnp.mean(x * x, axis=-1, keepdims=True)
    y = x * lax.rsqrt(ms + EPS) * g
    return y * (1.0 + scale) + shift


def _mod_group(i, tile, n_lat, t_len, n_batch):
    return jnp.where(i * tile < n_lat, (i * tile) // t_len, n_batch)


def _adaln_kernel(c_ref, w_ref, b_ref, o_ref):
    s = _silu(c_ref[...])
    o_ref[0] = _dot(s, w_ref[0]) + b_ref[0]


def _adaln(cvec, ada_w, ada_b):
    depth, d, d6 = ada_w.shape
    tn = 1024
    return pl.pallas_call(
        _adaln_kernel,
        name="adaln",
        out_shape=jax.ShapeDtypeStruct((depth, 8, d6), F32),
        grid=(depth, d6 // tn),
        in_specs=[pl.BlockSpec((8, d), lambda l, j: (0, 0)),
                  pl.BlockSpec((1, d, tn), lambda l, j: (l, 0, j)),
                  pl.BlockSpec((1, 1, tn), lambda l, j: (l, 0, j))],
        out_specs=pl.BlockSpec((1, 8, tn), lambda l, j: (l, 0, j)),
        compiler_params=_cparams(("parallel", "parallel")),
    )(cvec, ada_w, ada_b.reshape(depth, 1, d6))


def _inproj_kernel(x_ref, m_ref, g_ref, w_ref, qg_ref, kg_ref,
                   pa_ref, pb_ref, pg_ref, qn_ref, kn_ref, vv_ref, *, d_a5, d_b4, d_c):
    h = _modulated_norm(x_ref[...], g_ref[...], m_ref[0, 0:1, :], m_ref[0, 1:2, :]).astype(BF16)
    o0 = 0
    pa_ref[...] = jnp.dot(h, w_ref[:, o0:o0 + d_a5], preferred_element_type=F32)
    o0 += d_a5
    pb_ref[...] = jnp.dot(h, w_ref[:, o0:o0 + d_b4], preferred_element_type=F32)
    o0 += d_b4
    pg_ref[...] = jnp.dot(h, w_ref[:, o0:o0 + LANES], preferred_element_type=F32)
    o0 += LANES
    q = jnp.dot(h, w_ref[:, o0:o0 + d_c], preferred_element_type=F32)
    q = q * lax.rsqrt(_group_mean_sq(q) + EPS) * qg_ref[...]
    qn_ref[...] = (q * HEAD_DIM ** -0.5).astype(BF16)
    o0 += d_c
    k = jnp.dot(h, w_ref[:, o0:o0 + d_c], preferred_element_type=F32)
    k = k * lax.rsqrt(_group_mean_sq(k) + EPS) * kg_ref[...]
    kn_ref[...] = k.astype(BF16)
    o0 += d_c
    vv_ref[...] = jnp.dot(h, w_ref[:, o0:o0 + d_c], preferred_element_type=F32).astype(BF16)


def _inproj(xall, mod, norm_g, w_pad, qn_g, kn_g, dims):
    n_all, d = xall.shape
    n_lat, t_len, n_batch, d_a, d_b, d_c = dims
    tm = ROW_TILE
    row = lambda i: (i, 0)
    const = lambda i: (0, 0)
    grp = lambda i: (_mod_group(i, tm, n_lat, t_len, n_batch), 0, 0)
    kern = functools.partial(_inproj_kernel, d_a5=5 * d_a, d_b4=4 * d_b, d_c=d_c)
    return pl.pallas_call(
        kern,
        name="inproj",
        out_shape=(jax.ShapeDtypeStruct((n_all, 5 * d_a), F32),
                   jax.ShapeDtypeStruct((n_all, 4 * d_b), F32),
                   jax.ShapeDtypeStruct((n_all, LANES), F32),
                   jax.ShapeDtypeStruct((n_all, d_c), BF16),
                   jax.ShapeDtypeStruct((n_all, d_c), BF16),
                   jax.ShapeDtypeStruct((n_all, d_c), BF16)),
        grid=(n_all // tm,),
        in_specs=[pl.BlockSpec((tm, d), row),
                  pl.BlockSpec((1, 6, d), grp),
                  pl.BlockSpec((1, d), const),
                  pl.BlockSpec(w_pad.shape, const),
                  pl.BlockSpec((1, d_c), const),
                  pl.BlockSpec((1, d_c), const)],
        out_specs=(pl.BlockSpec((tm, 5 * d_a), row), pl.BlockSpec((tm, 4 * d_b), row),
                   pl.BlockSpec((tm, LANES), row), pl.BlockSpec((tm, d_c), row),
                   pl.BlockSpec((tm, d_c), row), pl.BlockSpec((tm, d_c), row)),
        compiler_params=_cparams(("parallel",)),
    )(xall, mod, norm_g, w_pad, qn_g, kn_g)


def _scan_block_index(b, d, j, n_lat_blk, n_ctx_blk, n_batch):
    jc = jnp.where(d == 0, j, n_ctx_blk - 1 - j)
    jl = jnp.where(d == 0, j - n_ctx_blk, n_lat_blk - 1 - (j - n_ctx_blk))
    return jnp.where(j < n_ctx_blk, n_batch * n_lat_blk + b * n_ctx_blk + jc, b * n_lat_blk + jl)


def _block_diag(x, mask_bd, n_rep):
    xb = x.astype(BF16)
    return jnp.where(mask_bd, jnp.concatenate([xb] * n_rep, axis=0), jnp.zeros((), BF16))


def _scan_order(idx, d):
    return idx + d * (CHUNK - 1 - 2 * idx)


def _split3(x):
    hi = x.astype(BF16)
    r = x - hi.astype(F32)
    mid = r.astype(BF16)
    lo = (r - mid.astype(F32)).astype(BF16)
    return hi, mid, lo


def _sel_dot(sel, x):
    hi, mid, lo = _split3(x)
    dot = lambda a: jnp.dot(sel, a, preferred_element_type=F32)
    return (dot(lo) + dot(mid)) + dot(hi)


def _dot_sel(x, sel):
    hi, mid, lo = _split3(x)
    dot = lambda a: jnp.dot(a, sel, preferred_element_type=F32)
    return (dot(lo) + dot(mid)) + dot(hi)


HGRN_LEVELS = tuple(CHUNK >> (i + 1) for i in range(int(math.log2(CHUNK))))


def _hgrn_kernel(q_ref, z_ref, v_ref, lb_ref, o_ref, st_ref, *, n_chunk):
    d = pl.program_id(1)
    j = pl.program_id(2)
    n = q_ref.shape[-1]
    n_rep = n // HEAD_DIM

    @pl.when(j == 0)
    def _():
        st_ref[...] = jnp.zeros_like(st_ref)

    lb = lb_ref[0]
    mask_bd = _head_mask(n)
    gones = mask_bd.astype(BF16)
    bd = lambda a: _block_diag(a, mask_bd, n_rep)
    s_t = _scan_order(_iota((CHUNK, CHUNK), 0), d)
    s_u = _scan_order(_iota((CHUNK, CHUNK), 1), d)
    sets = [s_u <= s_t, s_u > s_t]
    for m in HGRN_LEVELS:
        ref = (s_t // (2 * m)) * (2 * m) + m - 1
        sets.append((s_u > jnp.minimum(s_t, ref)) & (s_u <= jnp.maximum(s_t, ref)))
    sel = jnp.concatenate([jnp.where(a, 1.0, 0.0).astype(BF16) for a in sets], axis=0)
    s_row = _scan_order(_iota((CHUNK, n), 0), d)
    s_col = _scan_order(_iota((CHUNK, n), 1) % HEAD_DIM, d)
    eye = s_row == s_col
    second_half = [(s_row // m) % 2 == 1 for m in HGRN_LEVELS]
    same_block = [(s_row // (2 * m)) == (s_col // (2 * m)) for m in HGRN_LEVELS]

    work = []
    for c in range(n_chunk):
        ci = c + d * (n_chunk - 1 - 2 * c)
        sl = pl.ds(pl.multiple_of(ci * CHUNK, CHUNK), CHUNK)
        q = q_ref[sl, :] * HEAD_DIM ** -0.5
        z = z_ref[sl, :]
        v = v_ref[sl, :]
        log_f = (jnp.minimum(z, 0.0) - jnp.log1p(jnp.exp(-jnp.abs(z)))) + jnp.log1p(lb * jnp.exp(-z))
        k = (1.0 - lb) * jax.nn.sigmoid(-z)
        work.append((sl, q, k, v, _sel_dot(sel, log_f)))
    atts = [jnp.where(eye, jnp.dot((q * k).astype(BF16), gones, preferred_element_type=F32), 0.0)
            for _, q, k, _, _ in work]
    for lvl, m in enumerate(HGRN_LEVELS):
        second = second_half[lvl]
        for i, (_, q, k, _, sums) in enumerate(work):
            e = jnp.exp(sums[(2 + lvl) * CHUNK:(3 + lvl) * CHUNK])
            a_m = _dot_nt(jnp.where(second, q * e, 0.0), bd(jnp.where(second, 0.0, k * e)))
            if 2 * m < CHUNK:
                a_m = jnp.where(same_block[lvl], a_m, 0.0)
            atts[i] = atts[i] + a_m
    pre = []
    for (sl, q, k, v, sums), att in zip(work, atts):
        b_incl = sums[0:CHUNK]
        b_after = sums[CHUNK:2 * CHUNK]
        e_end = jnp.exp(b_incl[0:1, :] + b_after[0:1, :])
        upd = jnp.where(mask_bd, _dot_tn(v, k * jnp.exp(b_after)), 0.0)
        pre.append((sl, _dot(att, bd(v)), (q * jnp.exp(b_incl)).astype(BF16), e_end, upd))

    st = st_ref[...]
    for sl, o_intra, q_hat, e_end, upd in pre:
        o_ref[0, sl, :] = o_intra + _dot_nt(q_hat, st)
        st = st * e_end + upd
    st_ref[...] = st


def _hgrn_scan(pa, lb2, dims):
    n_all = pa.shape[0]
    n_lat, t_len, n_batch, d_a, d_b, d_c = dims
    n_lat_blk = t_len // TOK_BLK
    n_ctx_blk = (n_all - n_lat) // n_batch // TOK_BLK
    blk = functools.partial(_scan_block_index, n_lat_blk=n_lat_blk, n_ctx_blk=n_ctx_blk, n_batch=n_batch)
    kern = functools.partial(_hgrn_kernel, n_chunk=TOK_BLK // CHUNK)
    return pl.pallas_call(
        kern,
        name="hgrn_scan",
        out_shape=jax.ShapeDtypeStruct((2, n_all, d_a), F32),
        grid=(n_batch, 2, n_lat_blk + n_ctx_blk),
        in_specs=[pl.BlockSpec((TOK_BLK, d_a), lambda b, d, j: (blk(b, d, j), 0)),
                  pl.BlockSpec((TOK_BLK, d_a), lambda b, d, j: (blk(b, d, j), 1 + d)),
                  pl.BlockSpec((TOK_BLK, d_a), lambda b, d, j: (blk(b, d, j), 3)),
                  pl.BlockSpec((1, 1, d_a), lambda b, d, j: (d, 0, 0))],
        out_specs=pl.BlockSpec((1, TOK_BLK, d_a), lambda b, d, j: (d, blk(b, d, j), 0)),
        scratch_shapes=[pltpu.VMEM((d_a, d_a), F32)],
        compiler_params=_cparams(("parallel", "parallel", "arbitrary")),
    )(pa, pa, pa, lb2)


def _gdn_prep_kernel(x_ref, prev_ref, next_ref, pg_ref, cw_ref, alog_ref, dtb_ref, cos_ref, sin_ref,
                     w_ref, u0_ref, qh_ref, qg_ref, kh_ref, ee_ref,
                     *, n_lat_blk, n_ctx_blk, n_lat_blks_total, d_b):
    i = pl.program_id(0)
    is_lat = i < n_lat_blks_total
    pos = jnp.where(is_lat, i % n_lat_blk, (i - n_lat_blks_total) % n_ctx_blk)
    n_seq_blk = jnp.where(is_lat, n_lat_blk, n_ctx_blk)
    first = pos == 0
    last = pos == n_seq_blk - 1
    x = x_ref[...]
    halo = prev_ref.shape[0]
    prev = jnp.where(first, 0.0, prev_ref[...])
    nxt = jnp.where(last, 0.0, next_ref[...])
    rows = _iota(x.shape, 0)
    tb = x.shape[0]
    half = CONV_K // 2
    acc = x * cw_ref[half:half + 1, :]
    for s in range(1, half + 1):
        xs = pltpu.roll(x, s, 0)
        for r in range(s):
            xs = jnp.where(rows == r, prev[halo - s + r:halo - s + r + 1, :], xs)
        acc = acc + xs * cw_ref[half - s:half - s + 1, :]
        xs = pltpu.roll(x, tb - s, 0)
        for r in range(s):
            xs = jnp.where(rows == tb - s + r, nxt[r:r + 1, :], xs)
        acc = acc + xs * cw_ref[half + s:half + s + 1, :]
    y = _silu(acc)
    def l2n(a):
        return a * lax.rsqrt(_head_sums(a * a, 1.0) + EPS)

    lane = _iota((tb, d_b), 1) % HEAD_DIM
    lo = lane < HEAD_DIM // 2

    def rope(a):
        partner = jnp.where(lo, pltpu.roll(a, d_b - HEAD_DIM // 2, 1), pltpu.roll(a, HEAD_DIM // 2, 1))
        return jnp.where(is_lat, a * cos_ref[...] + partner * sin_ref[...], a)

    q_all = rope(l2n(y[:, 0:d_b])) * HEAD_DIM ** -0.5
    k_all = rope(l2n(y[:, d_b:2 * d_b]))
    v_all = y[:, 2 * d_b:3 * d_b]
    n_h = d_b // HEAD_DIM
    g = pg_ref[...]
    e_r = _iota((LANES, d_b), 0)
    e_c = _iota((LANES, d_b), 1) // HEAD_DIM
    mask_bd = _head_mask(d_b)
    bd = lambda a: _block_diag(a, mask_bd, n_h)
    ones_cc = jnp.ones((CHUNK, CHUNK), BF16)
    ee_ref[...] = jnp.zeros_like(ee_ref)
    work = []
    t_invs = []
    for dd in range(2):
        a_x = _dot_sel(g, jnp.where(e_r == dd * n_h + e_c, 1.0, 0.0).astype(BF16))
        b_x = _dot_sel(g, jnp.where(e_r == (2 + dd) * n_h + e_c, 1.0, 0.0).astype(BF16))
        t = a_x + dtb_ref[dd]
        softplus = jnp.maximum(t, 0.0) + jnp.log1p(jnp.exp(-jnp.abs(t)))
        la_all = -jnp.exp(alog_ref[dd]) * softplus
        be_all = jax.nn.sigmoid(b_x)
        s_t = _scan_order(_iota((CHUNK, CHUNK), 0), dd)
        s_u = _scan_order(_iota((CHUNK, CHUNK), 1), dd)
        m_incl = jnp.where(s_u <= s_t, 1.0, 0.0).astype(BF16)
        s_row = _scan_order(_iota((CHUNK, d_b), 0), dd)
        s_col = _scan_order(_iota((CHUNK, d_b), 1) % HEAD_DIM, dd)
        incl = s_col <= s_row
        strict = s_col < s_row
        m_before = jnp.where(s_row <= s_col, 1.0, 0.0)
        eye_f = jnp.where(s_col == s_row, 1.0, 0.0)
        pair = (s_row // 2) == (s_col // 2)
        levels = []
        m = 4
        while m <= CHUNK:
            levels.append(((s_row // m) == (s_col // m)) & ((s_row // (m // 2)) != (s_col // (m // 2))))
            m *= 2
        end_row = CHUNK - 1 if dd == 0 else 0
        for c in range(tb // CHUNK):
            r = slice(c * CHUNK, (c + 1) * CHUNK)
            q, k, v, la, be = q_all[r], k_all[r], v_all[r], la_all[r], be_all[r]
            g_t = _sel_dot(m_incl, la)
            g_s = _sel_dot(ones_cc, la * m_before)
            gam = jnp.where(incl, jnp.exp(jnp.minimum(g_t - g_s, 0.0)), 0.0)
            k_bd = bd(k)
            kk = _dot_nt(k, k_bd)
            qk = _dot_nt(q, k_bd)
            a = jnp.where(strict, be * kk * gam, 0.0)
            eg = jnp.exp(g_t)
            g_end = g_t[end_row:end_row + 1, :]
            qh_ref[dd, r, :] = (q * eg).astype(BF16)
            qg_ref[dd, r, :] = (qk * gam).astype(BF16)
            kh_ref[dd, r, :] = (k * jnp.exp(g_end - g_t)).astype(BF16)
            ee_ref[dd, 0, c:c + 1, :] = jnp.exp(g_end)
            work.append((dd, r, a, bd(be * eg * k), bd(be * v), levels))
            t_invs.append(eye_f - jnp.where(pair, a, 0.0))
    for lv in range(int(math.log2(CHUNK)) - 1):
        xs = [_dot(jnp.where(wk[5][lv], wk[2], 0.0), bd(t)) for wk, t in zip(work, t_invs)]
        t_invs = [t - _dot(t, bd(x)) for t, x in zip(t_invs, xs)]
    for (dd, r, _, wk_bd, vb_bd, _), t in zip(work, t_invs):
        w_ref[dd, r, :] = _dot(t, wk_bd).astype(BF16)
        u0_ref[dd, r, :] = _dot(t, vb_bd)


def _gdn_prep(pb, pg, conv_w, alog_x, dtb_x, cos_t, sin_t, dims):
    n_all = pb.shape[0]
    n_lat, t_len, n_batch, d_a, d_b, d_c = dims
    n_lat_blk = t_len // TOK_BLK
    n_ctx_blk = (n_all - n_lat) // n_batch // TOK_BLK
    n_blk = n_all // TOK_BLK
    halo = 8
    per = TOK_BLK // halo
    kern = functools.partial(_gdn_prep_kernel, n_lat_blk=n_lat_blk, n_ctx_blk=n_ctx_blk,
                             n_lat_blks_total=n_lat // TOK_BLK, d_b=d_b)
    row = lambda i: (i, 0)
    const2 = lambda i: (0, 0)
    const3 = lambda i: (0, 0, 0)
    tab = lambda i: (jnp.where(i < n_lat // TOK_BLK, i % n_lat_blk, 0), 0)
    both = pl.BlockSpec((2, TOK_BLK, d_b), lambda i: (0, i, 0))
    sds = jax.ShapeDtypeStruct
    return pl.pallas_call(
        kern,
        name="gdn_prep",
        out_shape=(sds((2, n_all, d_b), BF16), sds((2, n_all, d_b), F32), sds((2, n_all, d_b), BF16),
                   sds((2, n_all, d_b), BF16), sds((2, n_all, d_b), BF16), sds((2, n_blk, 8, d_b), F32)),
        grid=(n_blk,),
        in_specs=[pl.BlockSpec((TOK_BLK, 3 * d_b), row),
                  pl.BlockSpec((halo, 3 * d_b), lambda i: (jnp.maximum(i * per - 1, 0), 0)),
                  pl.BlockSpec((halo, 3 * d_b), lambda i: (jnp.minimum((i + 1) * per, n_blk * per - 1), 0)),
                  pl.BlockSpec((TOK_BLK, LANES), row),
                  pl.BlockSpec((8, 3 * d_b), const2),
                  pl.BlockSpec((2, 1, d_b), const3),
                  pl.BlockSpec((2, 1, d_b), const3),
                  pl.BlockSpec((TOK_BLK, d_b), tab),
                  pl.BlockSpec((TOK_BLK, d_b), tab)],
        out_specs=(both, both, both, both, both, pl.BlockSpec((2, 1, 8, d_b), lambda i: (0, i, 0, 0))),
        compiler_params=_cparams(("parallel",)),
    )(pb, pb, pb, pg, conv_w, alog_x, dtb_x, cos_t, sin_t)


def _gdn_kernel(*refs, n_chunk):
    ins = (refs[0:6], refs[6:12])
    outs = refs[12:14]
    st_ref = refs[14]
    j = pl.program_id(1)
    n = outs[0].shape[-1]
    n_rep = n // HEAD_DIM

    @pl.when(j == 0)
    def _():
        st_ref[...] = jnp.zeros_like(st_ref)

    mask_bd = _head_mask(n)
    bd = lambda a: _block_diag(a, mask_bd, n_rep)
    sts = [st_ref[0], st_ref[1]]
    for c in range(n_chunk):
        for dd in range(2):
            w_ref, u0_ref, qh_ref, qg_ref, kh_ref, ee_ref = ins[dd]
            cc = c if dd == 0 else n_chunk - 1 - c
            r = slice(cc * CHUNK, (cc + 1) * CHUNK)
            st = sts[dd]
            u = u0_ref[0, r, :] - _dot_nt(w_ref[0, r, :], st)
            outs[dd][r, :] = _dot_nt(qh_ref[0, r, :], st) + _dot(qg_ref[0, r, :], bd(u))
            sts[dd] = st * ee_ref[0, 0, cc:cc + 1, :] + jnp.where(mask_bd, _dot_tn(u, kh_ref[0, r, :]), 0.0)
    st_ref[0] = sts[0]
    st_ref[1] = sts[1]


def _gdn_scan(wy, dims):
    n_all = wy[0].shape[1]
    n_lat, t_len, n_batch, d_a, d_b, d_c = dims
    n_lat_blk = t_len // TOK_BLK
    n_ctx_blk = (n_all - n_lat) // n_batch // TOK_BLK
    blk = functools.partial(_scan_block_index, n_lat_blk=n_lat_blk, n_ctx_blk=n_ctx_blk, n_batch=n_batch)
    kern = functools.partial(_gdn_kernel, n_chunk=TOK_BLK // CHUNK)
    in_specs = []
    for dd in range(2):
        tok = pl.BlockSpec((1, TOK_BLK, d_b), lambda b, j, dd=dd: (dd, blk(b, dd, j), 0))
        in_specs += [tok] * 5 + [pl.BlockSpec((1, 1, 8, d_b), lambda b, j, dd=dd: (dd, blk(b, dd, j), 0, 0))]
    out_specs = tuple(pl.BlockSpec((TOK_BLK, d_b), lambda b, j, dd=dd: (blk(b, dd, j), 0)) for dd in range(2))
    return pl.pallas_call(
        kern,
        name="gdn_scan",
        out_shape=(jax.ShapeDtypeStruct((n_all, d_b), F32), jax.ShapeDtypeStruct((n_all, d_b), F32)),
        grid=(n_batch, n_lat_blk + n_ctx_blk),
        in_specs=in_specs,
        out_specs=out_specs,
        scratch_shapes=[pltpu.VMEM((2, d_b, d_b), F32)],
        compiler_params=_cparams(("parallel", "arbitrary")),
    )(*wy, *wy)


def _stack_heads(q, n_rep):
    m, n = q.shape
    keep = (_iota((n_rep * m, n), 0) // m) == (_iota((n_rep * m, n), 1) // HEAD_DIM)
    return jnp.where(keep, jnp.concatenate([q] * n_rep, axis=0), jnp.zeros((), q.dtype))


def _fold_heads(o, m, n_rep):
    n = o.shape[1]
    lane_h = _iota((m, n), 1) // HEAD_DIM
    acc = jnp.zeros((m, n), F32)
    for h in range(n_rep):
        acc = acc + jnp.where(lane_h == h, o[h * m:(h + 1) * m, :], 0.0)
    return acc


def _natten_kernel(q_ref, k_ref, v_ref, kc_ref, vc_ref, bias_ref, o_ref, *, n_rows):
    r = pl.program_id(1)
    n = q_ref.shape[-1]
    n_rep = n // HEAD_DIM
    kr = min(WIN_R, n_rows)
    r0 = jnp.clip(r - WIN_R // 2, 0, n_rows - kr)
    sl = pl.ds(pl.multiple_of(r0 * GRID_W, GRID_W), kr * GRID_W)
    dn = (((1,), (1,)), ((), ()))
    n_q = q_ref.shape[0]
    gw = NA_HEADS_PER_GROUP * HEAD_DIM
    for g in range(n // gw):
        lanes = slice(g * gw, (g + 1) * gw)
        qs = _stack_heads(q_ref[:, lanes], NA_HEADS_PER_GROUP)
        rows = slice(g * NA_HEADS_PER_GROUP * n_q, (g + 1) * NA_HEADS_PER_GROUP * n_q)
        s_loc = lax.dot_general(qs, k_ref[sl, lanes], dn, preferred_element_type=F32) + bias_ref[0, rows, :]
        s_ctx = lax.dot_general(qs, kc_ref[:, lanes], dn, preferred_element_type=F32)
        m = jnp.maximum(jnp.max(s_loc, axis=-1, keepdims=True), jnp.max(s_ctx, axis=-1, keepdims=True))
        p_loc = jnp.exp(s_loc - m)
        p_ctx = jnp.exp(s_ctx - m)
        inv = 1.0 / (jnp.sum(p_loc, axis=-1, keepdims=True) + jnp.sum(p_ctx, axis=-1, keepdims=True))
        o = jnp.dot((p_loc * inv).astype(BF16), v_ref[sl, lanes], preferred_element_type=F32)
        o = o + jnp.dot((p_ctx * inv).astype(BF16), vc_ref[:, lanes], preferred_element_type=F32)
        o_ref[:, lanes] = _fold_heads(o, n_q, NA_HEADS_PER_GROUP)


def _natten(qn, kn, vv, bias_tab, dims):
    n_lat, t_len, n_batch, d_a, d_b, d_c = dims
    n_all = qn.shape[0]
    l_ctx = (n_all - n_lat) // n_batch
    n_rows = t_len // GRID_W
    kr = min(WIN_R, n_rows)
    ctx0 = n_lat // l_ctx

    def cfg(b, r):
        r0 = jnp.clip(r - WIN_R // 2, 0, n_rows - kr)
        return (r - r0, 0, 0)

    kern = functools.partial(_natten_kernel, n_rows=n_rows)
    return pl.pallas_call(
        kern,
        name="natten",
        out_shape=jax.ShapeDtypeStruct((n_lat, d_c), F32),
        grid=(n_batch, n_rows),
        in_specs=[pl.BlockSpec((GRID_W, d_c), lambda b, r: (b * n_rows + r, 0)),
                  pl.BlockSpec((t_len, d_c), lambda b, r: (b, 0)),
                  pl.BlockSpec((t_len, d_c), lambda b, r: (b, 0)),
                  pl.BlockSpec((l_ctx, d_c), lambda b, r: (ctx0 + b, 0)),
                  pl.BlockSpec((l_ctx, d_c), lambda b, r: (ctx0 + b, 0)),
                  pl.BlockSpec((1,) + bias_tab.shape[1:], cfg)],
        out_specs=pl.BlockSpec((GRID_W, d_c), lambda b, r: (b * n_rows + r, 0)),
        compiler_params=_cparams(("parallel", "arbitrary")),
    )(qn, kn, vv, kn, vv, bias_tab)


def _ctx_attn_kernel(q_ref, k_ref, v_ref, o_ref):
    n = q_ref.shape[-1]
    n_rep = n // HEAD_DIM
    qs = _stack_heads(q_ref[...], n_rep)
    s = lax.dot_general(qs, k_ref[...], (((1,), (1,)), ((), ())), preferred_element_type=F32)
    p = jnp.exp(s - jnp.max(s, axis=-1, keepdims=True))
    p = p * (1.0 / jnp.sum(p, axis=-1, keepdims=True))
    o = jnp.dot(p.astype(BF16), v_ref[...], preferred_element_type=F32)
    o_ref[...] = _fold_heads(o, q_ref.shape[0], n_rep)


def _ctx_attn(qn, kn, vv, dims):
    n_lat, t_len, n_batch, d_a, d_b, d_c = dims
    n_all = qn.shape[0]
    l_ctx = (n_all - n_lat) // n_batch
    tq = 64
    per = l_ctx // tq
    q0 = n_lat // tq
    c0 = n_lat // l_ctx
    return pl.pallas_call(
        _ctx_attn_kernel,
        name="ctx_attn",
        out_shape=jax.ShapeDtypeStruct((n_all - n_lat, d_c), F32),
        grid=(n_batch, per),
        in_specs=[pl.BlockSpec((tq, d_c), lambda b, i: (q0 + b * per + i, 0)),
                  pl.BlockSpec((l_ctx, d_c), lambda b, i: (c0 + b, 0)),
                  pl.BlockSpec((l_ctx, d_c), lambda b, i: (c0 + b, 0))],
        out_specs=pl.BlockSpec((tq, d_c), lambda b, i: (b * per + i, 0)),
        compiler_params=_cparams(("parallel", "arbitrary")),
    )(qn, kn, vv)


def _natten_bias(rpb, n_rows):
    n_h = rpb.shape[0]
    kr = min(WIN_R, n_rows)
    cols = jnp.arange(GRID_W)
    c0 = jnp.clip(cols - WIN_C // 2, 0, GRID_W - WIN_C)
    kc = jnp.arange(GRID_W)
    in_win = (kc[None, :] >= c0[:, None]) & (kc[None, :] < c0[:, None] + WIN_C)
    per = 2 * GRID_W
    rp = rpb.astype(F32)
    u = jnp.concatenate([rp[..., WIN_C - 1:], jnp.zeros(rp.shape[:-1] + (per - 2 * WIN_C + 1,), F32),
                         rp[..., :WIN_C - 1]], axis=-1)
    rel = jnp.tile(u, (1, 1, GRID_W))[..., :GRID_W * (per - 1)]
    rel = rel.reshape(rp.shape[:-1] + (GRID_W, per - 1))[..., :GRID_W]
    toe = jnp.where(in_win[None, None], rel, NEG_BIG)
    tabs = jnp.stack([toe[:, WIN_R - 1 - delta:WIN_R - 1 - delta + kr] for delta in range(kr)])
    return tabs.transpose(0, 1, 3, 2, 4).reshape(kr, n_h * GRID_W, kr * GRID_W)


def _outproj_kernel(x_ref, m_ref, oa_ref, ga_ref, obf_ref, obb_ref, gb_ref, oc_ref, na_ref, nb_ref, w_ref, o_ref,
                    *, d_a, d_b):
    oa = oa_ref[0] + oa_ref[1]
    ya = oa * lax.rsqrt(_group_mean_sq(oa) + EPS) * na_ref[...] * _silu(ga_ref[...])
    ob = obf_ref[...] + obb_ref[...]
    yb = ob * lax.rsqrt(_group_mean_sq(ob) + EPS) * nb_ref[...] * _silu(gb_ref[...])
    acc = _dot(ya, w_ref[0:d_a, :])
    acc = acc + _dot(yb, w_ref[d_a:d_a + d_b, :])
    acc = acc + _dot(oc_ref[...], w_ref[d_a + d_b:, :])
    o_ref[...] = x_ref[...] + m_ref[0, 2:3, :] * acc


def _outproj(xall, mod, oa, pa, obf, obb, pb, oc, na_g, nb_g, w_out, n_rows_out, dims):
    n_lat, t_len, n_batch, d_a, d_b, d_c = dims
    d = xall.shape[1]
    tm = ROW_TILE
    row = lambda i: (i, 0)
    const = lambda i: (0, 0)
    grp = lambda i: (_mod_group(i, tm, n_lat, t_len, n_batch), 0, 0)
    kern = functools.partial(_outproj_kernel, d_a=d_a, d_b=d_b)
    return pl.pallas_call(
        kern,
        name="outproj",
        out_shape=jax.ShapeDtypeStruct((n_rows_out, d), F32),
        grid=(n_rows_out // tm,),
        in_specs=[pl.BlockSpec((tm, d), row),
                  pl.BlockSpec((1, 6, d), grp),
                  pl.BlockSpec((2, tm, d_a), lambda i: (0, i, 0)),
                  pl.BlockSpec((tm, d_a), lambda i: (i, 4)),
                  pl.BlockSpec((tm, d_b), row),
                  pl.BlockSpec((tm, d_b), row),
                  pl.BlockSpec((tm, d_b), lambda i: (i, 3)),
                  pl.BlockSpec((tm, d_c), row),
                  pl.BlockSpec((1, d_a), const),
                  pl.BlockSpec((1, d_b), const),
                  pl.BlockSpec(w_out.shape, const)],
        out_specs=pl.BlockSpec((tm, d), row),
        compiler_params=_cparams(("parallel",)),
    )(xall, mod, oa, pa, obf, obb, pb, oc, na_g, nb_g, w_out)


def _ffn_kernel(x_ref, m_ref, g_ref, w1_ref, w3_ref, w2_ref, o_ref, h_ref, acc_ref):
    j = pl.program_id(1)

    @pl.when(j == 0)
    def _():
        h_ref[...] = _modulated_norm(x_ref[...], g_ref[...], m_ref[0, 3:4, :], m_ref[0, 4:5, :]).astype(BF16)
        acc_ref[...] = jnp.zeros_like(acc_ref)

    h = h_ref[...]
    a = jnp.dot(h, w1_ref[...], preferred_element_type=F32)
    b = jnp.dot(h, w3_ref[...], preferred_element_type=F32)
    acc_ref[...] += _dot(_silu(a) * b, w2_ref[...])

    @pl.when(j == pl.num_programs(1) - 1)
    def _():
        o_ref[...] = x_ref[...] + m_ref[0, 5:6, :] * acc_ref[...]


def _ffn(xall, mod, norm_g, w1, w3, w2, dims):
    n_lat, t_len, n_batch, d_a, d_b, d_c = dims
    n_rows, d = xall.shape
    d_ff = w1.shape[1]
    tm, tf = FFN_ROWS, FF_TILE
    assert t_len % tm == 0
    grp = lambda i, j: (_mod_group(i, tm, n_lat, t_len, n_batch), 0, 0)
    return pl.pallas_call(
        _ffn_kernel,
        name="ffn",
        out_shape=jax.ShapeDtypeStruct((n_rows, d), F32),
        grid=(pl.cdiv(n_rows, tm), d_ff // tf),
        in_specs=[pl.BlockSpec((tm, d), lambda i, j: (i, 0)),
                  pl.BlockSpec((1, 6, d), grp),
                  pl.BlockSpec((1, d), lambda i, j: (0, 0)),
                  pl.BlockSpec((d, tf), lambda i, j: (0, j)),
                  pl.BlockSpec((d, tf), lambda i, j: (0, j)),
                  pl.BlockSpec((tf, d), lambda i, j: (j, 0))],
        out_specs=pl.BlockSpec((tm, d), lambda i, j: (i, 0)),
        scratch_shapes=[pltpu.VMEM((tm, d), BF16), pltpu.VMEM((tm, d), F32)],
        compiler_params=_cparams(("parallel", "arbitrary")),
    )(xall, mod, norm_g, w1, w3, w2)


def _route_kernel(x_ref, m_ref, g_ref, wr_ref, br_ref, h_ref, e_ref, gt_ref):
    h = _modulated_norm(x_ref[...], g_ref[...], m_ref[0, 3:4, :], m_ref[0, 4:5, :])
    h_ref[...] = h
    lane = _iota((h.shape[0], LANES), 1)
    logits = jnp.where(lane < N_EXPERTS, _dot_hi(h, wr_ref[...]) + br_ref[...], -jnp.inf)
    m1 = jnp.max(logits, axis=-1, keepdims=True)
    lane_f = lane.astype(F32)
    i1 = jnp.min(jnp.where(logits == m1, lane_f, float(LANES)), axis=-1, keepdims=True).astype(jnp.int32)
    rest = jnp.where(lane == i1, -jnp.inf, logits)
    m2 = jnp.max(rest, axis=-1, keepdims=True)
    i2 = jnp.min(jnp.where(rest == m2, lane_f, float(LANES)), axis=-1, keepdims=True).astype(jnp.int32)
    e2 = jnp.exp(m2 - m1)
    g1 = 1.0 / (1.0 + e2)
    g2 = e2 / (1.0 + e2)
    e_ref[...] = jnp.where(lane == 0, i1, jnp.where(lane == 1, i2, 0))
    gt_ref[...] = jnp.where(lane == 0, g1, jnp.where(lane == 1, g2, 0.0))


def _route(x, mod, norm_g, wr_pad, br_pad, dims):
    n_lat, t_len, n_batch, d_a, d_b, d_c = dims
    n, d = x.shape
    tm = ROW_TILE
    row = lambda i: (i, 0)
    const = lambda i: (0, 0)
    grp = lambda i: (_mod_group(i, tm, n_lat, t_len, n_batch), 0, 0)
    return pl.pallas_call(
        _route_kernel,
        name="moe_route",
        out_shape=(jax.ShapeDtypeStruct((n, d), F32), jax.ShapeDtypeStruct((n, LANES), jnp.int32),
                   jax.ShapeDtypeStruct((n, LANES), F32)),
        grid=(n // tm,),
        in_specs=[pl.BlockSpec((tm, d), row), pl.BlockSpec((1, 6, d), grp), pl.BlockSpec((1, d), const),
                  pl.BlockSpec((d, LANES), const), pl.BlockSpec((1, LANES), const)],
        out_specs=(pl.BlockSpec((tm, d), row), pl.BlockSpec((tm, LANES), row), pl.BlockSpec((tm, LANES), row)),
        compiler_params=_cparams(("parallel",)),
    )(x, mod, norm_g, wr_pad, br_pad)


def _expert_kernel(be_ref, nv_ref, idx0_ref, idxn_ref, dstp_ref, dstc_ref, h_hbm, w1_ref, w3_ref, w2_ref, y_hbm,
                   xbuf_ref, hb_ref, acc_ref, sem_in, sem_out, *, rows_per_step, n_pairs):
    i = pl.program_id(0)
    j = pl.program_id(1)
    n_blk = pl.num_programs(0)
    n_ff = pl.num_programs(1)
    bm = hb_ref.shape[0]
    rps = rows_per_step
    n_issue = xbuf_ref.shape[1]
    live = i < nv_ref[0]
    cur = i % 2
    nxt = 1 - cur

    def in_copy(tok, slot, r):
        return pltpu.make_async_copy(h_hbm.at[pl.ds(tok, 1)], xbuf_ref.at[slot, pl.ds(r, 1)], sem_in)

    def out_copy(slot, r, dst):
        return pltpu.make_async_copy(acc_ref.at[slot, pl.ds(r, 1)], y_hbm.at[pl.ds(dst, 1)], sem_out)

    def wait_in(slot):
        for c in range(n_issue // rps):
            pltpu.make_async_copy(h_hbm.at[pl.ds(0, rps)], xbuf_ref.at[slot, pl.ds(c * rps, rps)], sem_in).wait()

    def wait_out(slot):
        for c in range(n_issue // rps):
            pltpu.make_async_copy(acc_ref.at[slot, pl.ds(c * rps, rps)], y_hbm.at[pl.ds(0, rps)], sem_out).wait()

    @pl.when((i == 0) & (j == 0))
    def _():
        acc_ref[...] = jnp.zeros_like(acc_ref)

        def start(g, carry):
            for u in range(GATHER_PARTS):
                r = g * GATHER_PARTS + u
                in_copy(idx0_ref[0, 0, r], 0, r).start()
            return carry

        lax.fori_loop(0, n_issue // GATHER_PARTS, start, 0)

    @pl.when(j == 0)
    def _():
        wait_in(cur)

    @pl.when(live & (j == 0))
    def _():
        hb_ref[...] = xbuf_ref[cur, 0:bm, :].astype(BF16)
        acc_ref[cur, 0:bm, :] = jnp.zeros((bm, acc_ref.shape[2]), F32)

    def issue(part):
        per = rps // GATHER_PARTS
        for t in range(part * per, (part + 1) * per):
            r = j * rps + t
            in_copy(idxn_ref[0, 0, r], nxt, r).start(priority=0)
            dst = jnp.where(i == 0, n_pairs + r, dstp_ref[0, 0, r])
            out_copy(nxt, r, dst).start(priority=1)

    @pl.when(live)
    def _():
        h = hb_ref[...]
        a = _dot(h, w1_ref[0])
        issue(0)
        b = _dot(h, w3_ref[0])
        issue(1)
        g = (_silu(a) * b).astype(BF16)
        issue(2)
        acc_ref[cur, 0:bm, :] += jnp.dot(g, w2_ref[0].astype(BF16), preferred_element_type=F32)
        issue(3)

    @pl.when(jnp.logical_not(live))
    def _():
        for part in range(GATHER_PARTS):
            issue(part)

    last = j == n_ff - 1

    @pl.when(last)
    def _():
        wait_out(nxt)

    @pl.when((i == n_blk - 1) & last)
    def _():
        wait_in(nxt)

        def start(g, carry):
            for u in range(GATHER_PARTS):
                r = g * GATHER_PARTS + u
                out_copy(cur, r, dstc_ref[0, 0, r]).start()
            return carry

        lax.fori_loop(0, n_issue // GATHER_PARTS, start, 0)
        wait_out(cur)


def _experts(h, slot_tok, slot_pair, block_e, n_live, w1, w3, w2):
    n, d = h.shape
    d_ff = w1.shape[2]
    bm, tf = MOE_ROWS, FF_TILE
    n_blk = slot_tok.shape[0] // bm
    n_ff = d_ff // tf
    n_pairs = 2 * n
    rps = -(-bm // (n_ff * SUBLANES)) * SUBLANES
    assert rps % GATHER_PARTS == 0
    n_issue = rps * n_ff
    extra = n_issue - bm
    idx = jnp.pad(slot_tok.reshape(n_blk, 1, bm), ((0, 0), (0, 0), (0, extra)))
    dump = n_pairs + jnp.arange(n_issue, dtype=jnp.int32)
    dst = jnp.concatenate([slot_pair.reshape(n_blk, 1, bm), jnp.broadcast_to(dump[bm:], (n_blk, 1, extra))], axis=2)
    smem = lambda f: pl.BlockSpec((1, 1, n_issue), f, memory_space=pltpu.SMEM)
    grid_spec = pltpu.PrefetchScalarGridSpec(
        num_scalar_prefetch=2,
        grid=(n_blk, n_ff),
        in_specs=[smem(lambda i, j, be, nv: (0, 0, 0)),
                  smem(lambda i, j, be, nv: (jnp.minimum(i + 1, n_blk - 1), 0, 0)),
                  smem(lambda i, j, be, nv: (jnp.maximum(i - 1, 0), 0, 0)),
                  smem(lambda i, j, be, nv: (i, 0, 0)),
                  pl.BlockSpec(memory_space=pl.ANY),
                  pl.BlockSpec((1, d, tf), lambda i, j, be, nv: (be[i], 0, j)),
                  pl.BlockSpec((1, d, tf), lambda i, j, be, nv: (be[i], 0, j)),
                  pl.BlockSpec((1, tf, d), lambda i, j, be, nv: (be[i], j, 0))],
        out_specs=pl.BlockSpec(memory_space=pl.ANY),
        scratch_shapes=[pltpu.VMEM((2, n_issue, d), F32), pltpu.VMEM((bm, d), BF16),
                        pltpu.VMEM((2, n_issue, d), F32),
                        pltpu.SemaphoreType.DMA(()), pltpu.SemaphoreType.DMA(())])
    return pl.pallas_call(
        functools.partial(_expert_kernel, rows_per_step=rps, n_pairs=n_pairs),
        name="moe_experts",
        out_shape=jax.ShapeDtypeStruct((n_pairs + n_issue, d), F32),
        grid_spec=grid_spec,
        compiler_params=_cparams(("arbitrary", "arbitrary")),
    )(block_e, n_live, idx, idx, dst, dst, h, w1, w3, w2)


def _combine_kernel(x_ref, m_ref, gt_ref, y_ref, o_ref):
    d = x_ref.shape[1]
    gt = gt_ref[...]
    y = gt[:, 0:1] * y_ref[:, 0:d] + gt[:, 1:2] * y_ref[:, d:2 * d]
    o_ref[...] = x_ref[...] + m_ref[0, 5:6, :] * y


def _combine(x, mod, gates, y, dims):
    n_lat, t_len, n_batch, d_a, d_b, d_c = dims
    n, d = x.shape
    tm = ROW_TILE
    assert y.shape[0] % 2 == 0
    y2 = y.reshape(y.shape[0] // 2, 2 * d)
    grp = lambda i: (_mod_group(i, tm, n_lat, t_len, n_batch), 0, 0)
    return pl.pallas_call(
        _combine_kernel,
        name="moe_combine",
        out_shape=jax.ShapeDtypeStruct((n, d), F32),
        grid=(n // tm,),
        in_specs=[pl.BlockSpec((tm, d), lambda i: (i, 0)),
                  pl.BlockSpec((1, 6, d), grp),
                  pl.BlockSpec((tm, LANES), lambda i: (i, 0)),
                  pl.BlockSpec((tm, 2 * d), lambda i: (i, 0))],
        out_specs=pl.BlockSpec((tm, d), lambda i: (i, 0)),
        compiler_params=_cparams(("parallel",)),
    )(x, mod, gates, y2)


def _moe(x, mod, norm_g, w_router, b_router, w1, w3, w2, dims):
    n, d = x.shape
    wr_pad = jnp.zeros((d, LANES), F32).at[:, :N_EXPERTS].set(w_router.astype(F32))
    br_pad = jnp.zeros((1, LANES), F32).at[0, :N_EXPERTS].set(b_router.astype(F32))
    h, e_tile, g_tile = _route(x, mod, norm_g, wr_pad, br_pad, dims)
    bm = MOE_ROWS
    e_flat = e_tile[:, :2].reshape(-1)
    onehot = (e_flat[:, None] == jnp.arange(N_EXPERTS, dtype=jnp.int32)[None, :]).astype(jnp.int32)
    csum = jnp.cumsum(onehot, axis=0)
    counts = csum[-1]
    rank = jnp.sum(csum * onehot, axis=1) - 1
    padded = (counts + bm - 1) // bm * bm
    pad_end = jnp.cumsum(padded)
    pad_start = pad_end - padded
    slot = pad_start[e_flat] + rank
    n_blocks = (2 * n) // bm + N_EXPERTS
    pair_flat = jnp.arange(2 * n, dtype=jnp.int32)
    slot_ids = jnp.arange(n_blocks * bm, dtype=jnp.int32)
    slot_pair = (2 * n + slot_ids % bm).at[slot].set(pair_flat)
    slot_tok = jnp.where(slot_pair < 2 * n, slot_pair // 2, slot_ids % n)
    blk_start = jnp.arange(n_blocks, dtype=jnp.int32) * bm
    block_e = jnp.minimum(jnp.sum((pad_end[None, :] <= blk_start[:, None]).astype(jnp.int32), axis=1),
                          N_EXPERTS - 1).astype(jnp.int32)
    n_live = (pad_end[-1] // bm).astype(jnp.int32).reshape(1)
    y = _experts(h, slot_tok, slot_pair, block_e, n_live, w1, w3, w2)
    return _combine(x, mod, g_tile, y, dims)


def _rope_tables(t_len, n_heads):
    t = jnp.arange(t_len)
    row = (t // GRID_W).astype(F32)
    col = (t % GRID_W).astype(F32)
    n_freq = HEAD_DIM // 4
    inv = ROPE_BASE ** (-jnp.arange(n_freq, dtype=F32) / n_freq)
    ang = jnp.concatenate([row[:, None] * inv, col[:, None] * inv], axis=-1)
    cos, sin = jnp.cos(ang), jnp.sin(ang)
    cos_h = jnp.concatenate([cos, cos], axis=-1)
    sin_h = jnp.concatenate([-sin, sin], axis=-1)
    return jnp.tile(cos_h, (1, n_heads)), jnp.tile(sin_h, (1, n_heads))


def kernel(x, c, ctx, c_ctx, ada_w, ada_b, norm1_g, norm2_g, w_in, w_out, hgrn_lb_raw, hgrn_onorm_g,
           gdn_conv_w, gdn_a_log, gdn_dt_bias, gdn_onorm_g, na_qnorm_g, na_knorm_g, na_rpb, ffn_w1, ffn_w3,
           ffn_w2, moe_router_w, moe_router_b, moe_w1, moe_w3, moe_w2):
    n_batch, t_len, d = x.shape
    l_ctx = ctx.shape[1]
    depth = w_in.shape[0]
    d_a = hgrn_lb_raw.shape[-1]
    d_b = gdn_conv_w.shape[-1] // 3
    n_hb = gdn_a_log.shape[-1]
    n_hc = na_rpb.shape[1]
    d_c = n_hc * HEAD_DIM
    n_lat = n_batch * t_len
    n_all = n_lat + n_batch * l_ctx
    dims = (n_lat, t_len, n_batch, d_a, d_b, d_c)
    assert d_a % HEAD_DIM == 0 and d_b == n_hb * HEAD_DIM
    assert t_len % ROW_TILE == 0 and (n_batch * l_ctx) % ROW_TILE == 0 and l_ctx % TOK_BLK == 0
    assert t_len % GRID_W == 0 and 4 * n_hb <= LANES

    cvec = jnp.zeros((8, d), F32).at[:n_batch].set(c.astype(F32)).at[n_batch].set(c_ctx.astype(F32))
    mod_all = _adaln(cvec, ada_w, ada_b)

    lb_soft = jax.nn.softmax(hgrn_lb_raw.astype(F32), axis=1)
    lower_bound = jnp.cumsum(lb_soft, axis=1) - lb_soft[:, :1]
    cos_t, sin_t = _rope_tables(t_len, n_hb)
    n_gate = 4 * n_hb
    sizes_a, sizes_b = 5 * d_a, 4 * d_b

    xall = jnp.concatenate([x.reshape(n_lat, d), ctx.reshape(n_batch * l_ctx, d)], axis=0).astype(F32)
    for l in range(depth):
        last = l == depth - 1
        mod = mod_all[l, :n_batch + 1].reshape(n_batch + 1, 6, d)
        w = w_in[l]
        gate_cols = jnp.pad(w[:, sizes_a + sizes_b:sizes_a + sizes_b + n_gate], ((0, 0), (0, LANES - n_gate)))
        w_pad = jnp.concatenate([w[:, :sizes_a + sizes_b], gate_cols, w[:, sizes_a + sizes_b + n_gate:]],
                                axis=1).astype(BF16)
        qn_g = jnp.tile(na_qnorm_g[l].astype(F32), n_hc)[None]
        kn_g = jnp.tile(na_knorm_g[l].astype(F32), n_hc)[None]
        pa, pb, pg, qn, kn, vv = _inproj(xall, mod, norm1_g[l][None].astype(F32), w_pad, qn_g, kn_g, dims)

        oa = _hgrn_scan(pa, lower_bound[:, l][:, None, :], dims)
        conv_w = jnp.zeros((8, 3 * d_b), F32).at[:CONV_K].set(gdn_conv_w[l].astype(F32))
        alog_x = jnp.repeat(gdn_a_log[l].astype(F32), HEAD_DIM, axis=-1)[:, None, :]
        dtb_x = jnp.repeat(gdn_dt_bias[l].astype(F32), HEAD_DIM, axis=-1)[:, None, :]
        obf, obb = _gdn_scan(_gdn_prep(pb, pg, conv_w, alog_x, dtb_x, cos_t, sin_t, dims), dims)
        bias_tab = _natten_bias(na_rpb[l], t_len // GRID_W)
        oc = _natten(qn, kn, vv, bias_tab, dims)
        n_out = n_lat if last else n_all
        if not last:
            oc = jnp.concatenate([oc, _ctx_attn(qn, kn, vv, dims)], axis=0)
        na_g = jnp.tile(hgrn_onorm_g[l].astype(F32), d_a // HEAD_DIM)[None]
        nb_g = jnp.tile(gdn_onorm_g[l].astype(F32), n_hb)[None]
        xall_mid = _outproj(xall, mod, oa, pa, obf, obb, pb, oc, na_g, nb_g, w_out[l].astype(BF16), n_out, dims)
        i = l // 2
        if l % 2 == 0:
            xall = _ffn(xall_mid, mod, norm2_g[l][None].astype(F32), ffn_w1[i].astype(BF16),
                        ffn_w3[i].astype(BF16), ffn_w2[i].astype(BF16), dims)
        else:
            xall = _moe(xall_mid, mod, norm2_g[l][None].astype(F32), moe_router_w[i], moe_router_b[i],
                        moe_w1[i], moe_w3[i], moe_w2[i], dims)
    return xall[:n_lat].reshape(n_batch, t_len, d).astype(x.dtype)
```

```python
import functools
import math

import jax
import jax.numpy as jnp
import numpy as np
from jax import lax
from jax.experimental import pallas as pl
from jax.experimental.pallas import tpu as pltpu

F32 = jnp.float32
BF16 = jnp.bfloat16
HI = lax.Precision.HIGHEST

EPS = 1e-6
HEAD_DIM = 64
CHUNK = 64
TOK_BLK = 256
GRID_W = 64
WIN_R = 8
WIN_C = 16
CONV_K = 5
ROPE_BASE = 10000.0
N_EXPERTS = 8
LANES = 128
SUBLANES = 8
ROW_TILE = 512
FFN_ROWS = 1024
FF_TILE = 512
MOE_ROWS = 1024
GATHER_PARTS = 4
NA_HEADS_PER_GROUP = 4
VMEM_LIMIT = 56 * 1024 * 1024
MXU_DEPTH = 256
NEG_BIG = -1e30


def _cparams(sem):
    return pltpu.CompilerParams(dimension_semantics=sem, vmem_limit_bytes=VMEM_LIMIT)


def _dot(a, b):
    return jnp.dot(a.astype(BF16), b.astype(BF16), preferred_element_type=F32)


def _dot_nt(a, b):
    return lax.dot_general(a.astype(BF16), b.astype(BF16), (((1,), (1,)), ((), ())),
                           preferred_element_type=F32)


def _dot_tn(a, b):
    return lax.dot_general(a.astype(BF16), b.astype(BF16), (((0,), (0,)), ((), ())),
                           preferred_element_type=F32)


def _dot_hi(a, b):
    return jnp.dot(a, b, precision=HI, preferred_element_type=F32)


def _iota(shape, dim):
    return lax.broadcasted_iota(jnp.int32, shape, dim)


def _head_mask(n):
    return (_iota((n, n), 0) // HEAD_DIM) == (_iota((n, n), 1) // HEAD_DIM)


def _head_sums(x, scale):
    n = x.shape[-1]
    w = min(n, MXU_DEPTH)
    gm = jnp.where(_head_mask(w), scale, 0.0).astype(BF16)
    hi = x.astype(BF16)
    lo = (x - hi.astype(F32)).astype(BF16)
    dot = lambda a: jnp.dot(a, gm, preferred_element_type=F32)
    parts = [dot(lo[:, s:s + w]) + dot(hi[:, s:s + w]) for s in range(0, n, w)]
    return parts[0] if len(parts) == 1 else jnp.concatenate(parts, axis=1)


def _group_mean_sq(x):
    return _head_sums(x * x, 1.0 / HEAD_DIM)


def _silu(x):
    return x * jax.nn.sigmoid(x)


def _modulated_norm(x, g, shift, scale):
    ms = jnp.mean(x * x, axis=-1, keepdims=True)
    y = x * lax.rsqrt(ms + EPS) * g
    return y * (1.0 + scale) + shift


def _mod_group(i, tile, n_lat, t_len, n_batch):
    return jnp.where(i * tile < n_lat, (i * tile) // t_len, n_batch)


def _adaln_kernel(c_ref, w_ref, b_ref, o_ref):
    s = _silu(c_ref[...])
    o_ref[0] = _dot(s, w_ref[0]) + b_ref[0]


def _adaln(cvec, ada_w, ada_b):
    depth, d, d6 = ada_w.shape
    tn = 1024
    return pl.pallas_call(
        _adaln_kernel,
        name="adaln",
        out_shape=jax.ShapeDtypeStruct((depth, 8, d6), F32),
        grid=(depth, d6 // tn),
        in_specs=[pl.BlockSpec((8, d), lambda l, j: (0, 0)),
                  pl.BlockSpec((1, d, tn), lambda l, j: (l, 0, j)),
                  pl.BlockSpec((1, 1, tn), lambda l, j: (l, 0, j))],
        out_specs=pl.BlockSpec((1, 8, tn), lambda l, j: (l, 0, j)),
        compiler_params=_cparams(("parallel", "parallel")),
    )(cvec, ada_w, ada_b.reshape(depth, 1, d6))


def _inproj_kernel(x_ref, m_ref, g_ref, w_ref, qg_ref, kg_ref,
                   pa_ref, pb_ref, pg_ref, qn_ref, kn_ref, vv_ref, *, d_a5, d_b4, d_c):
    h = _modulated_norm(x_ref[...], g_ref[...], m_ref[0, 0:1, :], m_ref[0, 1:2, :]).astype(BF16)
    o0 = 0
    pa_ref[...] = jnp.dot(h, w_ref[:, o0:o0 + d_a5], preferred_element_type=F32)
    o0 += d_a5
    pb_ref[...] = jnp.dot(h, w_ref[:, o0:o0 + d_b4], preferred_element_type=F32)
    o0 += d_b4
    pg_ref[...] = jnp.dot(h, w_ref[:, o0:o0 + LANES], preferred_element_type=F32)
    o0 += LANES
    q = jnp.dot(h, w_ref[:, o0:o0 + d_c], preferred_element_type=F32)
    q = q * lax.rsqrt(_group_mean_sq(q) + EPS) * qg_ref[...]
    qn_ref[...] = (q * HEAD_DIM ** -0.5).astype(BF16)
    o0 += d_c
    k = jnp.dot(h, w_ref[:, o0:o0 + d_c], preferred_element_type=F32)
    k = k * lax.rsqrt(_group_mean_sq(k) + EPS) * kg_ref[...]
    kn_ref[...] = k.astype(BF16)
    o0 += d_c
    vv_ref[...] = jnp.dot(h, w_ref[:, o0:o0 + d_c], preferred_element_type=F32).astype(BF16)


def _inproj(xall, mod, norm_g, w_pad, qn_g, kn_g, dims):
    n_all, d = xall.shape
    n_lat, t_len, n_batch, d_a, d_b, d_c = dims
    tm = ROW_TILE
    row = lambda i: (i, 0)
    const = lambda i: (0, 0)
    grp = lambda i: (_mod_group(i, tm, n_lat, t_len, n_batch), 0, 0)
    kern = functools.partial(_inproj_kernel, d_a5=5 * d_a, d_b4=4 * d_b, d_c=d_c)
    return pl.pallas_call(
        kern,
        name="inproj",
        out_shape=(jax.ShapeDtypeStruct((n_all, 5 * d_a), F32),
                   jax.ShapeDtypeStruct((n_all, 4 * d_b), F32),
                   jax.ShapeDtypeStruct((n_all, LANES), F32),
                   jax.ShapeDtypeStruct((n_all, d_c), BF16),
                   jax.ShapeDtypeStruct((n_all, d_c), BF16),
                   jax.ShapeDtypeStruct((n_all, d_c), BF16)),
        grid=(n_all // tm,),
        in_specs=[pl.BlockSpec((tm, d), row),
                  pl.BlockSpec((1, 6, d), grp),
                  pl.BlockSpec((1, d), const),
                  pl.BlockSpec(w_pad.shape, const),
                  pl.BlockSpec((1, d_c), const),
                  pl.BlockSpec((1, d_c), const)],
        out_specs=(pl.BlockSpec((tm, 5 * d_a), row), pl.BlockSpec((tm, 4 * d_b), row),
                   pl.BlockSpec((tm, LANES), row), pl.BlockSpec((tm, d_c), row),
                   pl.BlockSpec((tm, d_c), row), pl.BlockSpec((tm, d_c), row)),
        compiler_params=_cparams(("parallel",)),
    )(xall, mod, norm_g, w_pad, qn_g, kn_g)


def _scan_block_index(b, d, j, n_lat_blk, n_ctx_blk, n_batch):
    jc = jnp.where(d == 0, j, n_ctx_blk - 1 - j)
    jl = jnp.where(d == 0, j - n_ctx_blk, n_lat_blk - 1 - (j - n_ctx_blk))
    return jnp.where(j < n_ctx_blk, n_batch * n_lat_blk + b * n_ctx_blk + jc, b * n_lat_blk + jl)


def _block_diag(x, mask_bd, n_rep):
    xb = x.astype(BF16)
    return jnp.where(mask_bd, jnp.concatenate([xb] * n_rep, axis=0), jnp.zeros((), BF16))


def _scan_order(idx, d):
    return idx + d * (CHUNK - 1 - 2 * idx)


def _split3(x):
    hi = x.astype(BF16)
    r = x - hi.astype(F32)
    mid = r.astype(BF16)
    lo = (r - mid.astype(F32)).astype(BF16)
    return hi, mid, lo


def _sel_dot(sel, x):
    hi, mid, lo = _split3(x)
    dot = lambda a: jnp.dot(sel, a, preferred_element_type=F32)
    return (dot(lo) + dot(mid)) + dot(hi)


def _dot_sel(x, sel):
    hi, mid, lo = _split3(x)
    dot = lambda a: jnp.dot(a, sel, preferred_element_type=F32)
    return (dot(lo) + dot(mid)) + dot(hi)


HGRN_LEVELS = tuple(CHUNK >> (i + 1) for i in range(int(math.log2(CHUNK))))


def _hgrn_kernel(q_ref, z_ref, v_ref, lb_ref, o_ref, st_ref, *, n_chunk):
    d = pl.program_id(1)
    j = pl.program_id(2)
    n = q_ref.shape[-1]
    n_rep = n // HEAD_DIM

    @pl.when(j == 0)
    def _():
        st_ref[...] = jnp.zeros_like(st_ref)

    lb = lb_ref[0]
    mask_bd = _head_mask(n)
    gones = mask_bd.astype(BF16)
    bd = lambda a: _block_diag(a, mask_bd, n_rep)
    s_t = _scan_order(_iota((CHUNK, CHUNK), 0), d)
    s_u = _scan_order(_iota((CHUNK, CHUNK), 1), d)
    sets = [s_u <= s_t, s_u > s_t]
    for m in HGRN_LEVELS:
        ref = (s_t // (2 * m)) * (2 * m) + m - 1
        sets.append((s_u > jnp.minimum(s_t, ref)) & (s_u <= jnp.maximum(s_t, ref)))
    sel = jnp.concatenate([jnp.where(a, 1.0, 0.0).astype(BF16) for a in sets], axis=0)
    s_row = _scan_order(_iota((CHUNK, n), 0), d)
    s_col = _scan_order(_iota((CHUNK, n), 1) % HEAD_DIM, d)
    eye = s_row == s_col
    second_half = [(s_row // m) % 2 == 1 for m in HGRN_LEVELS]
    same_block = [(s_row // (2 * m)) == (s_col // (2 * m)) for m in HGRN_LEVELS]

    work = []
    for c in range(n_chunk):
        ci = c + d * (n_chunk - 1 - 2 * c)
        sl = pl.ds(pl.multiple_of(ci * CHUNK, CHUNK), CHUNK)
        q = q_ref[sl, :] * HEAD_DIM ** -0.5
        z = z_ref[sl, :]
        v = v_ref[sl, :]
        log_f = (jnp.minimum(z, 0.0) - jnp.log1p(jnp.exp(-jnp.abs(z)))) + jnp.log1p(lb * jnp.exp(-z))
        k = (1.0 - lb) * jax.nn.sigmoid(-z)
        work.append((sl, q, k, v, _sel_dot(sel, log_f)))
    atts = [jnp.where(eye, jnp.dot((q * k).astype(BF16), gones, preferred_element_type=F32), 0.0)
            for _, q, k, _, _ in work]
    for lvl, m in enumerate(HGRN_LEVELS):
        second = second_half[lvl]
        for i, (_, q, k, _, sums) in enumerate(work):
            e = jnp.exp(sums[(2 + lvl) * CHUNK:(3 + lvl) * CHUNK])
            a_m = _dot_nt(jnp.where(second, q * e, 0.0), bd(jnp.where(second, 0.0, k * e)))
            if 2 * m < CHUNK:
                a_m = jnp.where(same_block[lvl], a_m, 0.0)
            atts[i] = atts[i] + a_m
    pre = []
    for (sl, q, k, v, sums), att in zip(work, atts):
        b_incl = sums[0:CHUNK]
        b_after = sums[CHUNK:2 * CHUNK]
        e_end = jnp.exp(b_incl[0:1, :] + b_after[0:1, :])
        upd = jnp.where(mask_bd, _dot_tn(v, k * jnp.exp(b_after)), 0.0)
        pre.append((sl, _dot(att, bd(v)), (q * jnp.exp(b_incl)).astype(BF16), e_end, upd))

    st = st_ref[...]
    for sl, o_intra, q_hat, e_end, upd in pre:
        o_ref[0, sl, :] = o_intra + _dot_nt(q_hat, st)
        st = st * e_end + upd
    st_ref[...] = st


def _hgrn_scan(pa, lb2, dims):
    n_all = pa.shape[0]
    n_lat, t_len, n_batch, d_a, d_b, d_c = dims
    n_lat_blk = t_len // TOK_BLK
    n_ctx_blk = (n_all - n_lat) // n_batch // TOK_BLK
    blk = functools.partial(_scan_block_index, n_lat_blk=n_lat_blk, n_ctx_blk=n_ctx_blk, n_batch=n_batch)
    kern = functools.partial(_hgrn_kernel, n_chunk=TOK_BLK // CHUNK)
    return pl.pallas_call(
        kern,
        name="hgrn_scan",
        out_shape=jax.ShapeDtypeStruct((2, n_all, d_a), F32),
        grid=(n_batch, 2, n_lat_blk + n_ctx_blk),
        in_specs=[pl.BlockSpec((TOK_BLK, d_a), lambda b, d, j: (blk(b, d, j), 0)),
                  pl.BlockSpec((TOK_BLK, d_a), lambda b, d, j: (blk(b, d, j), 1 + d)),
                  pl.BlockSpec((TOK_BLK, d_a), lambda b, d, j: (blk(b, d, j), 3)),
                  pl.BlockSpec((1, 1, d_a), lambda b, d, j: (d, 0, 0))],
        out_specs=pl.BlockSpec((1, TOK_BLK, d_a), lambda b, d, j: (d, blk(b, d, j), 0)),
        scratch_shapes=[pltpu.VMEM((d_a, d_a), F32)],
        compiler_params=_cparams(("parallel", "parallel", "arbitrary")),
    )(pa, pa, pa, lb2)


def _gdn_prep_kernel(x_ref, prev_ref, next_ref, pg_ref, cw_ref, alog_ref, dtb_ref, cos_ref, sin_ref,
                     w_ref, u0_ref, qh_ref, qg_ref, kh_ref, ee_ref,
                     *, n_lat_blk, n_ctx_blk, n_lat_blks_total, d_b):
    i = pl.program_id(0)
    is_lat = i < n_lat_blks_total
    pos = jnp.where(is_lat, i % n_lat_blk, (i - n_lat_blks_total) % n_ctx_blk)
    n_seq_blk = jnp.where(is_lat, n_lat_blk, n_ctx_blk)
    first = pos == 0
    last = pos == n_seq_blk - 1
    x = x_ref[...]
    halo = prev_ref.shape[0]
    prev = jnp.where(first, 0.0, prev_ref[...])
    nxt = jnp.where(last, 0.0, next_ref[...])
    rows = _iota(x.shape, 0)
    tb = x.shape[0]
    half = CONV_K // 2
    acc = x * cw_ref[half:half + 1, :]
    for s in range(1, half + 1):
        xs = pltpu.roll(x, s, 0)
        for r in range(s):
            xs = jnp.where(rows == r, prev[halo - s + r:halo - s + r + 1, :], xs)
        acc = acc + xs * cw_ref[half - s:half - s + 1, :]
        xs = pltpu.roll(x, tb - s, 0)
        for r in range(s):
            xs = jnp.where(rows == tb - s + r, nxt[r:r + 1, :], xs)
        acc = acc + xs * cw_ref[half + s:half + s + 1, :]
    y = _silu(acc)
    def l2n(a):
        return a * lax.rsqrt(_head_sums(a * a, 1.0) + EPS)

    lane = _iota((tb, d_b), 1) % HEAD_DIM
    lo = lane < HEAD_DIM // 2

    def rope(a):
        partner = jnp.where(lo, pltpu.roll(a, d_b - HEAD_DIM // 2, 1), pltpu.roll(a, HEAD_DIM // 2, 1))
        return jnp.where(is_lat, a * cos_ref[...] + partner * sin_ref[...], a)

    q_all = rope(l2n(y[:, 0:d_b])) * HEAD_DIM ** -0.5
    k_all = rope(l2n(y[:, d_b:2 * d_b]))
    v_all = y[:, 2 * d_b:3 * d_b]
    n_h = d_b // HEAD_DIM
    g = pg_ref[...]
    e_r = _iota((LANES, d_b), 0)
    e_c = _iota((LANES, d_b), 1) // HEAD_DIM
    mask_bd = _head_mask(d_b)
    bd = lambda a: _block_diag(a, mask_bd, n_h)
    ones_cc = jnp.ones((CHUNK, CHUNK), BF16)
    ee_ref[...] = jnp.zeros_like(ee_ref)
    work = []
    t_invs = []
    for dd in range(2):
        a_x = _dot_sel(g, jnp.where(e_r == dd * n_h + e_c, 1.0, 0.0).astype(BF16))
        b_x = _dot_sel(g, jnp.where(e_r == (2 + dd) * n_h + e_c, 1.0, 0.0).astype(BF16))
        t = a_x + dtb_ref[dd]
        softplus = jnp.maximum(t, 0.0) + jnp.log1p(jnp.exp(-jnp.abs(t)))
        la_all = -jnp.exp(alog_ref[dd]) * softplus
        be_all = jax.nn.sigmoid(b_x)
        s_t = _scan_order(_iota((CHUNK, CHUNK), 0), dd)
        s_u = _scan_order(_iota((CHUNK, CHUNK), 1), dd)
        m_incl = jnp.where(s_u <= s_t, 1.0, 0.0).astype(BF16)
        s_row = _scan_order(_iota((CHUNK, d_b), 0), dd)
        s_col = _scan_order(_iota((CHUNK, d_b), 1) % HEAD_DIM, dd)
        incl = s_col <= s_row
        strict = s_col < s_row
        m_before = jnp.where(s_row <= s_col, 1.0, 0.0)
        eye_f = jnp.where(s_col == s_row, 1.0, 0.0)
        pair = (s_row // 2) == (s_col // 2)
        levels = []
        m = 4
        while m <= CHUNK:
            levels.append(((s_row // m) == (s_col // m)) & ((s_row // (m // 2)) != (s_col // (m // 2))))
            m *= 2
        end_row = CHUNK - 1 if dd == 0 else 0
        for c in range(tb // CHUNK):
            r = slice(c * CHUNK, (c + 1) * CHUNK)
            q, k, v, la, be = q_all[r], k_all[r], v_all[r], la_all[r], be_all[r]
            g_t = _sel_dot(m_incl, la)
            g_s = _sel_dot(ones_cc, la * m_before)
            gam = jnp.where(incl, jnp.exp(jnp.minimum(g_t - g_s, 0.0)), 0.0)
            k_bd = bd(k)
            kk = _dot_nt(k, k_bd)
            qk = _dot_nt(q, k_bd)
            a = jnp.where(strict, be * kk * gam, 0.0)
            eg = jnp.exp(g_t)
            g_end = g_t[end_row:end_row + 1, :]
            qh_ref[dd, r, :] = (q * eg).astype(BF16)
            qg_ref[dd, r, :] = (qk * gam).astype(BF16)
            kh_ref[dd, r, :] = (k * jnp.exp(g_end - g_t)).astype(BF16)
            ee_ref[dd, 0, c:c + 1, :] = jnp.exp(g_end)
            work.append((dd, r, a, bd(be * eg * k), bd(be * v), levels))
            t_invs.append(eye_f - jnp.where(pair, a, 0.0))
    for lv in range(int(math.log2(CHUNK)) - 1):
        xs = [_dot(jnp.where(wk[5][lv], wk[2], 0.0), bd(t)) for wk, t in zip(work, t_invs)]
        t_invs = [t - _dot(t, bd(x)) for t, x in zip(t_invs, xs)]
    for (dd, r, _, wk_bd, vb_bd, _), t in zip(work, t_invs):
        w_ref[dd, r, :] = _dot(t, wk_bd).astype(BF16)
        u0_ref[dd, r, :] = _dot(t, vb_bd)


def _gdn_prep(pb, pg, conv_w, alog_x, dtb_x, cos_t, sin_t, dims):
    n_all = pb.shape[0]
    n_lat, t_len, n_batch, d_a, d_b, d_c = dims
    n_lat_blk = t_len // TOK_BLK
    n_ctx_blk = (n_all - n_lat) // n_batch // TOK_BLK
    n_blk = n_all // TOK_BLK
    halo = 8
    per = TOK_BLK // halo
    kern = functools.partial(_gdn_prep_kernel, n_lat_blk=n_lat_blk, n_ctx_blk=n_ctx_blk,
                             n_lat_blks_total=n_lat // TOK_BLK, d_b=d_b)
    row = lambda i: (i, 0)
    const2 = lambda i: (0, 0)
    const3 = lambda i: (0, 0, 0)
    tab = lambda i: (jnp.where(i < n_lat // TOK_BLK, i % n_lat_blk, 0), 0)
    both = pl.BlockSpec((2, TOK_BLK, d_b), lambda i: (0, i, 0))
    sds = jax.ShapeDtypeStruct
    return pl.pallas_call(
        kern,
        name="gdn_prep",
        out_shape=(sds((2, n_all, d_b), BF16), sds((2, n_all, d_b), F32), sds((2, n_all, d_b), BF16),
                   sds((2, n_all, d_b), BF16), sds((2, n_all, d_b), BF16), sds((2, n_blk, 8, d_b), F32)),
        grid=(n_blk,),
        in_specs=[pl.BlockSpec((TOK_BLK, 3 * d_b), row),
                  pl.BlockSpec((halo, 3 * d_b), lambda i: (jnp.maximum(i * per - 1, 0), 0)),
                  pl.BlockSpec((halo, 3 * d_b), lambda i: (jnp.minimum((i + 1) * per, n_blk * per - 1), 0)),
                  pl.BlockSpec((TOK_BLK, LANES), row),
                  pl.BlockSpec((8, 3 * d_b), const2),
                  pl.BlockSpec((2, 1, d_b), const3),
                  pl.BlockSpec((2, 1, d_b), const3),
                  pl.BlockSpec((TOK_BLK, d_b), tab),
                  pl.BlockSpec((TOK_BLK, d_b), tab)],
        out_specs=(both, both, both, both, both, pl.BlockSpec((2, 1, 8, d_b), lambda i: (0, i, 0, 0))),
        compiler_params=_cparams(("parallel",)),
    )(pb, pb, pb, pg, conv_w, alog_x, dtb_x, cos_t, sin_t)


def _gdn_kernel(*refs, n_chunk):
    ins = (refs[0:6], refs[6:12])
    outs = refs[12:14]
    st_ref = refs[14]
    j = pl.program_id(1)
    n = outs[0].shape[-1]
    n_rep = n // HEAD_DIM

    @pl.when(j == 0)
    def _():
        st_ref[...] = jnp.zeros_like(st_ref)

    mask_bd = _head_mask(n)
    bd = lambda a: _block_diag(a, mask_bd, n_rep)
    sts = [st_ref[0], st_ref[1]]
    for c in range(n_chunk):
        for dd in range(2):
            w_ref, u0_ref, qh_ref, qg_ref, kh_ref, ee_ref = ins[dd]
            cc = c if dd == 0 else n_chunk - 1 - c
            r = slice(cc * CHUNK, (cc + 1) * CHUNK)
            st = sts[dd]
            u = u0_ref[0, r, :] - _dot_nt(w_ref[0, r, :], st)
            outs[dd][r, :] = _dot_nt(qh_ref[0, r, :], st) + _dot(qg_ref[0, r, :], bd(u))
            sts[dd] = st * ee_ref[0, 0, cc:cc + 1, :] + jnp.where(mask_bd, _dot_tn(u, kh_ref[0, r, :]), 0.0)
    st_ref[0] = sts[0]
    st_ref[1] = sts[1]


def _gdn_scan(wy, dims):
    n_all = wy[0].shape[1]
    n_lat, t_len, n_batch, d_a, d_b, d_c = dims
    n_lat_blk = t_len // TOK_BLK
    n_ctx_blk = (n_all - n_lat) // n_batch // TOK_BLK
    blk = functools.partial(_scan_block_index, n_lat_blk=n_lat_blk, n_ctx_blk=n_ctx_blk, n_batch=n_batch)
    kern = functools.partial(_gdn_kernel, n_chunk=TOK_BLK // CHUNK)
    in_specs = []
    for dd in range(2):
        tok = pl.BlockSpec((1, TOK_BLK, d_b), lambda b, j, dd=dd: (dd, blk(b, dd, j), 0))
        in_specs += [tok] * 5 + [pl.BlockSpec((1, 1, 8, d_b), lambda b, j, dd=dd: (dd, blk(b, dd, j), 0, 0))]
    out_specs = tuple(pl.BlockSpec((TOK_BLK, d_b), lambda b, j, dd=dd: (blk(b, dd, j), 0)) for dd in range(2))
    return pl.pallas_call(
        kern,
        name="gdn_scan",
        out_shape=(jax.ShapeDtypeStruct((n_all, d_b), F32), jax.ShapeDtypeStruct((n_all, d_b), F32)),
        grid=(n_batch, n_lat_blk + n_ctx_blk),
        in_specs=in_specs,
        out_specs=out_specs,
        scratch_shapes=[pltpu.VMEM((2, d_b, d_b), F32)],
        compiler_params=_cparams(("parallel", "arbitrary")),
    )(*wy, *wy)


def _stack_heads(q, n_rep):
    m, n = q.shape
    keep = (_iota((n_rep * m, n), 0) // m) == (_iota((n_rep * m, n), 1) // HEAD_DIM)
    return jnp.where(keep, jnp.concatenate([q] * n_rep, axis=0), jnp.zeros((), q.dtype))


def _fold_heads(o, m, n_rep):
    n = o.shape[1]
    lane_h = _iota((m, n), 1) // HEAD_DIM
    acc = jnp.zeros((m, n), F32)
    for h in range(n_rep):
        acc = acc + jnp.where(lane_h == h, o[h * m:(h + 1) * m, :], 0.0)
    return acc


def _natten_kernel(q_ref, k_ref, v_ref, kc_ref, vc_ref, bias_ref, o_ref, *, n_rows):
    r = pl.program_id(1)
    n = q_ref.shape[-1]
    n_rep = n // HEAD_DIM
    kr = min(WIN_R, n_rows)
    r0 = jnp.clip(r - WIN_R // 2, 0, n_rows - kr)
    sl = pl.ds(pl.multiple_of(r0 * GRID_W, GRID_W), kr * GRID_W)
    dn = (((1,), (1,)), ((), ()))
    n_q = q_ref.shape[0]
    gw = NA_HEADS_PER_GROUP * HEAD_DIM
    for g in range(n // gw):
        lanes = slice(g * gw, (g + 1) * gw)
        qs = _stack_heads(q_ref[:, lanes], NA_HEADS_PER_GROUP)
        rows = slice(g * NA_HEADS_PER_GROUP * n_q, (g + 1) * NA_HEADS_PER_GROUP * n_q)
        s_loc = lax.dot_general(qs, k_ref[sl, lanes], dn, preferred_element_type=F32) + bias_ref[0, rows, :]
        s_ctx = lax.dot_general(qs, kc_ref[:, lanes], dn, preferred_element_type=F32)
        m = jnp.maximum(jnp.max(s_loc, axis=-1, keepdims=True), jnp.max(s_ctx, axis=-1, keepdims=True))
        p_loc = jnp.exp(s_loc - m)
        p_ctx = jnp.exp(s_ctx - m)
        inv = 1.0 / (jnp.sum(p_loc, axis=-1, keepdims=True) + jnp.sum(p_ctx, axis=-1, keepdims=True))
        o = jnp.dot((p_loc * inv).astype(BF16), v_ref[sl, lanes], preferred_element_type=F32)
        o = o + jnp.dot((p_ctx * inv).astype(BF16), vc_ref[:, lanes], preferred_element_type=F32)
        o_ref[:, lanes] = _fold_heads(o, n_q, NA_HEADS_PER_GROUP)


def _natten(qn, kn, vv, bias_tab, dims):
    n_lat, t_len, n_batch, d_a, d_b, d_c = dims
    n_all = qn.shape[0]
    l_ctx = (n_all - n_lat) // n_batch
    n_rows = t_len // GRID_W
    kr = min(WIN_R, n_rows)
    ctx0 = n_lat // l_ctx

    def cfg(b, r):
        r0 = jnp.clip(r - WIN_R // 2, 0, n_rows - kr)
        return (r - r0, 0, 0)

    kern = functools.partial(_natten_kernel, n_rows=n_rows)
    return pl.pallas_call(
        kern,
        name="natten",
        out_shape=jax.ShapeDtypeStruct((n_lat, d_c), F32),
        grid=(n_batch, n_rows),
        in_specs=[pl.BlockSpec((GRID_W, d_c), lambda b, r: (b * n_rows + r, 0)),
                  pl.BlockSpec((t_len, d_c), lambda b, r: (b, 0)),
                  pl.BlockSpec((t_len, d_c), lambda b, r: (b, 0)),
                  pl.BlockSpec((l_ctx, d_c), lambda b, r: (ctx0 + b, 0)),
                  pl.BlockSpec((l_ctx, d_c), lambda b, r: (ctx0 + b, 0)),
                  pl.BlockSpec((1,) + bias_tab.shape[1:], cfg)],
        out_specs=pl.BlockSpec((GRID_W, d_c), lambda b, r: (b * n_rows + r, 0)),
        compiler_params=_cparams(("parallel", "arbitrary")),
    )(qn, kn, vv, kn, vv, bias_tab)


def _ctx_attn_kernel(q_ref, k_ref, v_ref, o_ref):
    n = q_ref.shape[-1]
    n_rep = n // HEAD_DIM
    qs = _stack_heads(q_ref[...], n_rep)
    s = lax.dot_general(qs, k_ref[...], (((1,), (1,)), ((), ())), preferred_element_type=F32)
    p = jnp.exp(s - jnp.max(s, axis=-1, keepdims=True))
    p = p * (1.0 / jnp.sum(p, axis=-1, keepdims=True))
    o = jnp.dot(p.astype(BF16), v_ref[...], preferred_element_type=F32)
    o_ref[...] = _fold_heads(o, q_ref.shape[0], n_rep)


def _ctx_attn(qn, kn, vv, dims):
    n_lat, t_len, n_batch, d_a, d_b, d_c = dims
    n_all = qn.shape[0]
    l_ctx = (n_all - n_lat) // n_batch
    tq = 64
    per = l_ctx // tq
    q0 = n_lat // tq
    c0 = n_lat // l_ctx
    return pl.pallas_call(
        _ctx_attn_kernel,
        name="ctx_attn",
        out_shape=jax.ShapeDtypeStruct((n_all - n_lat, d_c), F32),
        grid=(n_batch, per),
        in_specs=[pl.BlockSpec((tq, d_c), lambda b, i: (q0 + b * per + i, 0)),
                  pl.BlockSpec((l_ctx, d_c), lambda b, i: (c0 + b, 0)),
                  pl.BlockSpec((l_ctx, d_c), lambda b, i: (c0 + b, 0))],
        out_specs=pl.BlockSpec((tq, d_c), lambda b, i: (b * per + i, 0)),
        compiler_params=_cparams(("parallel", "arbitrary")),
    )(qn, kn, vv)


def _natten_bias(rpb, n_rows):
    n_h = rpb.shape[0]
    kr = min(WIN_R, n_rows)
    cols = jnp.arange(GRID_W)
    c0 = jnp.clip(cols - WIN_C // 2, 0, GRID_W - WIN_C)
    kc = jnp.arange(GRID_W)
    in_win = (kc[None, :] >= c0[:, None]) & (kc[None, :] < c0[:, None] + WIN_C)
    per = 2 * GRID_W
    rp = rpb.astype(F32)
    u = jnp.concatenate([rp[..., WIN_C - 1:], jnp.zeros(rp.shape[:-1] + (per - 2 * WIN_C + 1,), F32),
                         rp[..., :WIN_C - 1]], axis=-1)
    rel = jnp.tile(u, (1, 1, GRID_W))[..., :GRID_W * (per - 1)]
    rel = rel.reshape(rp.shape[:-1] + (GRID_W, per - 1))[..., :GRID_W]
    toe = jnp.where(in_win[None, None], rel, NEG_BIG)
    tabs = jnp.stack([toe[:, WIN_R - 1 - delta:WIN_R - 1 - delta + kr] for delta in range(kr)])
    return tabs.transpose(0, 1, 3, 2, 4).reshape(kr, n_h * GRID_W, kr * GRID_W)


def _outproj_kernel(x_ref, m_ref, oa_ref, ga_ref, obf_ref, obb_ref, gb_ref, oc_ref, na_ref, nb_ref, w_ref, o_ref,
                    *, d_a, d_b):
    oa = oa_ref[0] + oa_ref[1]
    ya = oa * lax.rsqrt(_group_mean_sq(oa) + EPS) * na_ref[...] * _silu(ga_ref[...])
    ob = obf_ref[...] + obb_ref[...]
    yb = ob * lax.rsqrt(_group_mean_sq(ob) + EPS) * nb_ref[...] * _silu(gb_ref[...])
    acc = _dot(ya, w_ref[0:d_a, :])
    acc = acc + _dot(yb, w_ref[d_a:d_a + d_b, :])
    acc = acc + _dot(oc_ref[...], w_ref[d_a + d_b:, :])
    o_ref[...] = x_ref[...] + m_ref[0, 2:3, :] * acc


def _outproj(xall, mod, oa, pa, obf, obb, pb, oc, na_g, nb_g, w_out, n_rows_out, dims):
    n_lat, t_len, n_batch, d_a, d_b, d_c = dims
    d = xall.shape[1]
    tm = ROW_TILE
    row = lambda i: (i, 0)
    const = lambda i: (0, 0)
    grp = lambda i: (_mod_group(i, tm, n_lat, t_len, n_batch), 0, 0)
    kern = functools.partial(_outproj_kernel, d_a=d_a, d_b=d_b)
    return pl.pallas_call(
        kern,
        name="outproj",
        out_shape=jax.ShapeDtypeStruct((n_rows_out, d), F32),
        grid=(n_rows_out // tm,),
        in_specs=[pl.BlockSpec((tm, d), row),
                  pl.BlockSpec((1, 6, d), grp),
                  pl.BlockSpec((2, tm, d_a), lambda i: (0, i, 0)),
                  pl.BlockSpec((tm, d_a), lambda i: (i, 4)),
                  pl.BlockSpec((tm, d_b), row),
                  pl.BlockSpec((tm, d_b), row),
                  pl.BlockSpec((tm, d_b), lambda i: (i, 3)),
                  pl.BlockSpec((tm, d_c), row),
                  pl.BlockSpec((1, d_a), const),
                  pl.BlockSpec((1, d_b), const),
                  pl.BlockSpec(w_out.shape, const)],
        out_specs=pl.BlockSpec((tm, d), row),
        compiler_params=_cparams(("parallel",)),
    )(xall, mod, oa, pa, obf, obb, pb, oc, na_g, nb_g, w_out)


def _ffn_kernel(x_ref, m_ref, g_ref, w1_ref, w3_ref, w2_ref, o_ref, h_ref, acc_ref):
    j = pl.program_id(1)

    @pl.when(j == 0)
    def _():
        h_ref[...] = _modulated_norm(x_ref[...], g_ref[...], m_ref[0, 3:4, :], m_ref[0, 4:5, :]).astype(BF16)
        acc_ref[...] = jnp.zeros_like(acc_ref)

    h = h_ref[...]
    a = jnp.dot(h, w1_ref[...], preferred_element_type=F32)
    b = jnp.dot(h, w3_ref[...], preferred_element_type=F32)
    acc_ref[...] += _dot(_silu(a) * b, w2_ref[...])

    @pl.when(j == pl.num_programs(1) - 1)
    def _():
        o_ref[...] = x_ref[...] + m_ref[0, 5:6, :] * acc_ref[...]


def _ffn(xall, mod, norm_g, w1, w3, w2, dims):
    n_lat, t_len, n_batch, d_a, d_b, d_c = dims
    n_rows, d = xall.shape
    d_ff = w1.shape[1]
    tm, tf = FFN_ROWS, FF_TILE
    assert t_len % tm == 0
    grp = lambda i, j: (_mod_group(i, tm, n_lat, t_len, n_batch), 0, 0)
    return pl.pallas_call(
        _ffn_kernel,
        name="ffn",
        out_shape=jax.ShapeDtypeStruct((n_rows, d), F32),
        grid=(pl.cdiv(n_rows, tm), d_ff // tf),
        in_specs=[pl.BlockSpec((tm, d), lambda i, j: (i, 0)),
                  pl.BlockSpec((1, 6, d), grp),
                  pl.BlockSpec((1, d), lambda i, j: (0, 0)),
                  pl.BlockSpec((d, tf), lambda i, j: (0, j)),
                  pl.BlockSpec((d, tf), lambda i, j: (0, j)),
                  pl.BlockSpec((tf, d), lambda i, j: (j, 0))],
        out_specs=pl.BlockSpec((tm, d), lambda i, j: (i, 0)),
        scratch_shapes=[pltpu.VMEM((tm, d), BF16), pltpu.VMEM((tm, d), F32)],
        compiler_params=_cparams(("parallel", "arbitrary")),
    )(xall, mod, norm_g, w1, w3, w2)


def _rows_to_tiles(x):
    r, d = x.shape
    slabs = jnp.stack([x[:, s * LANES:(s + 1) * LANES].reshape(r // SUBLANES, SUBLANES, LANES)
                       for s in range(d // LANES)], axis=1)
    return jnp.swapaxes(slabs, 1, 2).reshape(r, d // LANES, LANES)


def _tiles_to_slabs(x3):
    r, n_tile, _ = x3.shape
    y = jnp.swapaxes(x3.reshape(r // SUBLANES, SUBLANES, n_tile, LANES), 1, 2)
    return [y[:, s].reshape(r, LANES) for s in range(n_tile)]


def _route_kernel(x_ref, m_ref, g_ref, wr_ref, br_ref, h_ref, e_ref, gt_ref):
    h = _modulated_norm(x_ref[...], g_ref[...], m_ref[0, 3:4, :], m_ref[0, 4:5, :])
    h_ref[...] = _rows_to_tiles(h)
    lane = _iota((h.shape[0], LANES), 1)
    logits = jnp.where(lane < N_EXPERTS, _dot_hi(h, wr_ref[...]) + br_ref[...], -jnp.inf)
    m1 = jnp.max(logits, axis=-1, keepdims=True)
    lane_f = lane.astype(F32)
    i1 = jnp.min(jnp.where(logits == m1, lane_f, float(LANES)), axis=-1, keepdims=True).astype(jnp.int32)
    rest = jnp.where(lane == i1, -jnp.inf, logits)
    m2 = jnp.max(rest, axis=-1, keepdims=True)
    i2 = jnp.min(jnp.where(rest == m2, lane_f, float(LANES)), axis=-1, keepdims=True).astype(jnp.int32)
    e2 = jnp.exp(m2 - m1)
    g1 = 1.0 / (1.0 + e2)
    g2 = e2 / (1.0 + e2)
    e_ref[...] = jnp.where(lane == 0, i1, jnp.where(lane == 1, i2, 0))
    gt_ref[...] = jnp.where(lane == 0, g1, jnp.where(lane == 1, g2, 0.0))


def _route(x, mod, norm_g, wr_pad, br_pad, dims):
    n_lat, t_len, n_batch, d_a, d_b, d_c = dims
    n, d = x.shape
    tm = ROW_TILE
    row = lambda i: (i, 0)
    const = lambda i: (0, 0)
    grp = lambda i: (_mod_group(i, tm, n_lat, t_len, n_batch), 0, 0)
    return pl.pallas_call(
        _route_kernel,
        name="moe_route",
        out_shape=(jax.ShapeDtypeStruct((n, d // LANES, LANES), F32), jax.ShapeDtypeStruct((n, LANES), jnp.int32),
                   jax.ShapeDtypeStruct((n, LANES), F32)),
        grid=(n // tm,),
        in_specs=[pl.BlockSpec((tm, d), row), pl.BlockSpec((1, 6, d), grp), pl.BlockSpec((1, d), const),
                  pl.BlockSpec((d, LANES), const), pl.BlockSpec((1, LANES), const)],
        out_specs=(pl.BlockSpec((tm, d // LANES, LANES), lambda i: (i, 0, 0)), pl.BlockSpec((tm, LANES), row),
                   pl.BlockSpec((tm, LANES), row)),
        compiler_params=_cparams(("parallel",)),
    )(x, mod, norm_g, wr_pad, br_pad)


def _expert_kernel(be_ref, nv_ref, idx0_ref, idxn_ref, dstp_ref, dstc_ref, h_hbm, w1_ref, w3_ref, w2_ref, y_hbm,
                   xbuf_ref, hb_ref, acc_ref, out_ref, sem_in, sem_out, *, rows_per_step, n_pairs):
    i = pl.program_id(0)
    j = pl.program_id(1)
    n_blk = pl.num_programs(0)
    n_ff = pl.num_programs(1)
    bm = hb_ref.shape[0]
    rps = rows_per_step
    n_issue = xbuf_ref.shape[1]
    n_tile = xbuf_ref.shape[2]
    live = i < nv_ref[0]
    cur = i % 2
    nxt = 1 - cur

    def in_copy(tok, slot, r):
        return pltpu.make_async_copy(h_hbm.at[pl.ds(tok, 1)], xbuf_ref.at[slot, pl.ds(r, 1)], sem_in)

    def out_copy(slot, r, dst):
        return pltpu.make_async_copy(out_ref.at[slot, pl.ds(r, 1)], y_hbm.at[pl.ds(dst, 1)], sem_out)

    def wait_in(slot):
        for c in range(n_issue // rps):
            pltpu.make_async_copy(h_hbm.at[pl.ds(0, rps)], xbuf_ref.at[slot, pl.ds(c * rps, rps)], sem_in).wait()

    def wait_out(slot):
        for c in range(n_issue // rps):
            pltpu.make_async_copy(out_ref.at[slot, pl.ds(c * rps, rps)], y_hbm.at[pl.ds(0, rps)], sem_out).wait()

    @pl.when((i == 0) & (j == 0))
    def _():
        out_ref[...] = jnp.zeros_like(out_ref)

        def start(g, carry):
            for u in range(GATHER_PARTS):
                r = g * GATHER_PARTS + u
                in_copy(idx0_ref[0, 0, r], 0, r).start()
            return carry

        lax.fori_loop(0, n_issue // GATHER_PARTS, start, 0)

    @pl.when(j == 0)
    def _():
        wait_in(cur)

    @pl.when(live & (j == 0))
    def _():
        for s, slab in enumerate(_tiles_to_slabs(xbuf_ref[cur, 0:bm])):
            hb_ref[:, s * LANES:(s + 1) * LANES] = slab.astype(BF16)
        acc_ref[...] = jnp.zeros_like(acc_ref)

    def issue(part):
        per = rps // GATHER_PARTS
        for t in range(part * per, (part + 1) * per):
            r = j * rps + t
            in_copy(idxn_ref[0, 0, r], nxt, r).start(priority=0)
            dst = jnp.where(i == 0, n_pairs + r, dstp_ref[0, 0, r])
            out_copy(nxt, r, dst).start(priority=1)

    @pl.when(live)
    def _():
        h = hb_ref[...]
        a = _dot(h, w1_ref[0])
        issue(0)
        b = _dot(h, w3_ref[0])
        issue(1)
        g = (_silu(a) * b).astype(BF16)
        issue(2)
        acc_ref[...] += jnp.dot(g, w2_ref[0].astype(BF16), preferred_element_type=F32)
        issue(3)

    @pl.when(jnp.logical_not(live))
    def _():
        for part in range(GATHER_PARTS):
            issue(part)

    last = j == n_ff - 1

    @pl.when(last)
    def _():
        wait_out(nxt)

    @pl.when(live & last)
    def _():
        out_ref[cur, 0:bm] = _rows_to_tiles(acc_ref[...])

    @pl.when((i == n_blk - 1) & last)
    def _():
        wait_in(nxt)

        def start(g, carry):
            for u in range(GATHER_PARTS):
                r = g * GATHER_PARTS + u
                out_copy(cur, r, dstc_ref[0, 0, r]).start()
            return carry

        lax.fori_loop(0, n_issue // GATHER_PARTS, start, 0)
        wait_out(cur)


def _experts(h, slot_tok, slot_pair, block_e, n_live, w1, w3, w2):
    n, n_tile, _ = h.shape
    d = n_tile * LANES
    d_ff = w1.shape[2]
    bm, tf = MOE_ROWS, FF_TILE
    n_blk = slot_tok.shape[0] // bm
    n_ff = d_ff // tf
    n_pairs = 2 * n
    rps = -(-bm // (n_ff * SUBLANES)) * SUBLANES
    assert rps % GATHER_PARTS == 0
    n_issue = rps * n_ff
    extra = n_issue - bm
    idx = jnp.pad(slot_tok.reshape(n_blk, 1, bm), ((0, 0), (0, 0), (0, extra)))
    dump = n_pairs + jnp.arange(n_issue, dtype=jnp.int32)
    dst = jnp.concatenate([slot_pair.reshape(n_blk, 1, bm), jnp.broadcast_to(dump[bm:], (n_blk, 1, extra))], axis=2)
    smem = lambda f: pl.BlockSpec((1, 1, n_issue), f, memory_space=pltpu.SMEM)
    grid_spec = pltpu.PrefetchScalarGridSpec(
        num_scalar_prefetch=2,
        grid=(n_blk, n_ff),
        in_specs=[smem(lambda i, j, be, nv: (0, 0, 0)),
                  smem(lambda i, j, be, nv: (jnp.minimum(i + 1, n_blk - 1), 0, 0)),
                  smem(lambda i, j, be, nv: (jnp.maximum(i - 1, 0), 0, 0)),
                  smem(lambda i, j, be, nv: (i, 0, 0)),
                  pl.BlockSpec(memory_space=pl.ANY),
                  pl.BlockSpec((1, d, tf), lambda i, j, be, nv: (be[i], 0, j)),
                  pl.BlockSpec((1, d, tf), lambda i, j, be, nv: (be[i], 0, j)),
                  pl.BlockSpec((1, tf, d), lambda i, j, be, nv: (be[i], j, 0))],
        out_specs=pl.BlockSpec(memory_space=pl.ANY),
        scratch_shapes=[pltpu.VMEM((2, n_issue, n_tile, LANES), F32), pltpu.VMEM((bm, d), BF16),
                        pltpu.VMEM((bm, d), F32), pltpu.VMEM((2, n_issue, n_tile, LANES), F32),
                        pltpu.SemaphoreType.DMA(()), pltpu.SemaphoreType.DMA(())])
    return pl.pallas_call(
        functools.partial(_expert_kernel, rows_per_step=rps, n_pairs=n_pairs),
        name="moe_experts",
        out_shape=jax.ShapeDtypeStruct((n_pairs + n_issue, n_tile, LANES), F32),
        grid_spec=grid_spec,
        compiler_params=_cparams(("arbitrary", "arbitrary")),
    )(block_e, n_live, idx, idx, dst, dst, h, w1, w3, w2)


def _combine_kernel(x_ref, m_ref, gt_ref, y0_ref, y1_ref, o_ref):
    gt = gt_ref[...]
    for s, (y0, y1) in enumerate(zip(_tiles_to_slabs(y0_ref[...]), _tiles_to_slabs(y1_ref[...]))):
        cols = slice(s * LANES, (s + 1) * LANES)
        y = gt[:, 0:1] * y0 + gt[:, 1:2] * y1
        o_ref[:, cols] = x_ref[:, cols] + m_ref[0, 5:6, cols] * y


def _combine(x, mod, gates, y, dims):
    n_lat, t_len, n_batch, d_a, d_b, d_c = dims
    n, d = x.shape
    tm = ROW_TILE
    n_tile = y.shape[1]
    grp = lambda i: (_mod_group(i, tm, n_lat, t_len, n_batch), 0, 0)
    return pl.pallas_call(
        _combine_kernel,
        name="moe_combine",
        out_shape=jax.ShapeDtypeStruct((n, d), F32),
        grid=(n // tm,),
        in_specs=[pl.BlockSpec((tm, d), lambda i: (i, 0)),
                  pl.BlockSpec((1, 6, d), grp),
                  pl.BlockSpec((tm, LANES), lambda i: (i, 0)),
                  pl.BlockSpec((tm, n_tile, LANES), lambda i: (i, 0, 0)),
                  pl.BlockSpec((tm, n_tile, LANES), lambda i: (i + n // tm, 0, 0))],
        out_specs=pl.BlockSpec((tm, d), lambda i: (i, 0)),
        compiler_params=_cparams(("parallel",)),
    )(x, mod, gates, y, y)


def _moe(x, mod, norm_g, w_router, b_router, w1, w3, w2, dims):
    n, d = x.shape
    wr_pad = jnp.zeros((d, LANES), F32).at[:, :N_EXPERTS].set(w_router.astype(F32))
    br_pad = jnp.zeros((1, LANES), F32).at[0, :N_EXPERTS].set(b_router.astype(F32))
    h, e_tile, g_tile = _route(x, mod, norm_g, wr_pad, br_pad, dims)
    bm = MOE_ROWS
    e_flat = e_tile[:, :2].reshape(-1)
    onehot = (e_flat[:, None] == jnp.arange(N_EXPERTS, dtype=jnp.int32)[None, :]).astype(jnp.int32)
    csum = jnp.cumsum(onehot, axis=0)
    counts = csum[-1]
    rank = jnp.sum(csum * onehot, axis=1) - 1
    padded = (counts + bm - 1) // bm * bm
    pad_end = jnp.cumsum(padded)
    pad_start = pad_end - padded
    slot = pad_start[e_flat] + rank
    n_blocks = (2 * n) // bm + N_EXPERTS
    pair_flat = jnp.arange(2 * n, dtype=jnp.int32)
    slot_ids = jnp.arange(n_blocks * bm, dtype=jnp.int32)
    slot_pair = (2 * n + slot_ids % bm).at[slot].set((pair_flat % 2) * n + pair_flat // 2)
    slot_tok = jnp.where(slot_pair < 2 * n, slot_pair % n, slot_ids % n)
    blk_start = jnp.arange(n_blocks, dtype=jnp.int32) * bm
    block_e = jnp.minimum(jnp.sum((pad_end[None, :] <= blk_start[:, None]).astype(jnp.int32), axis=1),
                          N_EXPERTS - 1).astype(jnp.int32)
    n_live = (pad_end[-1] // bm).astype(jnp.int32).reshape(1)
    y = _experts(h, slot_tok, slot_pair, block_e, n_live, w1, w3, w2)
    return _combine(x, mod, g_tile, y, dims)


def _rope_tables(t_len, n_heads):
    t = jnp.arange(t_len)
    row = (t // GRID_W).astype(F32)
    col = (t % GRID_W).astype(F32)
    n_freq = HEAD_DIM // 4
    inv = ROPE_BASE ** (-jnp.arange(n_freq, dtype=F32) / n_freq)
    ang = jnp.concatenate([row[:, None] * inv, col[:, None] * inv], axis=-1)
    cos, sin = jnp.cos(ang), jnp.sin(ang)
    cos_h = jnp.concatenate([cos, cos], axis=-1)
    sin_h = jnp.concatenate([-sin, sin], axis=-1)
    return jnp.tile(cos_h, (1, n_heads)), jnp.tile(sin_h, (1, n_heads))


def kernel(x, c, ctx, c_ctx, ada_w, ada_b, norm1_g, norm2_g, w_in, w_out, hgrn_lb_raw, hgrn_onorm_g,
           gdn_conv_w, gdn_a_log, gdn_dt_bias, gdn_onorm_g, na_qnorm_g, na_knorm_g, na_rpb, ffn_w1, ffn_w3,
           ffn_w2, moe_router_w, moe_router_b, moe_w1, moe_w3, moe_w2):
    n_batch, t_len, d = x.shape
    l_ctx = ctx.shape[1]
    depth = w_in.shape[0]
    d_a = hgrn_lb_raw.shape[-1]
    d_b = gdn_conv_w.shape[-1] // 3
    n_hb = gdn_a_log.shape[-1]
    n_hc = na_rpb.shape[1]
    d_c = n_hc * HEAD_DIM
    n_lat = n_batch * t_len
    n_all = n_lat + n_batch * l_ctx
    dims = (n_lat, t_len, n_batch, d_a, d_b, d_c)
    assert d_a % HEAD_DIM == 0 and d_b == n_hb * HEAD_DIM
    assert t_len % ROW_TILE == 0 and (n_batch * l_ctx) % ROW_TILE == 0 and l_ctx % TOK_BLK == 0
    assert t_len % GRID_W == 0 and 4 * n_hb <= LANES

    cvec = jnp.zeros((8, d), F32).at[:n_batch].set(c.astype(F32)).at[n_batch].set(c_ctx.astype(F32))
    mod_all = _adaln(cvec, ada_w, ada_b)

    lb_soft = jax.nn.softmax(hgrn_lb_raw.astype(F32), axis=1)
    lower_bound = jnp.cumsum(lb_soft, axis=1) - lb_soft[:, :1]
    cos_t, sin_t = _rope_tables(t_len, n_hb)
    n_gate = 4 * n_hb
    sizes_a, sizes_b = 5 * d_a, 4 * d_b

    xall = jnp.concatenate([x.reshape(n_lat, d), ctx.reshape(n_batch * l_ctx, d)], axis=0).astype(F32)
    for l in range(depth):
        last = l == depth - 1
        mod = mod_all[l, :n_batch + 1].reshape(n_batch + 1, 6, d)
        w = w_in[l]
        gate_cols = jnp.pad(w[:, sizes_a + sizes_b:sizes_a + sizes_b + n_gate], ((0, 0), (0, LANES - n_gate)))
        w_pad = jnp.concatenate([w[:, :sizes_a + sizes_b], gate_cols, w[:, sizes_a + sizes_b + n_gate:]],
                                axis=1).astype(BF16)
        qn_g = jnp.tile(na_qnorm_g[l].astype(F32), n_hc)[None]
        kn_g = jnp.tile(na_knorm_g[l].astype(F32), n_hc)[None]
        pa, pb, pg, qn, kn, vv = _inproj(xall, mod, norm1_g[l][None].astype(F32), w_pad, qn_g, kn_g, dims)

        oa = _hgrn_scan(pa, lower_bound[:, l][:, None, :], dims)
        conv_w = jnp.zeros((8, 3 * d_b), F32).at[:CONV_K].set(gdn_conv_w[l].astype(F32))
        alog_x = jnp.repeat(gdn_a_log[l].astype(F32), HEAD_DIM, axis=-1)[:, None, :]
        dtb_x = jnp.repeat(gdn_dt_bias[l].astype(F32), HEAD_DIM, axis=-1)[:, None, :]
        obf, obb = _gdn_scan(_gdn_prep(pb, pg, conv_w, alog_x, dtb_x, cos_t, sin_t, dims), dims)
        bias_tab = _natten_bias(na_rpb[l], t_len // GRID_W)
        oc = _natten(qn, kn, vv, bias_tab, dims)
        n_out = n_lat if last else n_all
        if not last:
            oc = jnp.concatenate([oc, _ctx_attn(qn, kn, vv, dims)], axis=0)
        na_g = jnp.tile(hgrn_onorm_g[l].astype(F32), d_a // HEAD_DIM)[None]
        nb_g = jnp.tile(gdn_onorm_g[l].astype(F32), n_hb)[None]
        xall_mid = _outproj(xall, mod, oa, pa, obf, obb, pb, oc, na_g, nb_g, w_out[l].astype(BF16), n_out, dims)
        i = l // 2
        if l % 2 == 0:
            xall = _ffn(xall_mid, mod, norm2_g[l][None].astype(F32), ffn_w1[i].astype(BF16),
                        ffn_w3[i].astype(BF16), ffn_w2[i].astype(BF16), dims)
        else:
            xall = _moe(xall_mid, mod, norm2_g[l][None].astype(F32), moe_router_w[i], moe_router_b[i],
                        moe_w1[i], moe_w3[i], moe_w2[i], dims)
    return xall[:n_lat].reshape(n_batch, t_len, d).astype(x.dtype)
```

```python
import functools
import math

import jax
import jax.numpy as jnp
import numpy as np
from jax import lax
from jax.experimental import pallas as pl
from jax.experimental.pallas import tpu as pltpu

F32 = jnp.float32
BF16 = jnp.bfloat16
HI = lax.Precision.HIGHEST

EPS = 1e-6
HEAD_DIM = 64
CHUNK = 64
TOK_BLK = 256
GRID_W = 64
WIN_R = 8
WIN_C = 16
CONV_K = 5
ROPE_BASE = 10000.0
N_EXPERTS = 8
LANES = 128
SUBLANES = 8
ROW_TILE = 512
FFN_ROWS = 1024
FF_TILE = 512
MOE_ROWS = 1024
GATHER_PARTS = 4
NA_HEADS_PER_GROUP = 4
NA_ROWS_PER_STEP = 2
VMEM_LIMIT = 56 * 1024 * 1024
MXU_DEPTH = 256
NEG_BIG = -1e30


def _cparams(sem):
    return pltpu.CompilerParams(dimension_semantics=sem, vmem_limit_bytes=VMEM_LIMIT)


def _dot(a, b):
    return jnp.dot(a.astype(BF16), b.astype(BF16), preferred_element_type=F32)


def _dot_nt(a, b):
    return lax.dot_general(a.astype(BF16), b.astype(BF16), (((1,), (1,)), ((), ())),
                           preferred_element_type=F32)


def _dot_tn(a, b):
    return lax.dot_general(a.astype(BF16), b.astype(BF16), (((0,), (0,)), ((), ())),
                           preferred_element_type=F32)


def _dot_hi(a, b):
    return jnp.dot(a, b, precision=HI, preferred_element_type=F32)


def _iota(shape, dim):
    return lax.broadcasted_iota(jnp.int32, shape, dim)


def _head_mask(n):
    return (_iota((n, n), 0) // HEAD_DIM) == (_iota((n, n), 1) // HEAD_DIM)


def _head_sums(x, scale):
    n = x.shape[-1]
    w = min(n, MXU_DEPTH)
    gm = jnp.where(_head_mask(w), scale, 0.0).astype(BF16)
    hi = x.astype(BF16)
    lo = (x - hi.astype(F32)).astype(BF16)
    dot = lambda a: jnp.dot(a, gm, preferred_element_type=F32)
    parts = [dot(lo[:, s:s + w]) + dot(hi[:, s:s + w]) for s in range(0, n, w)]
    return parts[0] if len(parts) == 1 else jnp.concatenate(parts, axis=1)


def _group_mean_sq(x):
    return _head_sums(x * x, 1.0 / HEAD_DIM)


def _silu(x):
    return x * jax.nn.sigmoid(x)


def _modulated_norm(x, g, shift, scale):
    ms = jnp.mean(x * x, axis=-1, keepdims=True)
    y = x * lax.rsqrt(ms + EPS) * g
    return y * (1.0 + scale) + shift


def _mod_group(i, tile, n_lat, t_len, n_batch):
    return jnp.where(i * tile < n_lat, (i * tile) // t_len, n_batch)


def _adaln_kernel(c_ref, w_ref, b_ref, o_ref):
    s = _silu(c_ref[...])
    o_ref[0] = _dot(s, w_ref[0]) + b_ref[0]


def _adaln(cvec, ada_w, ada_b):
    depth, d, d6 = ada_w.shape
    tn = 1024
    return pl.pallas_call(
        _adaln_kernel,
        name="adaln",
        out_shape=jax.ShapeDtypeStruct((depth, 8, d6), F32),
        grid=(depth, d6 // tn),
        in_specs=[pl.BlockSpec((8, d), lambda l, j: (0, 0)),
                  pl.BlockSpec((1, d, tn), lambda l, j: (l, 0, j)),
                  pl.BlockSpec((1, 1, tn), lambda l, j: (l, 0, j))],
        out_specs=pl.BlockSpec((1, 8, tn), lambda l, j: (l, 0, j)),
        compiler_params=_cparams(("parallel", "parallel")),
    )(cvec, ada_w, ada_b.reshape(depth, 1, d6))


def _inproj_kernel(x_ref, m_ref, g_ref, w_ref, qg_ref, kg_ref,
                   pa_ref, pb_ref, pg_ref, qn_ref, kn_ref, vv_ref, *, d_a5, d_b4, d_c):
    h = _modulated_norm(x_ref[...], g_ref[...], m_ref[0, 0:1, :], m_ref[0, 1:2, :]).astype(BF16)
    o0 = 0
    pa_ref[...] = jnp.dot(h, w_ref[:, o0:o0 + d_a5], preferred_element_type=F32)
    o0 += d_a5
    pb_ref[...] = jnp.dot(h, w_ref[:, o0:o0 + d_b4], preferred_element_type=F32)
    o0 += d_b4
    pg_ref[...] = jnp.dot(h, w_ref[:, o0:o0 + LANES], preferred_element_type=F32)
    o0 += LANES
    q = jnp.dot(h, w_ref[:, o0:o0 + d_c], preferred_element_type=F32)
    q = q * lax.rsqrt(_group_mean_sq(q) + EPS) * qg_ref[...]
    qn_ref[...] = (q * HEAD_DIM ** -0.5).astype(BF16)
    o0 += d_c
    k = jnp.dot(h, w_ref[:, o0:o0 + d_c], preferred_element_type=F32)
    k = k * lax.rsqrt(_group_mean_sq(k) + EPS) * kg_ref[...]
    kn_ref[...] = k.astype(BF16)
    o0 += d_c
    vv_ref[...] = jnp.dot(h, w_ref[:, o0:o0 + d_c], preferred_element_type=F32).astype(BF16)


def _inproj(xall, mod, norm_g, w_pad, qn_g, kn_g, dims):
    n_all, d = xall.shape
    n_lat, t_len, n_batch, d_a, d_b, d_c = dims
    tm = ROW_TILE
    row = lambda i: (i, 0)
    const = lambda i: (0, 0)
    grp = lambda i: (_mod_group(i, tm, n_lat, t_len, n_batch), 0, 0)
    kern = functools.partial(_inproj_kernel, d_a5=5 * d_a, d_b4=4 * d_b, d_c=d_c)
    return pl.pallas_call(
        kern,
        name="inproj",
        out_shape=(jax.ShapeDtypeStruct((n_all, 5 * d_a), F32),
                   jax.ShapeDtypeStruct((n_all, 4 * d_b), F32),
                   jax.ShapeDtypeStruct((n_all, LANES), F32),
                   jax.ShapeDtypeStruct((n_all, d_c), BF16),
                   jax.ShapeDtypeStruct((n_all, d_c), BF16),
                   jax.ShapeDtypeStruct((n_all, d_c), BF16)),
        grid=(n_all // tm,),
        in_specs=[pl.BlockSpec((tm, d), row),
                  pl.BlockSpec((1, 6, d), grp),
                  pl.BlockSpec((1, d), const),
                  pl.BlockSpec(w_pad.shape, const),
                  pl.BlockSpec((1, d_c), const),
                  pl.BlockSpec((1, d_c), const)],
        out_specs=(pl.BlockSpec((tm, 5 * d_a), row), pl.BlockSpec((tm, 4 * d_b), row),
                   pl.BlockSpec((tm, LANES), row), pl.BlockSpec((tm, d_c), row),
                   pl.BlockSpec((tm, d_c), row), pl.BlockSpec((tm, d_c), row)),
        compiler_params=_cparams(("parallel",)),
    )(xall, mod, norm_g, w_pad, qn_g, kn_g)


def _scan_block_index(b, d, j, n_lat_blk, n_ctx_blk, n_batch):
    jc = jnp.where(d == 0, j, n_ctx_blk - 1 - j)
    jl = jnp.where(d == 0, j - n_ctx_blk, n_lat_blk - 1 - (j - n_ctx_blk))
    return jnp.where(j < n_ctx_blk, n_batch * n_lat_blk + b * n_ctx_blk + jc, b * n_lat_blk + jl)


def _block_diag(x, mask_bd, n_rep):
    xb = x.astype(BF16)
    return jnp.where(mask_bd, jnp.concatenate([xb] * n_rep, axis=0), jnp.zeros((), BF16))


def _scan_order(idx, d):
    return idx + d * (CHUNK - 1 - 2 * idx)


def _split3(x):
    hi = x.astype(BF16)
    r = x - hi.astype(F32)
    mid = r.astype(BF16)
    lo = (r - mid.astype(F32)).astype(BF16)
    return hi, mid, lo


def _sel_dot(sel, x):
    hi, mid, lo = _split3(x)
    dot = lambda a: jnp.dot(sel, a, preferred_element_type=F32)
    return (dot(lo) + dot(mid)) + dot(hi)


def _dot_sel(x, sel):
    hi, mid, lo = _split3(x)
    dot = lambda a: jnp.dot(a, sel, preferred_element_type=F32)
    return (dot(lo) + dot(mid)) + dot(hi)


HGRN_LEVELS = tuple(CHUNK >> (i + 1) for i in range(int(math.log2(CHUNK))))


def _hgrn_kernel(qf_ref, zf_ref, vf_ref, qb_ref, zb_ref, vb_ref, lb_ref, of_ref, ob_ref, st_ref, *, n_chunk):
    ins = ((qf_ref, zf_ref, vf_ref, of_ref), (qb_ref, zb_ref, vb_ref, ob_ref))
    j = pl.program_id(1)
    n = qf_ref.shape[-1]
    n_rep = n // HEAD_DIM

    @pl.when(j == 0)
    def _():
        st_ref[...] = jnp.zeros_like(st_ref)

    mask_bd = _head_mask(n)
    gones = mask_bd.astype(BF16)
    bd = lambda a: _block_diag(a, mask_bd, n_rep)

    work = []
    masks = []
    for dd in range(2):
        q_ref, z_ref, v_ref, _ = ins[dd]
        lb = lb_ref[dd]
        s_t = _scan_order(_iota((CHUNK, CHUNK), 0), dd)
        s_u = _scan_order(_iota((CHUNK, CHUNK), 1), dd)
        sets = [s_u <= s_t, s_u > s_t]
        for m in HGRN_LEVELS:
            ref = (s_t // (2 * m)) * (2 * m) + m - 1
            sets.append((s_u > jnp.minimum(s_t, ref)) & (s_u <= jnp.maximum(s_t, ref)))
        sel = jnp.concatenate([jnp.where(a, 1.0, 0.0).astype(BF16) for a in sets], axis=0)
        s_row = _scan_order(_iota((CHUNK, n), 0), dd)
        s_col = _scan_order(_iota((CHUNK, n), 1) % HEAD_DIM, dd)
        masks.append((s_row == s_col, [(s_row // m) % 2 == 1 for m in HGRN_LEVELS],
                      [(s_row // (2 * m)) == (s_col // (2 * m)) for m in HGRN_LEVELS]))
        for c in range(n_chunk):
            cc = c if dd == 0 else n_chunk - 1 - c
            r = slice(cc * CHUNK, (cc + 1) * CHUNK)
            q = q_ref[r, :] * HEAD_DIM ** -0.5
            z = z_ref[r, :]
            v = v_ref[r, :]
            log_f = (jnp.minimum(z, 0.0) - jnp.log1p(jnp.exp(-jnp.abs(z)))) + jnp.log1p(lb * jnp.exp(-z))
            k = (1.0 - lb) * jax.nn.sigmoid(-z)
            work.append((dd, r, q, k, v, _sel_dot(sel, log_f)))
    atts = [jnp.where(masks[dd][0], jnp.dot((q * k).astype(BF16), gones, preferred_element_type=F32), 0.0)
            for dd, _, q, k, _, _ in work]
    for lvl, m in enumerate(HGRN_LEVELS):
        for i, (dd, _, q, k, _, sums) in enumerate(work):
            second = masks[dd][1][lvl]
            e = jnp.exp(sums[(2 + lvl) * CHUNK:(3 + lvl) * CHUNK])
            a_m = _dot_nt(jnp.where(second, q * e, 0.0), bd(jnp.where(second, 0.0, k * e)))
            if 2 * m < CHUNK:
                a_m = jnp.where(masks[dd][2][lvl], a_m, 0.0)
            atts[i] = atts[i] + a_m
    pre = []
    for (dd, r, q, k, v, sums), att in zip(work, atts):
        b_incl = sums[0:CHUNK]
        b_after = sums[CHUNK:2 * CHUNK]
        e_end = jnp.exp(b_incl[0:1, :] + b_after[0:1, :])
        upd = jnp.where(mask_bd, _dot_tn(v, k * jnp.exp(b_after)), 0.0)
        pre.append((dd, r, _dot(att, bd(v)), (q * jnp.exp(b_incl)).astype(BF16), e_end, upd))

    sts = [st_ref[0], st_ref[1]]
    for c in range(n_chunk):
        for dd in range(2):
            _, r, o_intra, q_hat, e_end, upd = pre[dd * n_chunk + c]
            ins[dd][3][r, :] = o_intra + _dot_nt(q_hat, sts[dd])
            sts[dd] = sts[dd] * e_end + upd
    st_ref[0] = sts[0]
    st_ref[1] = sts[1]


def _hgrn_scan(pa, lb2, dims):
    n_all = pa.shape[0]
    n_lat, t_len, n_batch, d_a, d_b, d_c = dims
    n_lat_blk = t_len // TOK_BLK
    n_ctx_blk = (n_all - n_lat) // n_batch // TOK_BLK
    blk = functools.partial(_scan_block_index, n_lat_blk=n_lat_blk, n_ctx_blk=n_ctx_blk, n_batch=n_batch)
    kern = functools.partial(_hgrn_kernel, n_chunk=TOK_BLK // CHUNK)
    in_specs = []
    for dd in range(2):
        in_specs += [pl.BlockSpec((TOK_BLK, d_a), lambda b, j, dd=dd: (blk(b, dd, j), 0)),
                     pl.BlockSpec((TOK_BLK, d_a), lambda b, j, dd=dd: (blk(b, dd, j), 1 + dd)),
                     pl.BlockSpec((TOK_BLK, d_a), lambda b, j, dd=dd: (blk(b, dd, j), 3))]
    in_specs.append(pl.BlockSpec((2, 1, d_a), lambda b, j: (0, 0, 0)))
    out_specs = tuple(pl.BlockSpec((TOK_BLK, d_a), lambda b, j, dd=dd: (blk(b, dd, j), 0)) for dd in range(2))
    return pl.pallas_call(
        kern,
        name="hgrn_scan",
        out_shape=(jax.ShapeDtypeStruct((n_all, d_a), F32), jax.ShapeDtypeStruct((n_all, d_a), F32)),
        grid=(n_batch, n_lat_blk + n_ctx_blk),
        in_specs=in_specs,
        out_specs=out_specs,
        scratch_shapes=[pltpu.VMEM((2, d_a, d_a), F32)],
        compiler_params=_cparams(("parallel", "arbitrary")),
    )(pa, pa, pa, pa, pa, pa, lb2)


def _gdn_prep_kernel(x_ref, prev_ref, next_ref, pg_ref, cw_ref, alog_ref, dtb_ref, cos_ref, sin_ref,
                     w_ref, u0_ref, qh_ref, qg_ref, kh_ref, ee_ref,
                     *, n_lat_blk, n_ctx_blk, n_lat_blks_total, d_b):
    i = pl.program_id(0)
    is_lat = i < n_lat_blks_total
    pos = jnp.where(is_lat, i % n_lat_blk, (i - n_lat_blks_total) % n_ctx_blk)
    n_seq_blk = jnp.where(is_lat, n_lat_blk, n_ctx_blk)
    first = pos == 0
    last = pos == n_seq_blk - 1
    x = x_ref[...]
    halo = prev_ref.shape[0]
    prev = jnp.where(first, 0.0, prev_ref[...])
    nxt = jnp.where(last, 0.0, next_ref[...])
    rows = _iota(x.shape, 0)
    tb = x.shape[0]
    half = CONV_K // 2
    acc = x * cw_ref[half:half + 1, :]
    for s in range(1, half + 1):
        xs = pltpu.roll(x, s, 0)
        for r in range(s):
            xs = jnp.where(rows == r, prev[halo - s + r:halo - s + r + 1, :], xs)
        acc = acc + xs * cw_ref[half - s:half - s + 1, :]
        xs = pltpu.roll(x, tb - s, 0)
        for r in range(s):
            xs = jnp.where(rows == tb - s + r, nxt[r:r + 1, :], xs)
        acc = acc + xs * cw_ref[half + s:half + s + 1, :]
    y = _silu(acc)
    def l2n(a):
        return a * lax.rsqrt(_head_sums(a * a, 1.0) + EPS)

    lane = _iota((tb, d_b), 1) % HEAD_DIM
    lo = lane < HEAD_DIM // 2

    def rope(a):
        partner = jnp.where(lo, pltpu.roll(a, d_b - HEAD_DIM // 2, 1), pltpu.roll(a, HEAD_DIM // 2, 1))
        return jnp.where(is_lat, a * cos_ref[...] + partner * sin_ref[...], a)

    q_all = rope(l2n(y[:, 0:d_b])) * HEAD_DIM ** -0.5
    k_all = rope(l2n(y[:, d_b:2 * d_b]))
    v_all = y[:, 2 * d_b:3 * d_b]
    n_h = d_b // HEAD_DIM
    g = pg_ref[...]
    e_r = _iota((LANES, d_b), 0)
    e_c = _iota((LANES, d_b), 1) // HEAD_DIM
    mask_bd = _head_mask(d_b)
    bd = lambda a: _block_diag(a, mask_bd, n_h)
    ones_cc = jnp.ones((CHUNK, CHUNK), BF16)
    ee_ref[...] = jnp.zeros_like(ee_ref)
    work = []
    t_invs = []
    for dd in range(2):
        a_x = _dot_sel(g, jnp.where(e_r == dd * n_h + e_c, 1.0, 0.0).astype(BF16))
        b_x = _dot_sel(g, jnp.where(e_r == (2 + dd) * n_h + e_c, 1.0, 0.0).astype(BF16))
        t = a_x + dtb_ref[dd]
        softplus = jnp.maximum(t, 0.0) + jnp.log1p(jnp.exp(-jnp.abs(t)))
        la_all = -jnp.exp(alog_ref[dd]) * softplus
        be_all = jax.nn.sigmoid(b_x)
        s_t = _scan_order(_iota((CHUNK, CHUNK), 0), dd)
        s_u = _scan_order(_iota((CHUNK, CHUNK), 1), dd)
        m_incl = jnp.where(s_u <= s_t, 1.0, 0.0).astype(BF16)
        s_row = _scan_order(_iota((CHUNK, d_b), 0), dd)
        s_col = _scan_order(_iota((CHUNK, d_b), 1) % HEAD_DIM, dd)
        incl = s_col <= s_row
        strict = s_col < s_row
        m_before = jnp.where(s_row <= s_col, 1.0, 0.0)
        eye_f = jnp.where(s_col == s_row, 1.0, 0.0)
        pair = (s_row // 2) == (s_col // 2)
        levels = []
        m = 4
        while m <= CHUNK:
            levels.append(((s_row // m) == (s_col // m)) & ((s_row // (m // 2)) != (s_col // (m // 2))))
            m *= 2
        end_row = CHUNK - 1 if dd == 0 else 0
        for c in range(tb // CHUNK):
            r = slice(c * CHUNK, (c + 1) * CHUNK)
            q, k, v, la, be = q_all[r], k_all[r], v_all[r], la_all[r], be_all[r]
            g_t = _sel_dot(m_incl, la)
            g_s = _sel_dot(ones_cc, la * m_before)
            gam = jnp.where(incl, jnp.exp(jnp.minimum(g_t - g_s, 0.0)), 0.0)
            k_bd = bd(k)
            kk = _dot_nt(k, k_bd)
            qk = _dot_nt(q, k_bd)
            a = jnp.where(strict, be * kk * gam, 0.0)
            eg = jnp.exp(g_t)
            g_end = g_t[end_row:end_row + 1, :]
            qh_ref[dd, r, :] = (q * eg).astype(BF16)
            qg_ref[dd, r, :] = (qk * gam).astype(BF16)
            kh_ref[dd, r, :] = (k * jnp.exp(g_end - g_t)).astype(BF16)
            ee_ref[dd, 0, c:c + 1, :] = jnp.exp(g_end)
            work.append((dd, r, a, bd(be * eg * k), bd(be * v), levels))
            t_invs.append(eye_f - jnp.where(pair, a, 0.0))
    for lv in range(int(math.log2(CHUNK)) - 1):
        xs = [_dot(jnp.where(wk[5][lv], wk[2], 0.0), bd(t)) for wk, t in zip(work, t_invs)]
        t_invs = [t - _dot(t, bd(x)) for t, x in zip(t_invs, xs)]
    for (dd, r, _, wk_bd, vb_bd, _), t in zip(work, t_invs):
        w_ref[dd, r, :] = _dot(t, wk_bd).astype(BF16)
        u0_ref[dd, r, :] = _dot(t, vb_bd)


def _gdn_prep(pb, pg, conv_w, alog_x, dtb_x, cos_t, sin_t, dims):
    n_all = pb.shape[0]
    n_lat, t_len, n_batch, d_a, d_b, d_c = dims
    n_lat_blk = t_len // TOK_BLK
    n_ctx_blk = (n_all - n_lat) // n_batch // TOK_BLK
    n_blk = n_all // TOK_BLK
    halo = 8
    per = TOK_BLK // halo
    kern = functools.partial(_gdn_prep_kernel, n_lat_blk=n_lat_blk, n_ctx_blk=n_ctx_blk,
                             n_lat_blks_total=n_lat // TOK_BLK, d_b=d_b)
    row = lambda i: (i, 0)
    const2 = lambda i: (0, 0)
    const3 = lambda i: (0, 0, 0)
    tab = lambda i: (jnp.where(i < n_lat // TOK_BLK, i % n_lat_blk, 0), 0)
    both = pl.BlockSpec((2, TOK_BLK, d_b), lambda i: (0, i, 0))
    sds = jax.ShapeDtypeStruct
    return pl.pallas_call(
        kern,
        name="gdn_prep",
        out_shape=(sds((2, n_all, d_b), BF16), sds((2, n_all, d_b), F32), sds((2, n_all, d_b), BF16),
                   sds((2, n_all, d_b), BF16), sds((2, n_all, d_b), BF16), sds((2, n_blk, 8, d_b), F32)),
        grid=(n_blk,),
        in_specs=[pl.BlockSpec((TOK_BLK, 3 * d_b), row),
                  pl.BlockSpec((halo, 3 * d_b), lambda i: (jnp.maximum(i * per - 1, 0), 0)),
                  pl.BlockSpec((halo, 3 * d_b), lambda i: (jnp.minimum((i + 1) * per, n_blk * per - 1), 0)),
                  pl.BlockSpec((TOK_BLK, LANES), row),
                  pl.BlockSpec((8, 3 * d_b), const2),
                  pl.BlockSpec((2, 1, d_b), const3),
                  pl.BlockSpec((2, 1, d_b), const3),
                  pl.BlockSpec((TOK_BLK, d_b), tab),
                  pl.BlockSpec((TOK_BLK, d_b), tab)],
        out_specs=(both, both, both, both, both, pl.BlockSpec((2, 1, 8, d_b), lambda i: (0, i, 0, 0))),
        compiler_params=_cparams(("parallel",)),
    )(pb, pb, pb, pg, conv_w, alog_x, dtb_x, cos_t, sin_t)


def _gdn_kernel(*refs, n_chunk):
    ins = (refs[0:6], refs[6:12])
    outs = refs[12:14]
    st_ref = refs[14]
    j = pl.program_id(1)
    n = outs[0].shape[-1]
    n_rep = n // HEAD_DIM

    @pl.when(j == 0)
    def _():
        st_ref[...] = jnp.zeros_like(st_ref)

    mask_bd = _head_mask(n)
    bd = lambda a: _block_diag(a, mask_bd, n_rep)
    sts = [st_ref[0], st_ref[1]]
    for c in range(n_chunk):
        for dd in range(2):
            w_ref, u0_ref, qh_ref, qg_ref, kh_ref, ee_ref = ins[dd]
            cc = c if dd == 0 else n_chunk - 1 - c
            r = slice(cc * CHUNK, (cc + 1) * CHUNK)
            st = sts[dd]
            u = u0_ref[0, r, :] - _dot_nt(w_ref[0, r, :], st)
            outs[dd][r, :] = _dot_nt(qh_ref[0, r, :], st) + _dot(qg_ref[0, r, :], bd(u))
            sts[dd] = st * ee_ref[0, 0, cc:cc + 1, :] + jnp.where(mask_bd, _dot_tn(u, kh_ref[0, r, :]), 0.0)
    st_ref[0] = sts[0]
    st_ref[1] = sts[1]


def _gdn_scan(wy, dims):
    n_all = wy[0].shape[1]
    n_lat, t_len, n_batch, d_a, d_b, d_c = dims
    n_lat_blk = t_len // TOK_BLK
    n_ctx_blk = (n_all - n_lat) // n_batch // TOK_BLK
    blk = functools.partial(_scan_block_index, n_lat_blk=n_lat_blk, n_ctx_blk=n_ctx_blk, n_batch=n_batch)
    kern = functools.partial(_gdn_kernel, n_chunk=TOK_BLK // CHUNK)
    in_specs = []
    for dd in range(2):
        tok = pl.BlockSpec((1, TOK_BLK, d_b), lambda b, j, dd=dd: (dd, blk(b, dd, j), 0))
        in_specs += [tok] * 5 + [pl.BlockSpec((1, 1, 8, d_b), lambda b, j, dd=dd: (dd, blk(b, dd, j), 0, 0))]
    out_specs = tuple(pl.BlockSpec((TOK_BLK, d_b), lambda b, j, dd=dd: (blk(b, dd, j), 0)) for dd in range(2))
    return pl.pallas_call(
        kern,
        name="gdn_scan",
        out_shape=(jax.ShapeDtypeStruct((n_all, d_b), F32), jax.ShapeDtypeStruct((n_all, d_b), F32)),
        grid=(n_batch, n_lat_blk + n_ctx_blk),
        in_specs=in_specs,
        out_specs=out_specs,
        scratch_shapes=[pltpu.VMEM((2, d_b, d_b), F32)],
        compiler_params=_cparams(("parallel", "arbitrary")),
    )(*wy, *wy)


def _stack_heads(q, n_rep):
    m, n = q.shape
    keep = (_iota((n_rep * m, n), 0) // m) == (_iota((n_rep * m, n), 1) // HEAD_DIM)
    return jnp.where(keep, jnp.concatenate([q] * n_rep, axis=0), jnp.zeros((), q.dtype))


def _fold_heads(o, m, n_rep):
    n = o.shape[1]
    lane_h = _iota((m, n), 1) // HEAD_DIM
    acc = jnp.zeros((m, n), F32)
    for h in range(n_rep):
        acc = acc + jnp.where(lane_h == h, o[h * m:(h + 1) * m, :], 0.0)
    return acc


def _natten_kernel(*refs, n_rows):
    q_ref, k_ref, v_ref, kc_ref, vc_ref = refs[:5]
    bias_refs = refs[5:-1]
    o_ref = refs[-1]
    step = pl.program_id(1)
    n = q_ref.shape[-1]
    kr = min(WIN_R, n_rows)
    dn = (((1,), (1,)), ((), ()))
    gw = NA_HEADS_PER_GROUP * HEAD_DIM
    n_stack = NA_HEADS_PER_GROUP * GRID_W
    units = []
    for rr, bias_ref in enumerate(bias_refs):
        r = step * len(bias_refs) + rr
        r0 = jnp.clip(r - WIN_R // 2, 0, n_rows - kr)
        sl = pl.ds(pl.multiple_of(r0 * GRID_W, GRID_W), kr * GRID_W)
        q_rows = slice(rr * GRID_W, (rr + 1) * GRID_W)
        for g in range(n // gw):
            lanes = slice(g * gw, (g + 1) * gw)
            qs = _stack_heads(q_ref[q_rows, lanes], NA_HEADS_PER_GROUP)
            s_loc = lax.dot_general(qs, k_ref[sl, lanes], dn, preferred_element_type=F32)
            s_loc = s_loc + bias_ref[0, g * n_stack:(g + 1) * n_stack, :]
            s_ctx = lax.dot_general(qs, kc_ref[:, lanes], dn, preferred_element_type=F32)
            units.append((q_rows, lanes, sl, s_loc, s_ctx))
    probs = []
    for q_rows, lanes, sl, s_loc, s_ctx in units:
        m = jnp.maximum(jnp.max(s_loc, axis=-1, keepdims=True), jnp.max(s_ctx, axis=-1, keepdims=True))
        p_loc = jnp.exp(s_loc - m)
        p_ctx = jnp.exp(s_ctx - m)
        inv = 1.0 / (jnp.sum(p_loc, axis=-1, keepdims=True) + jnp.sum(p_ctx, axis=-1, keepdims=True))
        probs.append(((p_loc * inv).astype(BF16), (p_ctx * inv).astype(BF16)))
    for (q_rows, lanes, sl, _, _), (p_loc, p_ctx) in zip(units, probs):
        o = jnp.dot(p_loc, v_ref[sl, lanes], preferred_element_type=F32)
        o = o + jnp.dot(p_ctx, vc_ref[:, lanes], preferred_element_type=F32)
        o_ref[q_rows, lanes] = _fold_heads(o, GRID_W, NA_HEADS_PER_GROUP)


def _natten(qn, kn, vv, bias_tab, dims):
    n_lat, t_len, n_batch, d_a, d_b, d_c = dims
    n_all = qn.shape[0]
    l_ctx = (n_all - n_lat) // n_batch
    n_rows = t_len // GRID_W
    kr = min(WIN_R, n_rows)
    ctx0 = n_lat // l_ctx
    rb = NA_ROWS_PER_STEP
    assert n_rows % rb == 0
    n_steps = n_rows // rb

    def cfg(rr):
        def index(b, i):
            r = i * rb + rr
            r0 = jnp.clip(r - WIN_R // 2, 0, n_rows - kr)
            return (r - r0, 0, 0)
        return index

    kern = functools.partial(_natten_kernel, n_rows=n_rows)
    return pl.pallas_call(
        kern,
        name="natten",
        out_shape=jax.ShapeDtypeStruct((n_lat, d_c), F32),
        grid=(n_batch, n_steps),
        in_specs=[pl.BlockSpec((rb * GRID_W, d_c), lambda b, i: (b * n_steps + i, 0)),
                  pl.BlockSpec((t_len, d_c), lambda b, i: (b, 0)),
                  pl.BlockSpec((t_len, d_c), lambda b, i: (b, 0)),
                  pl.BlockSpec((l_ctx, d_c), lambda b, i: (ctx0 + b, 0)),
                  pl.BlockSpec((l_ctx, d_c), lambda b, i: (ctx0 + b, 0))]
                 + [pl.BlockSpec((1,) + bias_tab.shape[1:], cfg(rr)) for rr in range(rb)],
        out_specs=pl.BlockSpec((rb * GRID_W, d_c), lambda b, i: (b * n_steps + i, 0)),
        compiler_params=_cparams(("parallel", "arbitrary")),
    )(qn, kn, vv, kn, vv, *([bias_tab] * rb))


def _ctx_attn_kernel(q_ref, k_ref, v_ref, o_ref):
    n = q_ref.shape[-1]
    n_rep = n // HEAD_DIM
    qs = _stack_heads(q_ref[...], n_rep)
    s = lax.dot_general(qs, k_ref[...], (((1,), (1,)), ((), ())), preferred_element_type=F32)
    p = jnp.exp(s - jnp.max(s, axis=-1, keepdims=True))
    p = p * (1.0 / jnp.sum(p, axis=-1, keepdims=True))
    o = jnp.dot(p.astype(BF16), v_ref[...], preferred_element_type=F32)
    o_ref[...] = _fold_heads(o, q_ref.shape[0], n_rep)


def _ctx_attn(qn, kn, vv, dims):
    n_lat, t_len, n_batch, d_a, d_b, d_c = dims
    n_all = qn.shape[0]
    l_ctx = (n_all - n_lat) // n_batch
    tq = 64
    per = l_ctx // tq
    q0 = n_lat // tq
    c0 = n_lat // l_ctx
    return pl.pallas_call(
        _ctx_attn_kernel,
        name="ctx_attn",
        out_shape=jax.ShapeDtypeStruct((n_all - n_lat, d_c), F32),
        grid=(n_batch, per),
        in_specs=[pl.BlockSpec((tq, d_c), lambda b, i: (q0 + b * per + i, 0)),
                  pl.BlockSpec((l_ctx, d_c), lambda b, i: (c0 + b, 0)),
                  pl.BlockSpec((l_ctx, d_c), lambda b, i: (c0 + b, 0))],
        out_specs=pl.BlockSpec((tq, d_c), lambda b, i: (b * per + i, 0)),
        compiler_params=_cparams(("parallel", "arbitrary")),
    )(qn, kn, vv)


def _natten_bias(rpb, n_rows):
    n_h = rpb.shape[0]
    kr = min(WIN_R, n_rows)
    cols = jnp.arange(GRID_W)
    c0 = jnp.clip(cols - WIN_C // 2, 0, GRID_W - WIN_C)
    kc = jnp.arange(GRID_W)
    in_win = (kc[None, :] >= c0[:, None]) & (kc[None, :] < c0[:, None] + WIN_C)
    per = 2 * GRID_W
    rp = rpb.astype(F32)
    u = jnp.concatenate([rp[..., WIN_C - 1:], jnp.zeros(rp.shape[:-1] + (per - 2 * WIN_C + 1,), F32),
                         rp[..., :WIN_C - 1]], axis=-1)
    rel = jnp.tile(u, (1, 1, GRID_W))[..., :GRID_W * (per - 1)]
    rel = rel.reshape(rp.shape[:-1] + (GRID_W, per - 1))[..., :GRID_W]
    toe = jnp.where(in_win[None, None], rel, NEG_BIG)
    tabs = jnp.stack([toe[:, WIN_R - 1 - delta:WIN_R - 1 - delta + kr] for delta in range(kr)])
    return tabs.transpose(0, 1, 3, 2, 4).reshape(kr, n_h * GRID_W, kr * GRID_W)


def _outproj_kernel(x_ref, m_ref, oaf_ref, oab_ref, ga_ref, obf_ref, obb_ref, gb_ref, oc_ref, na_ref, nb_ref, w_ref,
                    o_ref, *, d_a, d_b):
    oa = oaf_ref[...] + oab_ref[...]
    ya = oa * lax.rsqrt(_group_mean_sq(oa) + EPS) * na_ref[...] * _silu(ga_ref[...])
    ob = obf_ref[...] + obb_ref[...]
    yb = ob * lax.rsqrt(_group_mean_sq(ob) + EPS) * nb_ref[...] * _silu(gb_ref[...])
    acc = _dot(ya, w_ref[0:d_a, :])
    acc = acc + _dot(yb, w_ref[d_a:d_a + d_b, :])
    acc = acc + _dot(oc_ref[...], w_ref[d_a + d_b:, :])
    o_ref[...] = x_ref[...] + m_ref[0, 2:3, :] * acc


def _outproj(xall, mod, oaf, oab, pa, obf, obb, pb, oc, na_g, nb_g, w_out, n_rows_out, dims):
    n_lat, t_len, n_batch, d_a, d_b, d_c = dims
    d = xall.shape[1]
    tm = ROW_TILE
    row = lambda i: (i, 0)
    const = lambda i: (0, 0)
    grp = lambda i: (_mod_group(i, tm, n_lat, t_len, n_batch), 0, 0)
    kern = functools.partial(_outproj_kernel, d_a=d_a, d_b=d_b)
    return pl.pallas_call(
        kern,
        name="outproj",
        out_shape=jax.ShapeDtypeStruct((n_rows_out, d), F32),
        grid=(n_rows_out // tm,),
        in_specs=[pl.BlockSpec((tm, d), row),
                  pl.BlockSpec((1, 6, d), grp),
                  pl.BlockSpec((tm, d_a), row),
                  pl.BlockSpec((tm, d_a), row),
                  pl.BlockSpec((tm, d_a), lambda i: (i, 4)),
                  pl.BlockSpec((tm, d_b), row),
                  pl.BlockSpec((tm, d_b), row),
                  pl.BlockSpec((tm, d_b), lambda i: (i, 3)),
                  pl.BlockSpec((tm, d_c), row),
                  pl.BlockSpec((1, d_a), const),
                  pl.BlockSpec((1, d_b), const),
                  pl.BlockSpec(w_out.shape, const)],
        out_specs=pl.BlockSpec((tm, d), row),
        compiler_params=_cparams(("parallel",)),
    )(xall, mod, oaf, oab, pa, obf, obb, pb, oc, na_g, nb_g, w_out)


def _ffn_kernel(x_ref, m_ref, g_ref, w1_ref, w3_ref, w2_ref, o_ref, h_ref, acc_ref):
    j = pl.program_id(1)

    @pl.when(j == 0)
    def _():
        h_ref[...] = _modulated_norm(x_ref[...], g_ref[...], m_ref[0, 3:4, :], m_ref[0, 4:5, :]).astype(BF16)
        acc_ref[...] = jnp.zeros_like(acc_ref)

    h = h_ref[...]
    a = jnp.dot(h, w1_ref[...], preferred_element_type=F32)
    b = jnp.dot(h, w3_ref[...], preferred_element_type=F32)
    acc_ref[...] += _dot(_silu(a) * b, w2_ref[...])

    @pl.when(j == pl.num_programs(1) - 1)
    def _():
        o_ref[...] = x_ref[...] + m_ref[0, 5:6, :] * acc_ref[...]


def _ffn(xall, mod, norm_g, w1, w3, w2, dims):
    n_lat, t_len, n_batch, d_a, d_b, d_c = dims
    n_rows, d = xall.shape
    d_ff = w1.shape[1]
    tm, tf = FFN_ROWS, FF_TILE
    assert t_len % tm == 0
    grp = lambda i, j: (_mod_group(i, tm, n_lat, t_len, n_batch), 0, 0)
    return pl.pallas_call(
        _ffn_kernel,
        name="ffn",
        out_shape=jax.ShapeDtypeStruct((n_rows, d), F32),
        grid=(pl.cdiv(n_rows, tm), d_ff // tf),
        in_specs=[pl.BlockSpec((tm, d), lambda i, j: (i, 0)),
                  pl.BlockSpec((1, 6, d), grp),
                  pl.BlockSpec((1, d), lambda i, j: (0, 0)),
                  pl.BlockSpec((d, tf), lambda i, j: (0, j)),
                  pl.BlockSpec((d, tf), lambda i, j: (0, j)),
                  pl.BlockSpec((tf, d), lambda i, j: (j, 0))],
        out_specs=pl.BlockSpec((tm, d), lambda i, j: (i, 0)),
        scratch_shapes=[pltpu.VMEM((tm, d), BF16), pltpu.VMEM((tm, d), F32)],
        compiler_params=_cparams(("parallel", "arbitrary")),
    )(xall, mod, norm_g, w1, w3, w2)


def _rows_to_tiles(x):
    r, d = x.shape
    slabs = jnp.stack([x[:, s * LANES:(s + 1) * LANES].reshape(r // SUBLANES, SUBLANES, LANES)
                       for s in range(d // LANES)], axis=1)
    return jnp.swapaxes(slabs, 1, 2).reshape(r, d // LANES, LANES)


def _tiles_to_slabs(x3):
    r, n_tile, _ = x3.shape
    y = jnp.swapaxes(x3.reshape(r // SUBLANES, SUBLANES, n_tile, LANES), 1, 2)
    return [y[:, s].reshape(r, LANES) for s in range(n_tile)]


def _route_kernel(x_ref, m_ref, g_ref, wr_ref, br_ref, h_ref, e_ref, gt_ref):
    h = _modulated_norm(x_ref[...], g_ref[...], m_ref[0, 3:4, :], m_ref[0, 4:5, :])
    h_ref[...] = _rows_to_tiles(h)
    lane = _iota((h.shape[0], LANES), 1)
    logits = jnp.where(lane < N_EXPERTS, _dot_hi(h, wr_ref[...]) + br_ref[...], -jnp.inf)
    m1 = jnp.max(logits, axis=-1, keepdims=True)
    lane_f = lane.astype(F32)
    i1 = jnp.min(jnp.where(logits == m1, lane_f, float(LANES)), axis=-1, keepdims=True).astype(jnp.int32)
    rest = jnp.where(lane == i1, -jnp.inf, logits)
    m2 = jnp.max(rest, axis=-1, keepdims=True)
    i2 = jnp.min(jnp.where(rest == m2, lane_f, float(LANES)), axis=-1, keepdims=True).astype(jnp.int32)
    e2 = jnp.exp(m2 - m1)
    g1 = 1.0 / (1.0 + e2)
    g2 = e2 / (1.0 + e2)
    e_ref[...] = jnp.where(lane == 0, i1, jnp.where(lane == 1, i2, 0))
    gt_ref[...] = jnp.where(lane == 0, g1, jnp.where(lane == 1, g2, 0.0))


def _route(x, mod, norm_g, wr_pad, br_pad, dims):
    n_lat, t_len, n_batch, d_a, d_b, d_c = dims
    n, d = x.shape
    tm = ROW_TILE
    row = lambda i: (i, 0)
    const = lambda i: (0, 0)
    grp = lambda i: (_mod_group(i, tm, n_lat, t_len, n_batch), 0, 0)
    return pl.pallas_call(
        _route_kernel,
        name="moe_route",
        out_shape=(jax.ShapeDtypeStruct((n, d // LANES, LANES), F32), jax.ShapeDtypeStruct((n, LANES), jnp.int32),
                   jax.ShapeDtypeStruct((n, LANES), F32)),
        grid=(n // tm,),
        in_specs=[pl.BlockSpec((tm, d), row), pl.BlockSpec((1, 6, d), grp), pl.BlockSpec((1, d), const),
                  pl.BlockSpec((d, LANES), const), pl.BlockSpec((1, LANES), const)],
        out_specs=(pl.BlockSpec((tm, d // LANES, LANES), lambda i: (i, 0, 0)), pl.BlockSpec((tm, LANES), row),
                   pl.BlockSpec((tm, LANES), row)),
        compiler_params=_cparams(("parallel",)),
    )(x, mod, norm_g, wr_pad, br_pad)


def _expert_kernel(be_ref, nv_ref, idx0_ref, idxn_ref, dstp_ref, dstc_ref, h_hbm, w1_ref, w3_ref, w2_ref, y_hbm,
                   xbuf_ref, hb_ref, acc_ref, out_ref, sem_in, sem_out, *, rows_per_step, n_pairs):
    i = pl.program_id(0)
    j = pl.program_id(1)
    n_blk = pl.num_programs(0)
    n_ff = pl.num_programs(1)
    bm = hb_ref.shape[0]
    rps = rows_per_step
    n_issue = xbuf_ref.shape[1]
    n_tile = xbuf_ref.shape[2]
    live = i < nv_ref[0]
    cur = i % 2
    nxt = 1 - cur

    def in_copy(tok, slot, r):
        return pltpu.make_async_copy(h_hbm.at[pl.ds(tok, 1)], xbuf_ref.at[slot, pl.ds(r, 1)], sem_in)

    def out_copy(slot, r, dst):
        return pltpu.make_async_copy(out_ref.at[slot, pl.ds(r, 1)], y_hbm.at[pl.ds(dst, 1)], sem_out)

    def wait_in(slot):
        for c in range(n_issue // rps):
            pltpu.make_async_copy(h_hbm.at[pl.ds(0, rps)], xbuf_ref.at[slot, pl.ds(c * rps, rps)], sem_in).wait()

    def wait_out(slot):
        for c in range(n_issue // rps):
            pltpu.make_async_copy(out_ref.at[slot, pl.ds(c * rps, rps)], y_hbm.at[pl.ds(0, rps)], sem_out).wait()

    @pl.when((i == 0) & (j == 0))
    def _():
        out_ref[...] = jnp.zeros_like(out_ref)

        def start(g, carry):
            for u in range(GATHER_PARTS):
                r = g * GATHER_PARTS + u
                in_copy(idx0_ref[0, 0, r], 0, r).start()
            return carry

        lax.fori_loop(0, n_issue // GATHER_PARTS, start, 0)

    @pl.when(j == 0)
    def _():
        wait_in(cur)

    @pl.when(live & (j == 0))
    def _():
        for s, slab in enumerate(_tiles_to_slabs(xbuf_ref[cur, 0:bm])):
            hb_ref[:, s * LANES:(s + 1) * LANES] = slab.astype(BF16)
        acc_ref[...] = jnp.zeros_like(acc_ref)

    def issue(part):
        per = rps // GATHER_PARTS
        for t in range(part * per, (part + 1) * per):
            r = j * rps + t
            in_copy(idxn_ref[0, 0, r], nxt, r).start(priority=0)
            dst = jnp.where(i == 0, n_pairs + r, dstp_ref[0, 0, r])
            out_copy(nxt, r, dst).start(priority=1)

    @pl.when(live)
    def _():
        h = hb_ref[...]
        a = _dot(h, w1_ref[0])
        issue(0)
        b = _dot(h, w3_ref[0])
        issue(1)
        g = (_silu(a) * b).astype(BF16)
        issue(2)
        acc_ref[...] += jnp.dot(g, w2_ref[0].astype(BF16), preferred_element_type=F32)
        issue(3)

    @pl.when(jnp.logical_not(live))
    def _():
        for part in range(GATHER_PARTS):
            issue(part)

    last = j == n_ff - 1

    @pl.when(last)
    def _():
        wait_out(nxt)

    @pl.when(live & last)
    def _():
        out_ref[cur, 0:bm] = _rows_to_tiles(acc_ref[...])

    @pl.when((i == n_blk - 1) & last)
    def _():
        wait_in(nxt)

        def start(g, carry):
            for u in range(GATHER_PARTS):
                r = g * GATHER_PARTS + u
                out_copy(cur, r, dstc_ref[0, 0, r]).start()
            return carry

        lax.fori_loop(0, n_issue // GATHER_PARTS, start, 0)
        wait_out(cur)


def _experts(h, slot_tok, slot_pair, block_e, n_live, w1, w3, w2):
    n, n_tile, _ = h.shape
    d = n_tile * LANES
    d_ff = w1.shape[2]
    bm, tf = MOE_ROWS, FF_TILE
    n_blk = slot_tok.shape[0] // bm
    n_ff = d_ff // tf
    n_pairs = 2 * n
    rps = -(-bm // (n_ff * SUBLANES)) * SUBLANES
    assert rps % GATHER_PARTS == 0
    n_issue = rps * n_ff
    extra = n_issue - bm
    idx = jnp.pad(slot_tok.reshape(n_blk, 1, bm), ((0, 0), (0, 0), (0, extra)))
    dump = n_pairs + jnp.arange(n_issue, dtype=jnp.int32)
    dst = jnp.concatenate([slot_pair.reshape(n_blk, 1, bm), jnp.broadcast_to(dump[bm:], (n_blk, 1, extra))], axis=2)
    smem = lambda f: pl.BlockSpec((1, 1, n_issue), f, memory_space=pltpu.SMEM)
    grid_spec = pltpu.PrefetchScalarGridSpec(
        num_scalar_prefetch=2,
        grid=(n_blk, n_ff),
        in_specs=[smem(lambda i, j, be, nv: (0, 0, 0)),
                  smem(lambda i, j, be, nv: (jnp.minimum(i + 1, n_blk - 1), 0, 0)),
                  smem(lambda i, j, be, nv: (jnp.maximum(i - 1, 0), 0, 0)),
                  smem(lambda i, j, be, nv: (i, 0, 0)),
                  pl.BlockSpec(memory_space=pl.ANY),
                  pl.BlockSpec((1, d, tf), lambda i, j, be, nv: (be[i], 0, j)),
                  pl.BlockSpec((1, d, tf), lambda i, j, be, nv: (be[i], 0, j)),
                  pl.BlockSpec((1, tf, d), lambda i, j, be, nv: (be[i], j, 0))],
        out_specs=pl.BlockSpec(memory_space=pl.ANY),
        scratch_shapes=[pltpu.VMEM((2, n_issue, n_tile, LANES), F32), pltpu.VMEM((bm, d), BF16),
                        pltpu.VMEM((bm, d), F32), pltpu.VMEM((2, n_issue, n_tile, LANES), F32),
                        pltpu.SemaphoreType.DMA(()), pltpu.SemaphoreType.DMA(())])
    return pl.pallas_call(
        functools.partial(_expert_kernel, rows_per_step=rps, n_pairs=n_pairs),
        name="moe_experts",
        out_shape=jax.ShapeDtypeStruct((n_pairs + n_issue, n_tile, LANES), F32),
        grid_spec=grid_spec,
        compiler_params=_cparams(("arbitrary", "arbitrary")),
    )(block_e, n_live, idx, idx, dst, dst, h, w1, w3, w2)


def _combine_kernel(x_ref, m_ref, gt_ref, y0_ref, y1_ref, o_ref):
    gt = gt_ref[...]
    for s, (y0, y1) in enumerate(zip(_tiles_to_slabs(y0_ref[...]), _tiles_to_slabs(y1_ref[...]))):
        cols = slice(s * LANES, (s + 1) * LANES)
        y = gt[:, 0:1] * y0 + gt[:, 1:2] * y1
        o_ref[:, cols] = x_ref[:, cols] + m_ref[0, 5:6, cols] * y


def _combine(x, mod, gates, y, dims):
    n_lat, t_len, n_batch, d_a, d_b, d_c = dims
    n, d = x.shape
    tm = ROW_TILE
    n_tile = y.shape[1]
    grp = lambda i: (_mod_group(i, tm, n_lat, t_len, n_batch), 0, 0)
    return pl.pallas_call(
        _combine_kernel,
        name="moe_combine",
        out_shape=jax.ShapeDtypeStruct((n, d), F32),
        grid=(n // tm,),
        in_specs=[pl.BlockSpec((tm, d), lambda i: (i, 0)),
                  pl.BlockSpec((1, 6, d), grp),
                  pl.BlockSpec((tm, LANES), lambda i: (i, 0)),
                  pl.BlockSpec((tm, n_tile, LANES), lambda i: (i, 0, 0)),
                  pl.BlockSpec((tm, n_tile, LANES), lambda i: (i + n // tm, 0, 0))],
        out_specs=pl.BlockSpec((tm, d), lambda i: (i, 0)),
        compiler_params=_cparams(("parallel",)),
    )(x, mod, gates, y, y)


def _moe(x, mod, norm_g, w_router, b_router, w1, w3, w2, dims):
    n, d = x.shape
    wr_pad = jnp.zeros((d, LANES), F32).at[:, :N_EXPERTS].set(w_router.astype(F32))
    br_pad = jnp.zeros((1, LANES), F32).at[0, :N_EXPERTS].set(b_router.astype(F32))
    h, e_tile, g_tile = _route(x, mod, norm_g, wr_pad, br_pad, dims)
    bm = MOE_ROWS
    e_flat = e_tile[:, :2].reshape(-1)
    onehot = (e_flat[:, None] == jnp.arange(N_EXPERTS, dtype=jnp.int32)[None, :]).astype(jnp.int32)
    csum = jnp.cumsum(onehot, axis=0)
    counts = csum[-1]
    rank = jnp.sum(csum * onehot, axis=1) - 1
    padded = (counts + bm - 1) // bm * bm
    pad_end = jnp.cumsum(padded)
    pad_start = pad_end - padded
    slot = pad_start[e_flat] + rank
    n_blocks = (2 * n) // bm + N_EXPERTS
    pair_flat = jnp.arange(2 * n, dtype=jnp.int32)
    slot_ids = jnp.arange(n_blocks * bm, dtype=jnp.int32)
    slot_pair = (2 * n + slot_ids % bm).at[slot].set((pair_flat % 2) * n + pair_flat // 2)
    slot_tok = jnp.where(slot_pair < 2 * n, slot_pair % n, slot_ids % n)
    blk_start = jnp.arange(n_blocks, dtype=jnp.int32) * bm
    block_e = jnp.minimum(jnp.sum((pad_end[None, :] <= blk_start[:, None]).astype(jnp.int32), axis=1),
                          N_EXPERTS - 1).astype(jnp.int32)
    n_live = (pad_end[-1] // bm).astype(jnp.int32).reshape(1)
    y = _experts(h, slot_tok, slot_pair, block_e, n_live, w1, w3, w2)
    return _combine(x, mod, g_tile, y, dims)


def _rope_tables(t_len, n_heads):
    t = jnp.arange(t_len)
    row = (t // GRID_W).astype(F32)
    col = (t % GRID_W).astype(F32)
    n_freq = HEAD_DIM // 4
    inv = ROPE_BASE ** (-jnp.arange(n_freq, dtype=F32) / n_freq)
    ang = jnp.concatenate([row[:, None] * inv, col[:, None] * inv], axis=-1)
    cos, sin = jnp.cos(ang), jnp.sin(ang)
    cos_h = jnp.concatenate([cos, cos], axis=-1)
    sin_h = jnp.concatenate([-sin, sin], axis=-1)
    return jnp.tile(cos_h, (1, n_heads)), jnp.tile(sin_h, (1, n_heads))


def kernel(x, c, ctx, c_ctx, ada_w, ada_b, norm1_g, norm2_g, w_in, w_out, hgrn_lb_raw, hgrn_onorm_g,
           gdn_conv_w, gdn_a_log, gdn_dt_bias, gdn_onorm_g, na_qnorm_g, na_knorm_g, na_rpb, ffn_w1, ffn_w3,
           ffn_w2, moe_router_w, moe_router_b, moe_w1, moe_w3, moe_w2):
    n_batch, t_len, d = x.shape
    l_ctx = ctx.shape[1]
    depth = w_in.shape[0]
    d_a = hgrn_lb_raw.shape[-1]
    d_b = gdn_conv_w.shape[-1] // 3
    n_hb = gdn_a_log.shape[-1]
    n_hc = na_rpb.shape[1]
    d_c = n_hc * HEAD_DIM
    n_lat = n_batch * t_len
    n_all = n_lat + n_batch * l_ctx
    dims = (n_lat, t_len, n_batch, d_a, d_b, d_c)
    assert d_a % HEAD_DIM == 0 and d_b == n_hb * HEAD_DIM
    assert t_len % ROW_TILE == 0 and (n_batch * l_ctx) % ROW_TILE == 0 and l_ctx % TOK_BLK == 0
    assert t_len % GRID_W == 0 and 4 * n_hb <= LANES

    cvec = jnp.zeros((8, d), F32).at[:n_batch].set(c.astype(F32)).at[n_batch].set(c_ctx.astype(F32))
    mod_all = _adaln(cvec, ada_w, ada_b)

    lb_soft = jax.nn.softmax(hgrn_lb_raw.astype(F32), axis=1)
    lower_bound = jnp.cumsum(lb_soft, axis=1) - lb_soft[:, :1]
    cos_t, sin_t = _rope_tables(t_len, n_hb)
    n_gate = 4 * n_hb
    sizes_a, sizes_b = 5 * d_a, 4 * d_b

    xall = jnp.concatenate([x.reshape(n_lat, d), ctx.reshape(n_batch * l_ctx, d)], axis=0).astype(F32)
    for l in range(depth):
        last = l == depth - 1
        mod = mod_all[l, :n_batch + 1].reshape(n_batch + 1, 6, d)
        w = w_in[l]
        gate_cols = jnp.pad(w[:, sizes_a + sizes_b:sizes_a + sizes_b + n_gate], ((0, 0), (0, LANES - n_gate)))
        w_pad = jnp.concatenate([w[:, :sizes_a + sizes_b], gate_cols, w[:, sizes_a + sizes_b + n_gate:]],
                                axis=1).astype(BF16)
        qn_g = jnp.tile(na_qnorm_g[l].astype(F32), n_hc)[None]
        kn_g = jnp.tile(na_knorm_g[l].astype(F32), n_hc)[None]
        pa, pb, pg, qn, kn, vv = _inproj(xall, mod, norm1_g[l][None].astype(F32), w_pad, qn_g, kn_g, dims)

        oaf, oab = _hgrn_scan(pa, lower_bound[:, l][:, None, :], dims)
        conv_w = jnp.zeros((8, 3 * d_b), F32).at[:CONV_K].set(gdn_conv_w[l].astype(F32))
        alog_x = jnp.repeat(gdn_a_log[l].astype(F32), HEAD_DIM, axis=-1)[:, None, :]
        dtb_x = jnp.repeat(gdn_dt_bias[l].astype(F32), HEAD_DIM, axis=-1)[:, None, :]
        obf, obb = _gdn_scan(_gdn_prep(pb, pg, conv_w, alog_x, dtb_x, cos_t, sin_t, dims), dims)
        bias_tab = _natten_bias(na_rpb[l], t_len // GRID_W)
        oc = _natten(qn, kn, vv, bias_tab, dims)
        n_out = n_lat if last else n_all
        if not last:
            oc = jnp.concatenate([oc, _ctx_attn(qn, kn, vv, dims)], axis=0)
        na_g = jnp.tile(hgrn_onorm_g[l].astype(F32), d_a // HEAD_DIM)[None]
        nb_g = jnp.tile(gdn_onorm_g[l].astype(F32), n_hb)[None]
        xall_mid = _outproj(xall, mod, oaf, oab, pa, obf, obb, pb, oc, na_g, nb_g, w_out[l].astype(BF16), n_out,
                            dims)
        i = l // 2
        if l % 2 == 0:
            xall = _ffn(xall_mid, mod, norm2_g[l][None].astype(F32), ffn_w1[i].astype(BF16),
                        ffn_w3[i].astype(BF16), ffn_w2[i].astype(BF16), dims)
        else:
            xall = _moe(xall_mid, mod, norm2_g[l][None].astype(F32), moe_router_w[i], moe_router_b[i],
                        moe_w1[i], moe_w3[i], moe_w2[i], dims)
    return xall[:n_lat].reshape(n_batch, t_len, d).astype(x.dtype)
```

```python
import functools
import math

import jax
import jax.numpy as jnp
import numpy as np
from jax import lax
from jax.experimental import pallas as pl
from jax.experimental.pallas import tpu as pltpu

F32 = jnp.float32
BF16 = jnp.bfloat16
HI = lax.Precision.HIGHEST

EPS = 1e-6
HEAD_DIM = 64
CHUNK = 64
TOK_BLK = 256
GRID_W = 64
WIN_R = 8
WIN_C = 16
CONV_K = 5
ROPE_BASE = 10000.0
N_EXPERTS = 8
LANES = 128
SUBLANES = 8
ROW_TILE = 512
FFN_ROWS = 1024
FF_TILE = 512
MOE_ROWS = 1024
GATHER_PARTS = 4
NA_HEADS_PER_GROUP = 4
NA_ROWS_PER_STEP = 4
VMEM_LIMIT = 56 * 1024 * 1024
MXU_DEPTH = 256
NEG_BIG = -1e30


def _cparams(sem):
    return pltpu.CompilerParams(dimension_semantics=sem, vmem_limit_bytes=VMEM_LIMIT)


def _dot(a, b):
    return jnp.dot(a.astype(BF16), b.astype(BF16), preferred_element_type=F32)


def _dot_nt(a, b):
    return lax.dot_general(a.astype(BF16), b.astype(BF16), (((1,), (1,)), ((), ())),
                           preferred_element_type=F32)


def _dot_tn(a, b):
    return lax.dot_general(a.astype(BF16), b.astype(BF16), (((0,), (0,)), ((), ())),
                           preferred_element_type=F32)


def _dot_hi(a, b):
    return jnp.dot(a, b, precision=HI, preferred_element_type=F32)


def _iota(shape, dim):
    return lax.broadcasted_iota(jnp.int32, shape, dim)


def _head_mask(n):
    return (_iota((n, n), 0) // HEAD_DIM) == (_iota((n, n), 1) // HEAD_DIM)


def _head_sums(x, scale):
    n = x.shape[-1]
    w = min(n, MXU_DEPTH)
    gm = jnp.where(_head_mask(w), scale, 0.0).astype(BF16)
    hi = x.astype(BF16)
    lo = (x - hi.astype(F32)).astype(BF16)
    dot = lambda a: jnp.dot(a, gm, preferred_element_type=F32)
    parts = [dot(lo[:, s:s + w]) + dot(hi[:, s:s + w]) for s in range(0, n, w)]
    return parts[0] if len(parts) == 1 else jnp.concatenate(parts, axis=1)


def _group_mean_sq(x):
    return _head_sums(x * x, 1.0 / HEAD_DIM)


def _silu(x):
    return x * jax.nn.sigmoid(x)


def _modulated_norm(x, g, shift, scale):
    ms = jnp.mean(x * x, axis=-1, keepdims=True)
    y = x * lax.rsqrt(ms + EPS) * g
    return y * (1.0 + scale) + shift


def _mod_group(i, tile, n_lat, t_len, n_batch):
    return jnp.where(i * tile < n_lat, (i * tile) // t_len, n_batch)


def _adaln_kernel(c_ref, w_ref, b_ref, o_ref):
    s = _silu(c_ref[...])
    o_ref[0] = _dot(s, w_ref[0]) + b_ref[0]


def _adaln(cvec, ada_w, ada_b):
    depth, d, d6 = ada_w.shape
    tn = 1024
    return pl.pallas_call(
        _adaln_kernel,
        name="adaln",
        out_shape=jax.ShapeDtypeStruct((depth, 8, d6), F32),
        grid=(depth, d6 // tn),
        in_specs=[pl.BlockSpec((8, d), lambda l, j: (0, 0)),
                  pl.BlockSpec((1, d, tn), lambda l, j: (l, 0, j)),
                  pl.BlockSpec((1, 1, tn), lambda l, j: (l, 0, j))],
        out_specs=pl.BlockSpec((1, 8, tn), lambda l, j: (l, 0, j)),
        compiler_params=_cparams(("parallel", "parallel")),
    )(cvec, ada_w, ada_b.reshape(depth, 1, d6))


def _inproj_kernel(x_ref, m_ref, g_ref, w_ref, qg_ref, kg_ref,
                   pa_ref, pb_ref, pg_ref, qn_ref, kn_ref, vv_ref, *, d_a5, d_b4, d_c):
    h = _modulated_norm(x_ref[...], g_ref[...], m_ref[0, 0:1, :], m_ref[0, 1:2, :]).astype(BF16)
    o0 = 0
    pa_ref[...] = jnp.dot(h, w_ref[:, o0:o0 + d_a5], preferred_element_type=F32)
    o0 += d_a5
    pb_ref[...] = jnp.dot(h, w_ref[:, o0:o0 + d_b4], preferred_element_type=F32)
    o0 += d_b4
    pg_ref[...] = jnp.dot(h, w_ref[:, o0:o0 + LANES], preferred_element_type=F32)
    o0 += LANES
    q = jnp.dot(h, w_ref[:, o0:o0 + d_c], preferred_element_type=F32)
    q = q * lax.rsqrt(_group_mean_sq(q) + EPS) * qg_ref[...]
    qn_ref[...] = (q * HEAD_DIM ** -0.5).astype(BF16)
    o0 += d_c
    k = jnp.dot(h, w_ref[:, o0:o0 + d_c], preferred_element_type=F32)
    k = k * lax.rsqrt(_group_mean_sq(k) + EPS) * kg_ref[...]
    kn_ref[...] = k.astype(BF16)
    o0 += d_c
    vv_ref[...] = jnp.dot(h, w_ref[:, o0:o0 + d_c], preferred_element_type=F32).astype(BF16)


def _inproj(xall, mod, norm_g, w_pad, qn_g, kn_g, dims):
    n_all, d = xall.shape
    n_lat, t_len, n_batch, d_a, d_b, d_c = dims
    tm = ROW_TILE
    row = lambda i: (i, 0)
    const = lambda i: (0, 0)
    grp = lambda i: (_mod_group(i, tm, n_lat, t_len, n_batch), 0, 0)
    kern = functools.partial(_inproj_kernel, d_a5=5 * d_a, d_b4=4 * d_b, d_c=d_c)
    return pl.pallas_call(
        kern,
        name="inproj",
        out_shape=(jax.ShapeDtypeStruct((n_all, 5 * d_a), F32),
                   jax.ShapeDtypeStruct((n_all, 4 * d_b), F32),
                   jax.ShapeDtypeStruct((n_all, LANES), F32),
                   jax.ShapeDtypeStruct((n_all, d_c), BF16),
                   jax.ShapeDtypeStruct((n_all, d_c), BF16),
                   jax.ShapeDtypeStruct((n_all, d_c), BF16)),
        grid=(n_all // tm,),
        in_specs=[pl.BlockSpec((tm, d), row),
                  pl.BlockSpec((1, 6, d), grp),
                  pl.BlockSpec((1, d), const),
                  pl.BlockSpec(w_pad.shape, const),
                  pl.BlockSpec((1, d_c), const),
                  pl.BlockSpec((1, d_c), const)],
        out_specs=(pl.BlockSpec((tm, 5 * d_a), row), pl.BlockSpec((tm, 4 * d_b), row),
                   pl.BlockSpec((tm, LANES), row), pl.BlockSpec((tm, d_c), row),
                   pl.BlockSpec((tm, d_c), row), pl.BlockSpec((tm, d_c), row)),
        compiler_params=_cparams(("parallel",)),
    )(xall, mod, norm_g, w_pad, qn_g, kn_g)


def _scan_block_index(b, d, j, n_lat_blk, n_ctx_blk, n_batch):
    jc = jnp.where(d == 0, j, n_ctx_blk - 1 - j)
    jl = jnp.where(d == 0, j - n_ctx_blk, n_lat_blk - 1 - (j - n_ctx_blk))
    return jnp.where(j < n_ctx_blk, n_batch * n_lat_blk + b * n_ctx_blk + jc, b * n_lat_blk + jl)


def _block_diag(x, mask_bd, n_rep):
    xb = x.astype(BF16)
    return jnp.where(mask_bd, jnp.concatenate([xb] * n_rep, axis=0), jnp.zeros((), BF16))


def _scan_order(idx, d):
    return idx + d * (CHUNK - 1 - 2 * idx)


def _split3(x):
    hi = x.astype(BF16)
    r = x - hi.astype(F32)
    mid = r.astype(BF16)
    lo = (r - mid.astype(F32)).astype(BF16)
    return hi, mid, lo


def _sel_dot(sel, x):
    hi, mid, lo = _split3(x)
    dot = lambda a: jnp.dot(sel, a, preferred_element_type=F32)
    return (dot(lo) + dot(mid)) + dot(hi)


def _dot_sel(x, sel):
    hi, mid, lo = _split3(x)
    dot = lambda a: jnp.dot(a, sel, preferred_element_type=F32)
    return (dot(lo) + dot(mid)) + dot(hi)


HGRN_LEVELS = tuple(CHUNK >> (i + 1) for i in range(int(math.log2(CHUNK))))


def _hgrn_kernel(qf_ref, zf_ref, vf_ref, qb_ref, zb_ref, vb_ref, lb_ref, of_ref, ob_ref, st_ref, *, n_chunk):
    ins = ((qf_ref, zf_ref, vf_ref, of_ref), (qb_ref, zb_ref, vb_ref, ob_ref))
    j = pl.program_id(1)
    n = qf_ref.shape[-1]
    n_rep = n // HEAD_DIM

    @pl.when(j == 0)
    def _():
        st_ref[...] = jnp.zeros_like(st_ref)

    mask_bd = _head_mask(n)
    gones = mask_bd.astype(BF16)
    bd = lambda a: _block_diag(a, mask_bd, n_rep)

    work = []
    masks = []
    for dd in range(2):
        q_ref, z_ref, v_ref, _ = ins[dd]
        lb = lb_ref[dd]
        s_t = _scan_order(_iota((CHUNK, CHUNK), 0), dd)
        s_u = _scan_order(_iota((CHUNK, CHUNK), 1), dd)
        sets = [s_u <= s_t, s_u > s_t]
        for m in HGRN_LEVELS:
            ref = (s_t // (2 * m)) * (2 * m) + m - 1
            sets.append((s_u > jnp.minimum(s_t, ref)) & (s_u <= jnp.maximum(s_t, ref)))
        sel = jnp.concatenate([jnp.where(a, 1.0, 0.0).astype(BF16) for a in sets], axis=0)
        s_row = _scan_order(_iota((CHUNK, n), 0), dd)
        s_col = _scan_order(_iota((CHUNK, n), 1) % HEAD_DIM, dd)
        masks.append((s_row == s_col, [(s_row // m) % 2 == 1 for m in HGRN_LEVELS],
                      [(s_row // (2 * m)) == (s_col // (2 * m)) for m in HGRN_LEVELS]))
        for c in range(n_chunk):
            cc = c if dd == 0 else n_chunk - 1 - c
            r = slice(cc * CHUNK, (cc + 1) * CHUNK)
            q = q_ref[r, :] * HEAD_DIM ** -0.5
            z = z_ref[r, :]
            v = v_ref[r, :]
            log_f = (jnp.minimum(z, 0.0) - jnp.log1p(jnp.exp(-jnp.abs(z)))) + jnp.log1p(lb * jnp.exp(-z))
            k = (1.0 - lb) * jax.nn.sigmoid(-z)
            work.append((dd, r, q, k, v, _sel_dot(sel, log_f)))
    atts = [jnp.where(masks[dd][0], jnp.dot((q * k).astype(BF16), gones, preferred_element_type=F32), 0.0)
            for dd, _, q, k, _, _ in work]
    for lvl, m in enumerate(HGRN_LEVELS):
        for i, (dd, _, q, k, _, sums) in enumerate(work):
            second = masks[dd][1][lvl]
            e = jnp.exp(sums[(2 + lvl) * CHUNK:(3 + lvl) * CHUNK])
            a_m = _dot_nt(jnp.where(second, q * e, 0.0), bd(jnp.where(second, 0.0, k * e)))
            if 2 * m < CHUNK:
                a_m = jnp.where(masks[dd][2][lvl], a_m, 0.0)
            atts[i] = atts[i] + a_m
    pre = []
    for (dd, r, q, k, v, sums), att in zip(work, atts):
        b_incl = sums[0:CHUNK]
        b_after = sums[CHUNK:2 * CHUNK]
        e_end = jnp.exp(b_incl[0:1, :] + b_after[0:1, :])
        upd = jnp.where(mask_bd, _dot_tn(v, k * jnp.exp(b_after)), 0.0)
        pre.append((dd, r, _dot(att, bd(v)), (q * jnp.exp(b_incl)).astype(BF16), e_end, upd))

    sts = [st_ref[0], st_ref[1]]
    for c in range(n_chunk):
        for dd in range(2):
            _, r, o_intra, q_hat, e_end, upd = pre[dd * n_chunk + c]
            ins[dd][3][r, :] = o_intra + _dot_nt(q_hat, sts[dd])
            sts[dd] = sts[dd] * e_end + upd
    st_ref[0] = sts[0]
    st_ref[1] = sts[1]


def _hgrn_scan(pa, lb2, dims):
    n_all = pa.shape[0]
    n_lat, t_len, n_batch, d_a, d_b, d_c = dims
    n_lat_blk = t_len // TOK_BLK
    n_ctx_blk = (n_all - n_lat) // n_batch // TOK_BLK
    blk = functools.partial(_scan_block_index, n_lat_blk=n_lat_blk, n_ctx_blk=n_ctx_blk, n_batch=n_batch)
    kern = functools.partial(_hgrn_kernel, n_chunk=TOK_BLK // CHUNK)
    in_specs = []
    for dd in range(2):
        in_specs += [pl.BlockSpec((TOK_BLK, d_a), lambda b, j, dd=dd: (blk(b, dd, j), 0)),
                     pl.BlockSpec((TOK_BLK, d_a), lambda b, j, dd=dd: (blk(b, dd, j), 1 + dd)),
                     pl.BlockSpec((TOK_BLK, d_a), lambda b, j, dd=dd: (blk(b, dd, j), 3))]
    in_specs.append(pl.BlockSpec((2, 1, d_a), lambda b, j: (0, 0, 0)))
    out_specs = tuple(pl.BlockSpec((TOK_BLK, d_a), lambda b, j, dd=dd: (blk(b, dd, j), 0)) for dd in range(2))
    return pl.pallas_call(
        kern,
        name="hgrn_scan",
        out_shape=(jax.ShapeDtypeStruct((n_all, d_a), F32), jax.ShapeDtypeStruct((n_all, d_a), F32)),
        grid=(n_batch, n_lat_blk + n_ctx_blk),
        in_specs=in_specs,
        out_specs=out_specs,
        scratch_shapes=[pltpu.VMEM((2, d_a, d_a), F32)],
        compiler_params=_cparams(("parallel", "arbitrary")),
    )(pa, pa, pa, pa, pa, pa, lb2)


def _gdn_prep_kernel(x_ref, prev_ref, next_ref, pg_ref, cw_ref, alog_ref, dtb_ref, cos_ref, sin_ref,
                     w_ref, u0_ref, qh_ref, qg_ref, kh_ref, ee_ref,
                     *, n_lat_blk, n_ctx_blk, n_lat_blks_total, d_b):
    i = pl.program_id(0)
    is_lat = i < n_lat_blks_total
    pos = jnp.where(is_lat, i % n_lat_blk, (i - n_lat_blks_total) % n_ctx_blk)
    n_seq_blk = jnp.where(is_lat, n_lat_blk, n_ctx_blk)
    first = pos == 0
    last = pos == n_seq_blk - 1
    x = x_ref[...]
    halo = prev_ref.shape[0]
    prev = jnp.where(first, 0.0, prev_ref[...])
    nxt = jnp.where(last, 0.0, next_ref[...])
    rows = _iota(x.shape, 0)
    tb = x.shape[0]
    half = CONV_K // 2
    acc = x * cw_ref[half:half + 1, :]
    for s in range(1, half + 1):
        xs = pltpu.roll(x, s, 0)
        for r in range(s):
            xs = jnp.where(rows == r, prev[halo - s + r:halo - s + r + 1, :], xs)
        acc = acc + xs * cw_ref[half - s:half - s + 1, :]
        xs = pltpu.roll(x, tb - s, 0)
        for r in range(s):
            xs = jnp.where(rows == tb - s + r, nxt[r:r + 1, :], xs)
        acc = acc + xs * cw_ref[half + s:half + s + 1, :]
    y = _silu(acc)
    def l2n(a):
        return a * lax.rsqrt(_head_sums(a * a, 1.0) + EPS)

    lane = _iota((tb, d_b), 1) % HEAD_DIM
    lo = lane < HEAD_DIM // 2

    def rope(a):
        partner = jnp.where(lo, pltpu.roll(a, d_b - HEAD_DIM // 2, 1), pltpu.roll(a, HEAD_DIM // 2, 1))
        return jnp.where(is_lat, a * cos_ref[...] + partner * sin_ref[...], a)

    q_all = rope(l2n(y[:, 0:d_b])) * HEAD_DIM ** -0.5
    k_all = rope(l2n(y[:, d_b:2 * d_b]))
    v_all = y[:, 2 * d_b:3 * d_b]
    n_h = d_b // HEAD_DIM
    g = pg_ref[...]
    e_r = _iota((LANES, d_b), 0)
    e_c = _iota((LANES, d_b), 1) // HEAD_DIM
    mask_bd = _head_mask(d_b)
    bd = lambda a: _block_diag(a, mask_bd, n_h)
    ones_cc = jnp.ones((CHUNK, CHUNK), BF16)
    ee_ref[...] = jnp.zeros_like(ee_ref)
    work = []
    t_invs = []
    for dd in range(2):
        a_x = _dot_sel(g, jnp.where(e_r == dd * n_h + e_c, 1.0, 0.0).astype(BF16))
        b_x = _dot_sel(g, jnp.where(e_r == (2 + dd) * n_h + e_c, 1.0, 0.0).astype(BF16))
        t = a_x + dtb_ref[dd]
        softplus = jnp.maximum(t, 0.0) + jnp.log1p(jnp.exp(-jnp.abs(t)))
        la_all = -jnp.exp(alog_ref[dd]) * softplus
        be_all = jax.nn.sigmoid(b_x)
        s_t = _scan_order(_iota((CHUNK, CHUNK), 0), dd)
        s_u = _scan_order(_iota((CHUNK, CHUNK), 1), dd)
        m_incl = jnp.where(s_u <= s_t, 1.0, 0.0).astype(BF16)
        s_row = _scan_order(_iota((CHUNK, d_b), 0), dd)
        s_col = _scan_order(_iota((CHUNK, d_b), 1) % HEAD_DIM, dd)
        incl = s_col <= s_row
        strict = s_col < s_row
        m_before = jnp.where(s_row <= s_col, 1.0, 0.0)
        eye_f = jnp.where(s_col == s_row, 1.0, 0.0)
        pair = (s_row // 2) == (s_col // 2)
        levels = []
        m = 4
        while m <= CHUNK:
            levels.append(((s_row // m) == (s_col // m)) & ((s_row // (m // 2)) != (s_col // (m // 2))))
            m *= 2
        end_row = CHUNK - 1 if dd == 0 else 0
        for c in range(tb // CHUNK):
            r = slice(c * CHUNK, (c + 1) * CHUNK)
            q, k, v, la, be = q_all[r], k_all[r], v_all[r], la_all[r], be_all[r]
            g_t = _sel_dot(m_incl, la)
            g_s = _sel_dot(ones_cc, la * m_before)
            gam = jnp.where(incl, jnp.exp(jnp.minimum(g_t - g_s, 0.0)), 0.0)
            k_bd = bd(k)
            kk = _dot_nt(k, k_bd)
            qk = _dot_nt(q, k_bd)
            a = jnp.where(strict, be * kk * gam, 0.0)
            eg = jnp.exp(g_t)
            g_end = g_t[end_row:end_row + 1, :]
            qh_ref[dd, r, :] = (q * eg).astype(BF16)
            qg_ref[dd, r, :] = (qk * gam).astype(BF16)
            kh_ref[dd, r, :] = (k * jnp.exp(g_end - g_t)).astype(BF16)
            ee_ref[dd, 0, c:c + 1, :] = jnp.exp(g_end)
            work.append((dd, r, a, bd(be * eg * k), bd(be * v), levels))
            t_invs.append(eye_f - jnp.where(pair, a, 0.0))
    for lv in range(int(math.log2(CHUNK)) - 1):
        xs = [_dot(jnp.where(wk[5][lv], wk[2], 0.0), bd(t)) for wk, t in zip(work, t_invs)]
        t_invs = [t - _dot(t, bd(x)) for t, x in zip(t_invs, xs)]
    for (dd, r, _, wk_bd, vb_bd, _), t in zip(work, t_invs):
        w_ref[dd, r, :] = _dot(t, wk_bd).astype(BF16)
        u0_ref[dd, r, :] = _dot(t, vb_bd)


def _gdn_prep(pb, pg, conv_w, alog_x, dtb_x, cos_t, sin_t, dims):
    n_all = pb.shape[0]
    n_lat, t_len, n_batch, d_a, d_b, d_c = dims
    n_lat_blk = t_len // TOK_BLK
    n_ctx_blk = (n_all - n_lat) // n_batch // TOK_BLK
    n_blk = n_all // TOK_BLK
    halo = 8
    per = TOK_BLK // halo
    kern = functools.partial(_gdn_prep_kernel, n_lat_blk=n_lat_blk, n_ctx_blk=n_ctx_blk,
                             n_lat_blks_total=n_lat // TOK_BLK, d_b=d_b)
    row = lambda i: (i, 0)
    const2 = lambda i: (0, 0)
    const3 = lambda i: (0, 0, 0)
    tab = lambda i: (jnp.where(i < n_lat // TOK_BLK, i % n_lat_blk, 0), 0)
    both = pl.BlockSpec((2, TOK_BLK, d_b), lambda i: (0, i, 0))
    sds = jax.ShapeDtypeStruct
    return pl.pallas_call(
        kern,
        name="gdn_prep",
        out_shape=(sds((2, n_all, d_b), BF16), sds((2, n_all, d_b), F32), sds((2, n_all, d_b), BF16),
                   sds((2, n_all, d_b), BF16), sds((2, n_all, d_b), BF16), sds((2, n_blk, 8, d_b), F32)),
        grid=(n_blk,),
        in_specs=[pl.BlockSpec((TOK_BLK, 3 * d_b), row),
                  pl.BlockSpec((halo, 3 * d_b), lambda i: (jnp.maximum(i * per - 1, 0), 0)),
                  pl.BlockSpec((halo, 3 * d_b), lambda i: (jnp.minimum((i + 1) * per, n_blk * per - 1), 0)),
                  pl.BlockSpec((TOK_BLK, LANES), row),
                  pl.BlockSpec((8, 3 * d_b), const2),
                  pl.BlockSpec((2, 1, d_b), const3),
                  pl.BlockSpec((2, 1, d_b), const3),
                  pl.BlockSpec((TOK_BLK, d_b), tab),
                  pl.BlockSpec((TOK_BLK, d_b), tab)],
        out_specs=(both, both, both, both, both, pl.BlockSpec((2, 1, 8, d_b), lambda i: (0, i, 0, 0))),
        compiler_params=_cparams(("parallel",)),
    )(pb, pb, pb, pg, conv_w, alog_x, dtb_x, cos_t, sin_t)


def _gdn_kernel(*refs, n_chunk):
    ins = (refs[0:6], refs[6:12])
    outs = refs[12:14]
    st_ref = refs[14]
    j = pl.program_id(1)
    n = outs[0].shape[-1]
    n_rep = n // HEAD_DIM

    @pl.when(j == 0)
    def _():
        st_ref[...] = jnp.zeros_like(st_ref)

    mask_bd = _head_mask(n)
    bd = lambda a: _block_diag(a, mask_bd, n_rep)
    sts = [st_ref[0], st_ref[1]]
    for c in range(n_chunk):
        for dd in range(2):
            w_ref, u0_ref, qh_ref, qg_ref, kh_ref, ee_ref = ins[dd]
            cc = c if dd == 0 else n_chunk - 1 - c
            r = slice(cc * CHUNK, (cc + 1) * CHUNK)
            st = sts[dd]
            u = u0_ref[0, r, :] - _dot_nt(w_ref[0, r, :], st)
            outs[dd][r, :] = _dot_nt(qh_ref[0, r, :], st) + _dot(qg_ref[0, r, :], bd(u))
            sts[dd] = st * ee_ref[0, 0, cc:cc + 1, :] + jnp.where(mask_bd, _dot_tn(u, kh_ref[0, r, :]), 0.0)
    st_ref[0] = sts[0]
    st_ref[1] = sts[1]


def _gdn_scan(wy, dims):
    n_all = wy[0].shape[1]
    n_lat, t_len, n_batch, d_a, d_b, d_c = dims
    n_lat_blk = t_len // TOK_BLK
    n_ctx_blk = (n_all - n_lat) // n_batch // TOK_BLK
    blk = functools.partial(_scan_block_index, n_lat_blk=n_lat_blk, n_ctx_blk=n_ctx_blk, n_batch=n_batch)
    kern = functools.partial(_gdn_kernel, n_chunk=TOK_BLK // CHUNK)
    in_specs = []
    for dd in range(2):
        tok = pl.BlockSpec((1, TOK_BLK, d_b), lambda b, j, dd=dd: (dd, blk(b, dd, j), 0))
        in_specs += [tok] * 5 + [pl.BlockSpec((1, 1, 8, d_b), lambda b, j, dd=dd: (dd, blk(b, dd, j), 0, 0))]
    out_specs = tuple(pl.BlockSpec((TOK_BLK, d_b), lambda b, j, dd=dd: (blk(b, dd, j), 0)) for dd in range(2))
    return pl.pallas_call(
        kern,
        name="gdn_scan",
        out_shape=(jax.ShapeDtypeStruct((n_all, d_b), F32), jax.ShapeDtypeStruct((n_all, d_b), F32)),
        grid=(n_batch, n_lat_blk + n_ctx_blk),
        in_specs=in_specs,
        out_specs=out_specs,
        scratch_shapes=[pltpu.VMEM((2, d_b, d_b), F32)],
        compiler_params=_cparams(("parallel", "arbitrary")),
    )(*wy, *wy)


def _stack_heads(q, n_rep):
    m, n = q.shape
    keep = (_iota((n_rep * m, n), 0) // m) == (_iota((n_rep * m, n), 1) // HEAD_DIM)
    return jnp.where(keep, jnp.concatenate([q] * n_rep, axis=0), jnp.zeros((), q.dtype))


def _fold_heads(o, m, n_rep):
    n = o.shape[1]
    lane_h = _iota((m, n), 1) // HEAD_DIM
    acc = jnp.zeros((m, n), F32)
    for h in range(n_rep):
        acc = acc + jnp.where(lane_h == h, o[h * m:(h + 1) * m, :], 0.0)
    return acc


def _natten_kernel(*refs, n_rows):
    q_ref, k_ref, v_ref, kc_ref, vc_ref = refs[:5]
    bias_refs = refs[5:-1]
    o_ref = refs[-1]
    step = pl.program_id(1)
    n = q_ref.shape[-1]
    kr = min(WIN_R, n_rows)
    dn = (((1,), (1,)), ((), ()))
    gw = NA_HEADS_PER_GROUP * HEAD_DIM
    n_stack = NA_HEADS_PER_GROUP * GRID_W
    units = []
    for rr, bias_ref in enumerate(bias_refs):
        r = step * len(bias_refs) + rr
        r0 = jnp.clip(r - WIN_R // 2, 0, n_rows - kr)
        sl = pl.ds(pl.multiple_of(r0 * GRID_W, GRID_W), kr * GRID_W)
        q_rows = slice(rr * GRID_W, (rr + 1) * GRID_W)
        for g in range(n // gw):
            lanes = slice(g * gw, (g + 1) * gw)
            qs = _stack_heads(q_ref[q_rows, lanes], NA_HEADS_PER_GROUP)
            s_loc = lax.dot_general(qs, k_ref[sl, lanes], dn, preferred_element_type=F32)
            s_loc = s_loc + bias_ref[0, g * n_stack:(g + 1) * n_stack, :]
            s_ctx = lax.dot_general(qs, kc_ref[:, lanes], dn, preferred_element_type=F32)
            units.append((q_rows, lanes, sl, s_loc, s_ctx))
    probs = []
    for q_rows, lanes, sl, s_loc, s_ctx in units:
        m = jnp.maximum(jnp.max(s_loc, axis=-1, keepdims=True), jnp.max(s_ctx, axis=-1, keepdims=True))
        p_loc = jnp.exp(s_loc - m)
        p_ctx = jnp.exp(s_ctx - m)
        inv = 1.0 / (jnp.sum(p_loc, axis=-1, keepdims=True) + jnp.sum(p_ctx, axis=-1, keepdims=True))
        probs.append(((p_loc * inv).astype(BF16), (p_ctx * inv).astype(BF16)))
    for (q_rows, lanes, sl, _, _), (p_loc, p_ctx) in zip(units, probs):
        o = jnp.dot(p_loc, v_ref[sl, lanes], preferred_element_type=F32)
        o = o + jnp.dot(p_ctx, vc_ref[:, lanes], preferred_element_type=F32)
        o_ref[q_rows, lanes] = _fold_heads(o, GRID_W, NA_HEADS_PER_GROUP)


def _natten(qn, kn, vv, bias_tab, dims):
    n_lat, t_len, n_batch, d_a, d_b, d_c = dims
    n_all = qn.shape[0]
    l_ctx = (n_all - n_lat) // n_batch
    n_rows = t_len // GRID_W
    kr = min(WIN_R, n_rows)
    ctx0 = n_lat // l_ctx
    rb = NA_ROWS_PER_STEP
    assert n_rows % rb == 0
    n_steps = n_rows // rb

    def cfg(rr):
        def index(b, i):
            r = i * rb + rr
            r0 = jnp.clip(r - WIN_R // 2, 0, n_rows - kr)
            return (r - r0, 0, 0)
        return index

    kern = functools.partial(_natten_kernel, n_rows=n_rows)
    return pl.pallas_call(
        kern,
        name="natten",
        out_shape=jax.ShapeDtypeStruct((n_lat, d_c), F32),
        grid=(n_batch, n_steps),
        in_specs=[pl.BlockSpec((rb * GRID_W, d_c), lambda b, i: (b * n_steps + i, 0)),
                  pl.BlockSpec((t_len, d_c), lambda b, i: (b, 0)),
                  pl.BlockSpec((t_len, d_c), lambda b, i: (b, 0)),
                  pl.BlockSpec((l_ctx, d_c), lambda b, i: (ctx0 + b, 0)),
                  pl.BlockSpec((l_ctx, d_c), lambda b, i: (ctx0 + b, 0))]
                 + [pl.BlockSpec((1,) + bias_tab.shape[1:], cfg(rr)) for rr in range(rb)],
        out_specs=pl.BlockSpec((rb * GRID_W, d_c), lambda b, i: (b * n_steps + i, 0)),
        compiler_params=_cparams(("parallel", "arbitrary")),
    )(qn, kn, vv, kn, vv, *([bias_tab] * rb))


def _ctx_attn_kernel(q_ref, k_ref, v_ref, o_ref):
    n = q_ref.shape[-1]
    n_rep = n // HEAD_DIM
    qs = _stack_heads(q_ref[...], n_rep)
    s = lax.dot_general(qs, k_ref[...], (((1,), (1,)), ((), ())), preferred_element_type=F32)
    p = jnp.exp(s - jnp.max(s, axis=-1, keepdims=True))
    p = p * (1.0 / jnp.sum(p, axis=-1, keepdims=True))
    o = jnp.dot(p.astype(BF16), v_ref[...], preferred_element_type=F32)
    o_ref[...] = _fold_heads(o, q_ref.shape[0], n_rep)


def _ctx_attn(qn, kn, vv, dims):
    n_lat, t_len, n_batch, d_a, d_b, d_c = dims
    n_all = qn.shape[0]
    l_ctx = (n_all - n_lat) // n_batch
    tq = 64
    per = l_ctx // tq
    q0 = n_lat // tq
    c0 = n_lat // l_ctx
    return pl.pallas_call(
        _ctx_attn_kernel,
        name="ctx_attn",
        out_shape=jax.ShapeDtypeStruct((n_all - n_lat, d_c), F32),
        grid=(n_batch, per),
        in_specs=[pl.BlockSpec((tq, d_c), lambda b, i: (q0 + b * per + i, 0)),
                  pl.BlockSpec((l_ctx, d_c), lambda b, i: (c0 + b, 0)),
                  pl.BlockSpec((l_ctx, d_c), lambda b, i: (c0 + b, 0))],
        out_specs=pl.BlockSpec((tq, d_c), lambda b, i: (b * per + i, 0)),
        compiler_params=_cparams(("parallel", "arbitrary")),
    )(qn, kn, vv)


def _natten_bias(rpb, n_rows):
    n_h = rpb.shape[0]
    kr = min(WIN_R, n_rows)
    cols = jnp.arange(GRID_W)
    c0 = jnp.clip(cols - WIN_C // 2, 0, GRID_W - WIN_C)
    kc = jnp.arange(GRID_W)
    in_win = (kc[None, :] >= c0[:, None]) & (kc[None, :] < c0[:, None] + WIN_C)
    per = 2 * GRID_W
    rp = rpb.astype(F32)
    u = jnp.concatenate([rp[..., WIN_C - 1:], jnp.zeros(rp.shape[:-1] + (per - 2 * WIN_C + 1,), F32),
                         rp[..., :WIN_C - 1]], axis=-1)
    rel = jnp.tile(u, (1, 1, GRID_W))[..., :GRID_W * (per - 1)]
    rel = rel.reshape(rp.shape[:-1] + (GRID_W, per - 1))[..., :GRID_W]
    toe = jnp.where(in_win[None, None], rel, NEG_BIG)
    tabs = jnp.stack([toe[:, WIN_R - 1 - delta:WIN_R - 1 - delta + kr] for delta in range(kr)])
    return tabs.transpose(0, 1, 3, 2, 4).reshape(kr, n_h * GRID_W, kr * GRID_W)


def _outproj_kernel(x_ref, m_ref, oaf_ref, oab_ref, ga_ref, obf_ref, obb_ref, gb_ref, oc_ref, na_ref, nb_ref, w_ref,
                    o_ref, *, d_a, d_b):
    oa = oaf_ref[...] + oab_ref[...]
    ya = oa * lax.rsqrt(_group_mean_sq(oa) + EPS) * na_ref[...] * _silu(ga_ref[...])
    ob = obf_ref[...] + obb_ref[...]
    yb = ob * lax.rsqrt(_group_mean_sq(ob) + EPS) * nb_ref[...] * _silu(gb_ref[...])
    acc = _dot(ya, w_ref[0:d_a, :])
    acc = acc + _dot(yb, w_ref[d_a:d_a + d_b, :])
    acc = acc + _dot(oc_ref[...], w_ref[d_a + d_b:, :])
    o_ref[...] = x_ref[...] + m_ref[0, 2:3, :] * acc


def _outproj(xall, mod, oaf, oab, pa, obf, obb, pb, oc, na_g, nb_g, w_out, n_rows_out, dims):
    n_lat, t_len, n_batch, d_a, d_b, d_c = dims
    d = xall.shape[1]
    tm = ROW_TILE
    row = lambda i: (i, 0)
    const = lambda i: (0, 0)
    grp = lambda i: (_mod_group(i, tm, n_lat, t_len, n_batch), 0, 0)
    kern = functools.partial(_outproj_kernel, d_a=d_a, d_b=d_b)
    return pl.pallas_call(
        kern,
        name="outproj",
        out_shape=jax.ShapeDtypeStruct((n_rows_out, d), F32),
        grid=(n_rows_out // tm,),
        in_specs=[pl.BlockSpec((tm, d), row),
                  pl.BlockSpec((1, 6, d), grp),
                  pl.BlockSpec((tm, d_a), row),
                  pl.BlockSpec((tm, d_a), row),
                  pl.BlockSpec((tm, d_a), lambda i: (i, 4)),
                  pl.BlockSpec((tm, d_b), row),
                  pl.BlockSpec((tm, d_b), row),
                  pl.BlockSpec((tm, d_b), lambda i: (i, 3)),
                  pl.BlockSpec((tm, d_c), row),
                  pl.BlockSpec((1, d_a), const),
                  pl.BlockSpec((1, d_b), const),
                  pl.BlockSpec(w_out.shape, const)],
        out_specs=pl.BlockSpec((tm, d), row),
        compiler_params=_cparams(("parallel",)),
    )(xall, mod, oaf, oab, pa, obf, obb, pb, oc, na_g, nb_g, w_out)


def _ffn_kernel(x_ref, m_ref, g_ref, w1_ref, w3_ref, w2_ref, o_ref, h_ref, acc_ref):
    j = pl.program_id(1)

    @pl.when(j == 0)
    def _():
        h_ref[...] = _modulated_norm(x_ref[...], g_ref[...], m_ref[0, 3:4, :], m_ref[0, 4:5, :]).astype(BF16)
        acc_ref[...] = jnp.zeros_like(acc_ref)

    h = h_ref[...]
    a = jnp.dot(h, w1_ref[...], preferred_element_type=F32)
    b = jnp.dot(h, w3_ref[...], preferred_element_type=F32)
    acc_ref[...] += _dot(_silu(a) * b, w2_ref[...])

    @pl.when(j == pl.num_programs(1) - 1)
    def _():
        o_ref[...] = x_ref[...] + m_ref[0, 5:6, :] * acc_ref[...]


def _ffn(xall, mod, norm_g, w1, w3, w2, dims):
    n_lat, t_len, n_batch, d_a, d_b, d_c = dims
    n_rows, d = xall.shape
    d_ff = w1.shape[1]
    tm, tf = FFN_ROWS, FF_TILE
    assert t_len % tm == 0
    grp = lambda i, j: (_mod_group(i, tm, n_lat, t_len, n_batch), 0, 0)
    return pl.pallas_call(
        _ffn_kernel,
        name="ffn",
        out_shape=jax.ShapeDtypeStruct((n_rows, d), F32),
        grid=(pl.cdiv(n_rows, tm), d_ff // tf),
        in_specs=[pl.BlockSpec((tm, d), lambda i, j: (i, 0)),
                  pl.BlockSpec((1, 6, d), grp),
                  pl.BlockSpec((1, d), lambda i, j: (0, 0)),
                  pl.BlockSpec((d, tf), lambda i, j: (0, j)),
                  pl.BlockSpec((d, tf), lambda i, j: (0, j)),
                  pl.BlockSpec((tf, d), lambda i, j: (j, 0))],
        out_specs=pl.BlockSpec((tm, d), lambda i, j: (i, 0)),
        scratch_shapes=[pltpu.VMEM((tm, d), BF16), pltpu.VMEM((tm, d), F32)],
        compiler_params=_cparams(("parallel", "arbitrary")),
    )(xall, mod, norm_g, w1, w3, w2)


def _rows_to_tiles(x):
    r, d = x.shape
    slabs = jnp.stack([x[:, s * LANES:(s + 1) * LANES].reshape(r // SUBLANES, SUBLANES, LANES)
                       for s in range(d // LANES)], axis=1)
    return jnp.swapaxes(slabs, 1, 2).reshape(r, d // LANES, LANES)


def _tiles_to_slabs(x3):
    r, n_tile, _ = x3.shape
    y = jnp.swapaxes(x3.reshape(r // SUBLANES, SUBLANES, n_tile, LANES), 1, 2)
    return [y[:, s].reshape(r, LANES) for s in range(n_tile)]


def _route_kernel(x_ref, m_ref, g_ref, wr_ref, br_ref, h_ref, e_ref, gt_ref):
    h = _modulated_norm(x_ref[...], g_ref[...], m_ref[0, 3:4, :], m_ref[0, 4:5, :])
    h_ref[...] = _rows_to_tiles(h)
    lane = _iota((h.shape[0], LANES), 1)
    logits = jnp.where(lane < N_EXPERTS, _dot_hi(h, wr_ref[...]) + br_ref[...], -jnp.inf)
    m1 = jnp.max(logits, axis=-1, keepdims=True)
    lane_f = lane.astype(F32)
    i1 = jnp.min(jnp.where(logits == m1, lane_f, float(LANES)), axis=-1, keepdims=True).astype(jnp.int32)
    rest = jnp.where(lane == i1, -jnp.inf, logits)
    m2 = jnp.max(rest, axis=-1, keepdims=True)
    i2 = jnp.min(jnp.where(rest == m2, lane_f, float(LANES)), axis=-1, keepdims=True).astype(jnp.int32)
    e2 = jnp.exp(m2 - m1)
    g1 = 1.0 / (1.0 + e2)
    g2 = e2 / (1.0 + e2)
    e_ref[...] = jnp.where(lane == 0, i1, jnp.where(lane == 1, i2, 0))
    gt_ref[...] = jnp.where(lane == 0, g1, jnp.where(lane == 1, g2, 0.0))


def _route(x, mod, norm_g, wr_pad, br_pad, dims):
    n_lat, t_len, n_batch, d_a, d_b, d_c = dims
    n, d = x.shape
    tm = ROW_TILE
    row = lambda i: (i, 0)
    const = lambda i: (0, 0)
    grp = lambda i: (_mod_group(i, tm, n_lat, t_len, n_batch), 0, 0)
    return pl.pallas_call(
        _route_kernel,
        name="moe_route",
        out_shape=(jax.ShapeDtypeStruct((n, d // LANES, LANES), F32), jax.ShapeDtypeStruct((n, LANES), jnp.int32),
                   jax.ShapeDtypeStruct((n, LANES), F32)),
        grid=(n // tm,),
        in_specs=[pl.BlockSpec((tm, d), row), pl.BlockSpec((1, 6, d), grp), pl.BlockSpec((1, d), const),
                  pl.BlockSpec((d, LANES), const), pl.BlockSpec((1, LANES), const)],
        out_specs=(pl.BlockSpec((tm, d // LANES, LANES), lambda i: (i, 0, 0)), pl.BlockSpec((tm, LANES), row),
                   pl.BlockSpec((tm, LANES), row)),
        compiler_params=_cparams(("parallel",)),
    )(x, mod, norm_g, wr_pad, br_pad)


def _expert_kernel(be_ref, nv_ref, idx0_ref, idxn_ref, dstp_ref, dstc_ref, h_hbm, w1_ref, w3_ref, w2_ref, y_hbm,
                   xbuf_ref, hb_ref, acc_ref, out_ref, sem_in, sem_out, *, rows_per_step, n_pairs):
    i = pl.program_id(0)
    j = pl.program_id(1)
    n_blk = pl.num_programs(0)
    n_ff = pl.num_programs(1)
    bm = hb_ref.shape[0]
    rps = rows_per_step
    n_issue = xbuf_ref.shape[1]
    n_tile = xbuf_ref.shape[2]
    live = i < nv_ref[0]
    cur = i % 2
    nxt = 1 - cur

    def in_copy(tok, slot, r):
        return pltpu.make_async_copy(h_hbm.at[pl.ds(tok, 1)], xbuf_ref.at[slot, pl.ds(r, 1)], sem_in)

    def out_copy(slot, r, dst):
        return pltpu.make_async_copy(out_ref.at[slot, pl.ds(r, 1)], y_hbm.at[pl.ds(dst, 1)], sem_out)

    def wait_in(slot):
        for c in range(n_issue // rps):
            pltpu.make_async_copy(h_hbm.at[pl.ds(0, rps)], xbuf_ref.at[slot, pl.ds(c * rps, rps)], sem_in).wait()

    def wait_out(slot):
        for c in range(n_issue // rps):
            pltpu.make_async_copy(out_ref.at[slot, pl.ds(c * rps, rps)], y_hbm.at[pl.ds(0, rps)], sem_out).wait()

    @pl.when((i == 0) & (j == 0))
    def _():
        out_ref[...] = jnp.zeros_like(out_ref)

        def start(g, carry):
            for u in range(GATHER_PARTS):
                r = g * GATHER_PARTS + u
                in_copy(idx0_ref[0, 0, r], 0, r).start()
            return carry

        lax.fori_loop(0, n_issue // GATHER_PARTS, start, 0)

    @pl.when(j == 0)
    def _():
        wait_in(cur)

    @pl.when(live & (j == 0))
    def _():
        for s, slab in enumerate(_tiles_to_slabs(xbuf_ref[cur, 0:bm])):
            hb_ref[:, s * LANES:(s + 1) * LANES] = slab.astype(BF16)
        acc_ref[...] = jnp.zeros_like(acc_ref)

    def issue(part):
        per = rps // GATHER_PARTS
        for t in range(part * per, (part + 1) * per):
            r = j * rps + t
            in_copy(idxn_ref[0, 0, r], nxt, r).start(priority=0)
            dst = jnp.where(i == 0, n_pairs + r, dstp_ref[0, 0, r])
            out_copy(nxt, r, dst).start(priority=1)

    def compute(with_issue):
        h = hb_ref[...]
        a = _dot(h, w1_ref[0])
        if with_issue:
            issue(0)
        b = _dot(h, w3_ref[0])
        if with_issue:
            issue(1)
        g = (_silu(a) * b).astype(BF16)
        if with_issue:
            issue(2)
        acc_ref[...] += jnp.dot(g, w2_ref[0].astype(BF16), preferred_element_type=F32)
        if with_issue:
            issue(3)

    last = j == n_ff - 1
    not_last = jnp.logical_not(last)

    @pl.when(live & not_last)
    def _():
        compute(True)

    @pl.when(live & last)
    def _():
        compute(False)

    @pl.when(jnp.logical_not(live) & not_last)
    def _():
        for part in range(GATHER_PARTS):
            issue(part)

    @pl.when(last)
    def _():
        wait_out(nxt)

    @pl.when(live & last)
    def _():
        out_ref[cur, 0:bm] = _rows_to_tiles(acc_ref[...])

    @pl.when((i == n_blk - 1) & last)
    def _():
        wait_in(nxt)

        def start(g, carry):
            for u in range(GATHER_PARTS):
                r = g * GATHER_PARTS + u
                out_copy(cur, r, dstc_ref[0, 0, r]).start()
            return carry

        lax.fori_loop(0, n_issue // GATHER_PARTS, start, 0)
        wait_out(cur)


def _experts(h, slot_tok, slot_pair, block_e, n_live, w1, w3, w2):
    n, n_tile, _ = h.shape
    d = n_tile * LANES
    d_ff = w1.shape[2]
    bm, tf = MOE_ROWS, FF_TILE
    n_blk = slot_tok.shape[0] // bm
    n_ff = d_ff // tf
    n_pairs = 2 * n
    rps = -(-bm // ((n_ff - 1) * SUBLANES)) * SUBLANES
    assert rps % GATHER_PARTS == 0
    n_issue = rps * (n_ff - 1)
    extra = n_issue - bm
    idx = jnp.pad(slot_tok.reshape(n_blk, 1, bm), ((0, 0), (0, 0), (0, extra)))
    dump = n_pairs + jnp.arange(n_issue, dtype=jnp.int32)
    dst = jnp.concatenate([slot_pair.reshape(n_blk, 1, bm), jnp.broadcast_to(dump[bm:], (n_blk, 1, extra))], axis=2)
    smem = lambda f: pl.BlockSpec((1, 1, n_issue), f, memory_space=pltpu.SMEM)
    grid_spec = pltpu.PrefetchScalarGridSpec(
        num_scalar_prefetch=2,
        grid=(n_blk, n_ff),
        in_specs=[smem(lambda i, j, be, nv: (0, 0, 0)),
                  smem(lambda i, j, be, nv: (jnp.minimum(i + 1, n_blk - 1), 0, 0)),
                  smem(lambda i, j, be, nv: (jnp.maximum(i - 1, 0), 0, 0)),
                  smem(lambda i, j, be, nv: (i, 0, 0)),
                  pl.BlockSpec(memory_space=pl.ANY),
                  pl.BlockSpec((1, d, tf), lambda i, j, be, nv: (be[i], 0, j)),
                  pl.BlockSpec((1, d, tf), lambda i, j, be, nv: (be[i], 0, j)),
                  pl.BlockSpec((1, tf, d), lambda i, j, be, nv: (be[i], j, 0))],
        out_specs=pl.BlockSpec(memory_space=pl.ANY),
        scratch_shapes=[pltpu.VMEM((2, n_issue, n_tile, LANES), F32), pltpu.VMEM((bm, d), BF16),
                        pltpu.VMEM((bm, d), F32), pltpu.VMEM((2, n_issue, n_tile, LANES), F32),
                        pltpu.SemaphoreType.DMA(()), pltpu.SemaphoreType.DMA(())])
    return pl.pallas_call(
        functools.partial(_expert_kernel, rows_per_step=rps, n_pairs=n_pairs),
        name="moe_experts",
        out_shape=jax.ShapeDtypeStruct((n_pairs + n_issue, n_tile, LANES), F32),
        grid_spec=grid_spec,
        compiler_params=_cparams(("arbitrary", "arbitrary")),
    )(block_e, n_live, idx, idx, dst, dst, h, w1, w3, w2)


def _combine_kernel(x_ref, m_ref, gt_ref, y0_ref, y1_ref, o_ref):
    gt = gt_ref[...]
    for s, (y0, y1) in enumerate(zip(_tiles_to_slabs(y0_ref[...]), _tiles_to_slabs(y1_ref[...]))):
        cols = slice(s * LANES, (s + 1) * LANES)
        y = gt[:, 0:1] * y0 + gt[:, 1:2] * y1
        o_ref[:, cols] = x_ref[:, cols] + m_ref[0, 5:6, cols] * y


def _combine(x, mod, gates, y, dims):
    n_lat, t_len, n_batch, d_a, d_b, d_c = dims
    n, d = x.shape
    tm = ROW_TILE
    n_tile = y.shape[1]
    grp = lambda i: (_mod_group(i, tm, n_lat, t_len, n_batch), 0, 0)
    return pl.pallas_call(
        _combine_kernel,
        name="moe_combine",
        out_shape=jax.ShapeDtypeStruct((n, d), F32),
        grid=(n // tm,),
        in_specs=[pl.BlockSpec((tm, d), lambda i: (i, 0)),
                  pl.BlockSpec((1, 6, d), grp),
                  pl.BlockSpec((tm, LANES), lambda i: (i, 0)),
                  pl.BlockSpec((tm, n_tile, LANES), lambda i: (i, 0, 0)),
                  pl.BlockSpec((tm, n_tile, LANES), lambda i: (i + n // tm, 0, 0))],
        out_specs=pl.BlockSpec((tm, d), lambda i: (i, 0)),
        compiler_params=_cparams(("parallel",)),
    )(x, mod, gates, y, y)


def _moe(x, mod, norm_g, w_router, b_router, w1, w3, w2, dims):
    n, d = x.shape
    wr_pad = jnp.zeros((d, LANES), F32).at[:, :N_EXPERTS].set(w_router.astype(F32))
    br_pad = jnp.zeros((1, LANES), F32).at[0, :N_EXPERTS].set(b_router.astype(F32))
    h, e_tile, g_tile = _route(x, mod, norm_g, wr_pad, br_pad, dims)
    bm = MOE_ROWS
    e_flat = e_tile[:, :2].reshape(-1)
    onehot = (e_flat[:, None] == jnp.arange(N_EXPERTS, dtype=jnp.int32)[None, :]).astype(jnp.int32)
    csum = jnp.cumsum(onehot, axis=0)
    counts = csum[-1]
    rank = jnp.sum(csum * onehot, axis=1) - 1
    padded = (counts + bm - 1) // bm * bm
    pad_end = jnp.cumsum(padded)
    pad_start = pad_end - padded
    slot = pad_start[e_flat] + rank
    n_blocks = (2 * n) // bm + N_EXPERTS
    pair_flat = jnp.arange(2 * n, dtype=jnp.int32)
    slot_ids = jnp.arange(n_blocks * bm, dtype=jnp.int32)
    slot_pair = (2 * n + slot_ids % bm).at[slot].set((pair_flat % 2) * n + pair_flat // 2)
    slot_tok = jnp.where(slot_pair < 2 * n, slot_pair % n, slot_ids % n)
    blk_start = jnp.arange(n_blocks, dtype=jnp.int32) * bm
    block_e = jnp.minimum(jnp.sum((pad_end[None, :] <= blk_start[:, None]).astype(jnp.int32), axis=1),
                          N_EXPERTS - 1).astype(jnp.int32)
    n_live = (pad_end[-1] // bm).astype(jnp.int32).reshape(1)
    y = _experts(h, slot_tok, slot_pair, block_e, n_live, w1, w3, w2)
    return _combine(x, mod, g_tile, y, dims)


def _rope_tables(t_len, n_heads):
    t = jnp.arange(t_len)
    row = (t // GRID_W).astype(F32)
    col = (t % GRID_W).astype(F32)
    n_freq = HEAD_DIM // 4
    inv = ROPE_BASE ** (-jnp.arange(n_freq, dtype=F32) / n_freq)
    ang = jnp.concatenate([row[:, None] * inv, col[:, None] * inv], axis=-1)
    cos, sin = jnp.cos(ang), jnp.sin(ang)
    cos_h = jnp.concatenate([cos, cos], axis=-1)
    sin_h = jnp.concatenate([-sin, sin], axis=-1)
    return jnp.tile(cos_h, (1, n_heads)), jnp.tile(sin_h, (1, n_heads))


def kernel(x, c, ctx, c_ctx, ada_w, ada_b, norm1_g, norm2_g, w_in, w_out, hgrn_lb_raw, hgrn_onorm_g,
           gdn_conv_w, gdn_a_log, gdn_dt_bias, gdn_onorm_g, na_qnorm_g, na_knorm_g, na_rpb, ffn_w1, ffn_w3,
           ffn_w2, moe_router_w, moe_router_b, moe_w1, moe_w3, moe_w2):
    n_batch, t_len, d = x.shape
    l_ctx = ctx.shape[1]
    depth = w_in.shape[0]
    d_a = hgrn_lb_raw.shape[-1]
    d_b = gdn_conv_w.shape[-1] // 3
    n_hb = gdn_a_log.shape[-1]
    n_hc = na_rpb.shape[1]
    d_c = n_hc * HEAD_DIM
    n_lat = n_batch * t_len
    n_all = n_lat + n_batch * l_ctx
    dims = (n_lat, t_len, n_batch, d_a, d_b, d_c)
    assert d_a % HEAD_DIM == 0 and d_b == n_hb * HEAD_DIM
    assert t_len % ROW_TILE == 0 and (n_batch * l_ctx) % ROW_TILE == 0 and l_ctx % TOK_BLK == 0
    assert t_len % GRID_W == 0 and 4 * n_hb <= LANES

    cvec = jnp.zeros((8, d), F32).at[:n_batch].set(c.astype(F32)).at[n_batch].set(c_ctx.astype(F32))
    mod_all = _adaln(cvec, ada_w, ada_b)

    lb_soft = jax.nn.softmax(hgrn_lb_raw.astype(F32), axis=1)
    lower_bound = jnp.cumsum(lb_soft, axis=1) - lb_soft[:, :1]
    cos_t, sin_t = _rope_tables(t_len, n_hb)
    n_gate = 4 * n_hb
    sizes_a, sizes_b = 5 * d_a, 4 * d_b

    xall = jnp.concatenate([x.reshape(n_lat, d), ctx.reshape(n_batch * l_ctx, d)], axis=0).astype(F32)
    for l in range(depth):
        last = l == depth - 1
        mod = mod_all[l, :n_batch + 1].reshape(n_batch + 1, 6, d)
        w = w_in[l]
        gate_cols = jnp.pad(w[:, sizes_a + sizes_b:sizes_a + sizes_b + n_gate], ((0, 0), (0, LANES - n_gate)))
        w_pad = jnp.concatenate([w[:, :sizes_a + sizes_b], gate_cols, w[:, sizes_a + sizes_b + n_gate:]],
                                axis=1).astype(BF16)
        qn_g = jnp.tile(na_qnorm_g[l].astype(F32), n_hc)[None]
        kn_g = jnp.tile(na_knorm_g[l].astype(F32), n_hc)[None]
        pa, pb, pg, qn, kn, vv = _inproj(xall, mod, norm1_g[l][None].astype(F32), w_pad, qn_g, kn_g, dims)

        oaf, oab = _hgrn_scan(pa, lower_bound[:, l][:, None, :], dims)
        conv_w = jnp.zeros((8, 3 * d_b), F32).at[:CONV_K].set(gdn_conv_w[l].astype(F32))
        alog_x = jnp.repeat(gdn_a_log[l].astype(F32), HEAD_DIM, axis=-1)[:, None, :]
        dtb_x = jnp.repeat(gdn_dt_bias[l].astype(F32), HEAD_DIM, axis=-1)[:, None, :]
        obf, obb = _gdn_scan(_gdn_prep(pb, pg, conv_w, alog_x, dtb_x, cos_t, sin_t, dims), dims)
        bias_tab = _natten_bias(na_rpb[l], t_len // GRID_W)
        oc = _natten(qn, kn, vv, bias_tab, dims)
        n_out = n_lat if last else n_all
        if not last:
            oc = jnp.concatenate([oc, _ctx_attn(qn, kn, vv, dims)], axis=0)
        na_g = jnp.tile(hgrn_onorm_g[l].astype(F32), d_a // HEAD_DIM)[None]
        nb_g = jnp.tile(gdn_onorm_g[l].astype(F32), n_hb)[None]
        xall_mid = _outproj(xall, mod, oaf, oab, pa, obf, obb, pb, oc, na_g, nb_g, w_out[l].astype(BF16), n_out,
                            dims)
        i = l // 2
        if l % 2 == 0:
            xall = _ffn(xall_mid, mod, norm2_g[l][None].astype(F32), ffn_w1[i].astype(BF16),
                        ffn_w3[i].astype(BF16), ffn_w2[i].astype(BF16), dims)
        else:
            xall = _moe(xall_mid, mod, norm2_g[l][None].astype(F32), moe_router_w[i], moe_router_b[i],
                        moe_w1[i], moe_w3[i], moe_w2[i], dims)
    return xall[:n_lat].reshape(n_batch, t_len, d).astype(x.dtype)
```

```python
import functools
import math

import jax
import jax.numpy as jnp
import numpy as np
from jax import lax
from jax.experimental import pallas as pl
from jax.experimental.pallas import tpu as pltpu

F32 = jnp.float32
BF16 = jnp.bfloat16
HI = lax.Precision.HIGHEST

EPS = 1e-6
HEAD_DIM = 64
CHUNK = 64
TOK_BLK = 256
GRID_W = 64
WIN_R = 8
WIN_C = 16
CONV_K = 5
ROPE_BASE = 10000.0
N_EXPERTS = 8
LANES = 128
SUBLANES = 8
ROW_TILE = 512
FFN_ROWS = 1024
FF_TILE = 512
MOE_ROWS = 1024
GATHER_PARTS = 4
NA_HEADS_PER_GROUP = 4
NA_ROWS_PER_STEP = 4
VMEM_LIMIT = 56 * 1024 * 1024
MXU_DEPTH = 256
NEG_BIG = -1e30


def _cparams(sem):
    return pltpu.CompilerParams(dimension_semantics=sem, vmem_limit_bytes=VMEM_LIMIT)


def _dot(a, b):
    return jnp.dot(a.astype(BF16), b.astype(BF16), preferred_element_type=F32)


def _dot_nt(a, b):
    return lax.dot_general(a.astype(BF16), b.astype(BF16), (((1,), (1,)), ((), ())),
                           preferred_element_type=F32)


def _dot_tn(a, b):
    return lax.dot_general(a.astype(BF16), b.astype(BF16), (((0,), (0,)), ((), ())),
                           preferred_element_type=F32)


def _dot_hi(a, b):
    return jnp.dot(a, b, precision=HI, preferred_element_type=F32)


def _iota(shape, dim):
    return lax.broadcasted_iota(jnp.int32, shape, dim)


def _head_mask(n):
    return (_iota((n, n), 0) // HEAD_DIM) == (_iota((n, n), 1) // HEAD_DIM)


def _head_sums(x, scale):
    n = x.shape[-1]
    w = min(n, MXU_DEPTH)
    gm = jnp.where(_head_mask(w), scale, 0.0).astype(BF16)
    hi = x.astype(BF16)
    lo = (x - hi.astype(F32)).astype(BF16)
    dot = lambda a: jnp.dot(a, gm, preferred_element_type=F32)
    parts = [dot(lo[:, s:s + w]) + dot(hi[:, s:s + w]) for s in range(0, n, w)]
    return parts[0] if len(parts) == 1 else jnp.concatenate(parts, axis=1)


def _group_mean_sq(x):
    return _head_sums(x * x, 1.0 / HEAD_DIM)


def _silu(x):
    return x * jax.nn.sigmoid(x)


def _modulated_norm(x, g, shift, scale):
    ms = jnp.mean(x * x, axis=-1, keepdims=True)
    y = x * lax.rsqrt(ms + EPS) * g
    return y * (1.0 + scale) + shift


def _mod_group(i, tile, n_lat, t_len, n_batch):
    return jnp.where(i * tile < n_lat, (i * tile) // t_len, n_batch)


def _adaln_kernel(c_ref, w_ref, b_ref, o_ref):
    s = _silu(c_ref[...])
    o_ref[0] = _dot(s, w_ref[0]) + b_ref[0]


def _adaln(cvec, ada_w, ada_b):
    depth, d, d6 = ada_w.shape
    tn = 1024
    return pl.pallas_call(
        _adaln_kernel,
        name="adaln",
        out_shape=jax.ShapeDtypeStruct((depth, 8, d6), F32),
        grid=(depth, d6 // tn),
        in_specs=[pl.BlockSpec((8, d), lambda l, j: (0, 0)),
                  pl.BlockSpec((1, d, tn), lambda l, j: (l, 0, j)),
                  pl.BlockSpec((1, 1, tn), lambda l, j: (l, 0, j))],
        out_specs=pl.BlockSpec((1, 8, tn), lambda l, j: (l, 0, j)),
        compiler_params=_cparams(("parallel", "parallel")),
    )(cvec, ada_w, ada_b.reshape(depth, 1, d6))


def _inproj_kernel(x_ref, m_ref, g_ref, w_ref, qg_ref, kg_ref,
                   pa_ref, pb_ref, pg_ref, qn_ref, kn_ref, vv_ref, *, d_a5, d_b4, d_c):
    h = _modulated_norm(x_ref[...], g_ref[...], m_ref[0, 0:1, :], m_ref[0, 1:2, :]).astype(BF16)
    o0 = 0
    pa_ref[...] = jnp.dot(h, w_ref[:, o0:o0 + d_a5], preferred_element_type=F32)
    o0 += d_a5
    pb_ref[...] = jnp.dot(h, w_ref[:, o0:o0 + d_b4], preferred_element_type=F32)
    o0 += d_b4
    pg_ref[...] = jnp.dot(h, w_ref[:, o0:o0 + LANES], preferred_element_type=F32)
    o0 += LANES
    q = jnp.dot(h, w_ref[:, o0:o0 + d_c], preferred_element_type=F32)
    q = q * lax.rsqrt(_group_mean_sq(q) + EPS) * qg_ref[...]
    qn_ref[...] = (q * HEAD_DIM ** -0.5).astype(BF16)
    o0 += d_c
    k = jnp.dot(h, w_ref[:, o0:o0 + d_c], preferred_element_type=F32)
    k = k * lax.rsqrt(_group_mean_sq(k) + EPS) * kg_ref[...]
    kn_ref[...] = k.astype(BF16)
    o0 += d_c
    vv_ref[...] = jnp.dot(h, w_ref[:, o0:o0 + d_c], preferred_element_type=F32).astype(BF16)


def _inproj(xall, mod, norm_g, w_pad, qn_g, kn_g, dims):
    n_all, d = xall.shape
    n_lat, t_len, n_batch, d_a, d_b, d_c = dims
    tm = ROW_TILE
    row = lambda i: (i, 0)
    const = lambda i: (0, 0)
    grp = lambda i: (_mod_group(i, tm, n_lat, t_len, n_batch), 0, 0)
    kern = functools.partial(_inproj_kernel, d_a5=5 * d_a, d_b4=4 * d_b, d_c=d_c)
    return pl.pallas_call(
        kern,
        name="inproj",
        out_shape=(jax.ShapeDtypeStruct((n_all, 5 * d_a), F32),
                   jax.ShapeDtypeStruct((n_all, 4 * d_b), F32),
                   jax.ShapeDtypeStruct((n_all, LANES), F32),
                   jax.ShapeDtypeStruct((n_all, d_c), BF16),
                   jax.ShapeDtypeStruct((n_all, d_c), BF16),
                   jax.ShapeDtypeStruct((n_all, d_c), BF16)),
        grid=(n_all // tm,),
        in_specs=[pl.BlockSpec((tm, d), row),
                  pl.BlockSpec((1, 6, d), grp),
                  pl.BlockSpec((1, d), const),
                  pl.BlockSpec(w_pad.shape, const),
                  pl.BlockSpec((1, d_c), const),
                  pl.BlockSpec((1, d_c), const)],
        out_specs=(pl.BlockSpec((tm, 5 * d_a), row), pl.BlockSpec((tm, 4 * d_b), row),
                   pl.BlockSpec((tm, LANES), row), pl.BlockSpec((tm, d_c), row),
                   pl.BlockSpec((tm, d_c), row), pl.BlockSpec((tm, d_c), row)),
        compiler_params=_cparams(("parallel",)),
    )(xall, mod, norm_g, w_pad, qn_g, kn_g)


def _scan_block_index(b, d, j, n_lat_blk, n_ctx_blk, n_batch):
    jc = jnp.where(d == 0, j, n_ctx_blk - 1 - j)
    jl = jnp.where(d == 0, j - n_ctx_blk, n_lat_blk - 1 - (j - n_ctx_blk))
    return jnp.where(j < n_ctx_blk, n_batch * n_lat_blk + b * n_ctx_blk + jc, b * n_lat_blk + jl)


def _block_diag(x, mask_bd, n_rep):
    xb = x.astype(BF16)
    return jnp.where(mask_bd, jnp.concatenate([xb] * n_rep, axis=0), jnp.zeros((), BF16))


def _scan_order(idx, d):
    return idx + d * (CHUNK - 1 - 2 * idx)


def _split3(x):
    hi = x.astype(BF16)
    r = x - hi.astype(F32)
    mid = r.astype(BF16)
    lo = (r - mid.astype(F32)).astype(BF16)
    return hi, mid, lo


def _sel_dot(sel, x):
    hi, mid, lo = _split3(x)
    dot = lambda a: jnp.dot(sel, a, preferred_element_type=F32)
    return (dot(lo) + dot(mid)) + dot(hi)


def _dot_sel(x, sel):
    hi, mid, lo = _split3(x)
    dot = lambda a: jnp.dot(a, sel, preferred_element_type=F32)
    return (dot(lo) + dot(mid)) + dot(hi)


HGRN_LEVELS = tuple(CHUNK >> (i + 1) for i in range(int(math.log2(CHUNK))))


def _hgrn_kernel(qf_ref, zf_ref, vf_ref, qb_ref, zb_ref, vb_ref, lb_ref, of_ref, ob_ref, st_ref, *, n_chunk):
    ins = ((qf_ref, zf_ref, vf_ref, of_ref), (qb_ref, zb_ref, vb_ref, ob_ref))
    j = pl.program_id(1)
    n = qf_ref.shape[-1]
    n_rep = n // HEAD_DIM

    @pl.when(j == 0)
    def _():
        st_ref[...] = jnp.zeros_like(st_ref)

    mask_bd = _head_mask(n)
    gones = mask_bd.astype(BF16)
    bd = lambda a: _block_diag(a, mask_bd, n_rep)

    work = []
    masks = []
    for dd in range(2):
        q_ref, z_ref, v_ref, _ = ins[dd]
        lb = lb_ref[dd]
        s_t = _scan_order(_iota((CHUNK, CHUNK), 0), dd)
        s_u = _scan_order(_iota((CHUNK, CHUNK), 1), dd)
        sets = [s_u <= s_t, s_u > s_t]
        for m in HGRN_LEVELS:
            ref = (s_t // (2 * m)) * (2 * m) + m - 1
            sets.append((s_u > jnp.minimum(s_t, ref)) & (s_u <= jnp.maximum(s_t, ref)))
        sel = jnp.concatenate([jnp.where(a, 1.0, 0.0).astype(BF16) for a in sets], axis=0)
        s_row = _scan_order(_iota((CHUNK, n), 0), dd)
        s_col = _scan_order(_iota((CHUNK, n), 1) % HEAD_DIM, dd)
        masks.append((s_row == s_col, [(s_row // m) % 2 == 1 for m in HGRN_LEVELS],
                      [(s_row // (2 * m)) == (s_col // (2 * m)) for m in HGRN_LEVELS]))
        for c in range(n_chunk):
            cc = c if dd == 0 else n_chunk - 1 - c
            r = slice(cc * CHUNK, (cc + 1) * CHUNK)
            q = q_ref[r, :] * HEAD_DIM ** -0.5
            z = z_ref[r, :]
            v = v_ref[r, :]
            log_f = (jnp.minimum(z, 0.0) - jnp.log1p(jnp.exp(-jnp.abs(z)))) + jnp.log1p(lb * jnp.exp(-z))
            k = (1.0 - lb) * jax.nn.sigmoid(-z)
            work.append((dd, r, q, k, v, _sel_dot(sel, log_f)))
    atts = [jnp.where(masks[dd][0], jnp.dot((q * k).astype(BF16), gones, preferred_element_type=F32), 0.0)
            for dd, _, q, k, _, _ in work]
    for lvl, m in enumerate(HGRN_LEVELS):
        for i, (dd, _, q, k, _, sums) in enumerate(work):
            second = masks[dd][1][lvl]
            e = jnp.exp(sums[(2 + lvl) * CHUNK:(3 + lvl) * CHUNK])
            a_m = _dot_nt(jnp.where(second, q * e, 0.0), bd(jnp.where(second, 0.0, k * e)))
            if 2 * m < CHUNK:
                a_m = jnp.where(masks[dd][2][lvl], a_m, 0.0)
            atts[i] = atts[i] + a_m
    pre = []
    for (dd, r, q, k, v, sums), att in zip(work, atts):
        b_incl = sums[0:CHUNK]
        b_after = sums[CHUNK:2 * CHUNK]
        e_end = jnp.exp(b_incl[0:1, :] + b_after[0:1, :])
        upd = jnp.where(mask_bd, _dot_tn(v, k * jnp.exp(b_after)), 0.0)
        pre.append((dd, r, _dot(att, bd(v)), (q * jnp.exp(b_incl)).astype(BF16), e_end, upd))

    sts = [st_ref[0], st_ref[1]]
    for c in range(n_chunk):
        for dd in range(2):
            _, r, o_intra, q_hat, e_end, upd = pre[dd * n_chunk + c]
            ins[dd][3][r, :] = o_intra + _dot_nt(q_hat, sts[dd])
            sts[dd] = sts[dd] * e_end + upd
    st_ref[0] = sts[0]
    st_ref[1] = sts[1]


def _hgrn_scan(pa, lb2, dims):
    n_all = pa.shape[0]
    n_lat, t_len, n_batch, d_a, d_b, d_c = dims
    n_lat_blk = t_len // TOK_BLK
    n_ctx_blk = (n_all - n_lat) // n_batch // TOK_BLK
    blk = functools.partial(_scan_block_index, n_lat_blk=n_lat_blk, n_ctx_blk=n_ctx_blk, n_batch=n_batch)
    kern = functools.partial(_hgrn_kernel, n_chunk=TOK_BLK // CHUNK)
    in_specs = []
    for dd in range(2):
        in_specs += [pl.BlockSpec((TOK_BLK, d_a), lambda b, j, dd=dd: (blk(b, dd, j), 0)),
                     pl.BlockSpec((TOK_BLK, d_a), lambda b, j, dd=dd: (blk(b, dd, j), 1 + dd)),
                     pl.BlockSpec((TOK_BLK, d_a), lambda b, j, dd=dd: (blk(b, dd, j), 3))]
    in_specs.append(pl.BlockSpec((2, 1, d_a), lambda b, j: (0, 0, 0)))
    out_specs = tuple(pl.BlockSpec((TOK_BLK, d_a), lambda b, j, dd=dd: (blk(b, dd, j), 0)) for dd in range(2))
    return pl.pallas_call(
        kern,
        name="hgrn_scan",
        out_shape=(jax.ShapeDtypeStruct((n_all, d_a), F32), jax.ShapeDtypeStruct((n_all, d_a), F32)),
        grid=(n_batch, n_lat_blk + n_ctx_blk),
        in_specs=in_specs,
        out_specs=out_specs,
        scratch_shapes=[pltpu.VMEM((2, d_a, d_a), F32)],
        compiler_params=_cparams(("parallel", "arbitrary")),
    )(pa, pa, pa, pa, pa, pa, lb2)


def _gdn_prep_kernel(x_ref, prev_ref, next_ref, pg_ref, cw_ref, alog_ref, dtb_ref, cos_ref, sin_ref,
                     w_ref, u0_ref, qh_ref, qg_ref, kh_ref, ee_ref,
                     *, n_lat_blk, n_ctx_blk, n_lat_blks_total, d_b):
    i = pl.program_id(0)
    is_lat = i < n_lat_blks_total
    pos = jnp.where(is_lat, i % n_lat_blk, (i - n_lat_blks_total) % n_ctx_blk)
    n_seq_blk = jnp.where(is_lat, n_lat_blk, n_ctx_blk)
    first = pos == 0
    last = pos == n_seq_blk - 1
    x = x_ref[...]
    halo = prev_ref.shape[0]
    prev = jnp.where(first, 0.0, prev_ref[...])
    nxt = jnp.where(last, 0.0, next_ref[...])
    rows = _iota(x.shape, 0)
    tb = x.shape[0]
    half = CONV_K // 2
    acc = x * cw_ref[half:half + 1, :]
    for s in range(1, half + 1):
        xs = pltpu.roll(x, s, 0)
        for r in range(s):
            xs = jnp.where(rows == r, prev[halo - s + r:halo - s + r + 1, :], xs)
        acc = acc + xs * cw_ref[half - s:half - s + 1, :]
        xs = pltpu.roll(x, tb - s, 0)
        for r in range(s):
            xs = jnp.where(rows == tb - s + r, nxt[r:r + 1, :], xs)
        acc = acc + xs * cw_ref[half + s:half + s + 1, :]
    y = _silu(acc)
    def l2n(a):
        return a * lax.rsqrt(_head_sums(a * a, 1.0) + EPS)

    lane = _iota((tb, d_b), 1) % HEAD_DIM
    lo = lane < HEAD_DIM // 2

    def rope(a):
        partner = jnp.where(lo, pltpu.roll(a, d_b - HEAD_DIM // 2, 1), pltpu.roll(a, HEAD_DIM // 2, 1))
        return jnp.where(is_lat, a * cos_ref[...] + partner * sin_ref[...], a)

    q_all = rope(l2n(y[:, 0:d_b])) * HEAD_DIM ** -0.5
    k_all = rope(l2n(y[:, d_b:2 * d_b]))
    v_all = y[:, 2 * d_b:3 * d_b]
    n_h = d_b // HEAD_DIM
    g = pg_ref[...]
    e_r = _iota((LANES, d_b), 0)
    e_c = _iota((LANES, d_b), 1) // HEAD_DIM
    mask_bd = _head_mask(d_b)
    bd = lambda a: _block_diag(a, mask_bd, n_h)
    ones_cc = jnp.ones((CHUNK, CHUNK), BF16)
    ee_ref[...] = jnp.zeros_like(ee_ref)
    work = []
    t_invs = []
    for dd in range(2):
        a_x = _dot_sel(g, jnp.where(e_r == dd * n_h + e_c, 1.0, 0.0).astype(BF16))
        b_x = _dot_sel(g, jnp.where(e_r == (2 + dd) * n_h + e_c, 1.0, 0.0).astype(BF16))
        t = a_x + dtb_ref[dd]
        softplus = jnp.maximum(t, 0.0) + jnp.log1p(jnp.exp(-jnp.abs(t)))
        la_all = -jnp.exp(alog_ref[dd]) * softplus
        be_all = jax.nn.sigmoid(b_x)
        s_t = _scan_order(_iota((CHUNK, CHUNK), 0), dd)
        s_u = _scan_order(_iota((CHUNK, CHUNK), 1), dd)
        m_incl = jnp.where(s_u <= s_t, 1.0, 0.0).astype(BF16)
        s_row = _scan_order(_iota((CHUNK, d_b), 0), dd)
        s_col = _scan_order(_iota((CHUNK, d_b), 1) % HEAD_DIM, dd)
        incl = s_col <= s_row
        strict = s_col < s_row
        m_before = jnp.where(s_row <= s_col, 1.0, 0.0)
        eye_f = jnp.where(s_col == s_row, 1.0, 0.0)
        pair = (s_row // 2) == (s_col // 2)
        levels = []
        m = 4
        while m <= CHUNK:
            levels.append(((s_row // m) == (s_col // m)) & ((s_row // (m // 2)) != (s_col // (m // 2))))
            m *= 2
        end_row = CHUNK - 1 if dd == 0 else 0
        for c in range(tb // CHUNK):
            r = slice(c * CHUNK, (c + 1) * CHUNK)
            q, k, v, la, be = q_all[r], k_all[r], v_all[r], la_all[r], be_all[r]
            g_t = _sel_dot(m_incl, la)
            g_s = _sel_dot(ones_cc, la * m_before)
            gam = jnp.where(incl, jnp.exp(jnp.minimum(g_t - g_s, 0.0)), 0.0)
            k_bd = bd(k)
            kk = _dot_nt(k, k_bd)
            qk = _dot_nt(q, k_bd)
            a = jnp.where(strict, be * kk * gam, 0.0)
            eg = jnp.exp(g_t)
            g_end = g_t[end_row:end_row + 1, :]
            qh_ref[dd, r, :] = (q * eg).astype(BF16)
            qg_ref[dd, r, :] = (qk * gam).astype(BF16)
            kh_ref[dd, r, :] = (k * jnp.exp(g_end - g_t)).astype(BF16)
            ee_ref[dd, 0, c:c + 1, :] = jnp.exp(g_end)
            work.append((dd, r, a, bd(be * eg * k), bd(be * v), levels))
            t_invs.append(eye_f - jnp.where(pair, a, 0.0))
    for lv in range(int(math.log2(CHUNK)) - 1):
        xs = [_dot(jnp.where(wk[5][lv], wk[2], 0.0), bd(t)) for wk, t in zip(work, t_invs)]
        t_invs = [t - _dot(t, bd(x)) for t, x in zip(t_invs, xs)]
    for (dd, r, _, wk_bd, vb_bd, _), t in zip(work, t_invs):
        w_ref[dd, r, :] = _dot(t, wk_bd).astype(BF16)
        u0_ref[dd, r, :] = _dot(t, vb_bd)


def _gdn_prep(pb, pg, conv_w, alog_x, dtb_x, cos_t, sin_t, dims):
    n_all = pb.shape[0]
    n_lat, t_len, n_batch, d_a, d_b, d_c = dims
    n_lat_blk = t_len // TOK_BLK
    n_ctx_blk = (n_all - n_lat) // n_batch // TOK_BLK
    n_blk = n_all // TOK_BLK
    halo = 8
    per = TOK_BLK // halo
    kern = functools.partial(_gdn_prep_kernel, n_lat_blk=n_lat_blk, n_ctx_blk=n_ctx_blk,
                             n_lat_blks_total=n_lat // TOK_BLK, d_b=d_b)
    row = lambda i: (i, 0)
    const2 = lambda i: (0, 0)
    const3 = lambda i: (0, 0, 0)
    tab = lambda i: (jnp.where(i < n_lat // TOK_BLK, i % n_lat_blk, 0), 0)
    both = pl.BlockSpec((2, TOK_BLK, d_b), lambda i: (0, i, 0))
    sds = jax.ShapeDtypeStruct
    return pl.pallas_call(
        kern,
        name="gdn_prep",
        out_shape=(sds((2, n_all, d_b), BF16), sds((2, n_all, d_b), F32), sds((2, n_all, d_b), BF16),
                   sds((2, n_all, d_b), BF16), sds((2, n_all, d_b), BF16), sds((2, n_blk, 8, d_b), F32)),
        grid=(n_blk,),
        in_specs=[pl.BlockSpec((TOK_BLK, 3 * d_b), row),
                  pl.BlockSpec((halo, 3 * d_b), lambda i: (jnp.maximum(i * per - 1, 0), 0)),
                  pl.BlockSpec((halo, 3 * d_b), lambda i: (jnp.minimum((i + 1) * per, n_blk * per - 1), 0)),
                  pl.BlockSpec((TOK_BLK, LANES), row),
                  pl.BlockSpec((8, 3 * d_b), const2),
                  pl.BlockSpec((2, 1, d_b), const3),
                  pl.BlockSpec((2, 1, d_b), const3),
                  pl.BlockSpec((TOK_BLK, d_b), tab),
                  pl.BlockSpec((TOK_BLK, d_b), tab)],
        out_specs=(both, both, both, both, both, pl.BlockSpec((2, 1, 8, d_b), lambda i: (0, i, 0, 0))),
        compiler_params=_cparams(("parallel",)),
    )(pb, pb, pb, pg, conv_w, alog_x, dtb_x, cos_t, sin_t)


def _gdn_kernel(*refs, n_chunk, n_batch, blk):
    n_chain = 2 * n_batch
    ins = [refs[6 * c:6 * c + 6] for c in range(n_chain)]
    o_hbm = refs[6 * n_chain:6 * n_chain + 2]
    st_ref, stage_ref, sem = refs[6 * n_chain + 2:]
    j = pl.program_id(0)
    n = stage_ref.shape[-1]
    n_rep = n // HEAD_DIM

    def out_copy(c, step):
        row0 = pl.multiple_of(blk(c // 2, c % 2, step) * TOK_BLK, TOK_BLK)
        return pltpu.make_async_copy(stage_ref.at[c], o_hbm[c % 2].at[pl.ds(row0, TOK_BLK)], sem.at[c])

    @pl.when(j == 0)
    def _():
        st_ref[...] = jnp.zeros_like(st_ref)

    @pl.when(j > 0)
    def _():
        for c in range(n_chain):
            out_copy(c, j - 1).wait()

    mask_bd = _head_mask(n)
    bd = lambda a: _block_diag(a, mask_bd, n_rep)
    sts = [st_ref[c] for c in range(n_chain)]
    for ck in range(n_chunk):
        ccs = [ck if c % 2 == 0 else n_chunk - 1 - ck for c in range(n_chain)]
        rs = [slice(cc * CHUNK, (cc + 1) * CHUNK) for cc in ccs]
        us = [ins[c][1][0, rs[c], :] - _dot_nt(ins[c][0][0, rs[c], :], sts[c]) for c in range(n_chain)]
        o_state = [_dot_nt(ins[c][2][0, rs[c], :], sts[c]) for c in range(n_chain)]
        u_bds = [bd(u) for u in us]
        for c in range(n_chain):
            stage_ref[c, rs[c], :] = o_state[c] + _dot(ins[c][3][0, rs[c], :], u_bds[c])
        upds = [jnp.where(mask_bd, _dot_tn(us[c], ins[c][4][0, rs[c], :]), 0.0) for c in range(n_chain)]
        sts = [sts[c] * ins[c][5][0, 0, ccs[c]:ccs[c] + 1, :] + upds[c] for c in range(n_chain)]
    for c in range(n_chain):
        st_ref[c] = sts[c]
        out_copy(c, j).start()

    @pl.when(j == pl.num_programs(0) - 1)
    def _():
        for c in range(n_chain):
            out_copy(c, j).wait()


def _gdn_scan(wy, dims):
    n_all = wy[0].shape[1]
    n_lat, t_len, n_batch, d_a, d_b, d_c = dims
    n_lat_blk = t_len // TOK_BLK
    n_ctx_blk = (n_all - n_lat) // n_batch // TOK_BLK
    blk = functools.partial(_scan_block_index, n_lat_blk=n_lat_blk, n_ctx_blk=n_ctx_blk, n_batch=n_batch)
    kern = functools.partial(_gdn_kernel, n_chunk=TOK_BLK // CHUNK, n_batch=n_batch, blk=blk)
    in_specs = []
    for b in range(n_batch):
        for dd in range(2):
            tok = pl.BlockSpec((1, TOK_BLK, d_b), lambda j, b=b, dd=dd: (dd, blk(b, dd, j), 0))
            in_specs += [tok] * 5 + [pl.BlockSpec((1, 1, 8, d_b), lambda j, b=b, dd=dd: (dd, blk(b, dd, j), 0, 0))]
    n_chain = 2 * n_batch
    return pl.pallas_call(
        kern,
        name="gdn_scan",
        out_shape=(jax.ShapeDtypeStruct((n_all, d_b), F32), jax.ShapeDtypeStruct((n_all, d_b), F32)),
        grid=(n_lat_blk + n_ctx_blk,),
        in_specs=in_specs,
        out_specs=(pl.BlockSpec(memory_space=pl.ANY), pl.BlockSpec(memory_space=pl.ANY)),
        scratch_shapes=[pltpu.VMEM((n_chain, d_b, d_b), F32), pltpu.VMEM((n_chain, TOK_BLK, d_b), F32),
                        pltpu.SemaphoreType.DMA((n_chain,))],
        compiler_params=_cparams(("arbitrary",)),
    )(*(list(wy) * n_chain))


def _stack_heads(q, n_rep):
    m, n = q.shape
    keep = (_iota((n_rep * m, n), 0) // m) == (_iota((n_rep * m, n), 1) // HEAD_DIM)
    return jnp.where(keep, jnp.concatenate([q] * n_rep, axis=0), jnp.zeros((), q.dtype))


def _fold_heads(o, m, n_rep):
    n = o.shape[1]
    lane_h = _iota((m, n), 1) // HEAD_DIM
    acc = jnp.zeros((m, n), F32)
    for h in range(n_rep):
        acc = acc + jnp.where(lane_h == h, o[h * m:(h + 1) * m, :], 0.0)
    return acc


def _natten_kernel(*refs, n_rows):
    q_ref, k_ref, v_ref, kc_ref, vc_ref = refs[:5]
    bias_refs = refs[5:-1]
    o_ref = refs[-1]
    step = pl.program_id(1)
    n = q_ref.shape[-1]
    kr = min(WIN_R, n_rows)
    dn = (((1,), (1,)), ((), ()))
    gw = NA_HEADS_PER_GROUP * HEAD_DIM
    n_stack = NA_HEADS_PER_GROUP * GRID_W
    units = []
    for rr, bias_ref in enumerate(bias_refs):
        r = step * len(bias_refs) + rr
        r0 = jnp.clip(r - WIN_R // 2, 0, n_rows - kr)
        sl = pl.ds(pl.multiple_of(r0 * GRID_W, GRID_W), kr * GRID_W)
        q_rows = slice(rr * GRID_W, (rr + 1) * GRID_W)
        for g in range(n // gw):
            lanes = slice(g * gw, (g + 1) * gw)
            qs = _stack_heads(q_ref[q_rows, lanes], NA_HEADS_PER_GROUP)
            s_loc = lax.dot_general(qs, k_ref[sl, lanes], dn, preferred_element_type=F32)
            s_loc = s_loc + bias_ref[0, g * n_stack:(g + 1) * n_stack, :]
            s_ctx = lax.dot_general(qs, kc_ref[:, lanes], dn, preferred_element_type=F32)
            units.append((q_rows, lanes, sl, s_loc, s_ctx))
    probs = []
    for q_rows, lanes, sl, s_loc, s_ctx in units:
        m = jnp.maximum(jnp.max(s_loc, axis=-1, keepdims=True), jnp.max(s_ctx, axis=-1, keepdims=True))
        p_loc = jnp.exp(s_loc - m)
        p_ctx = jnp.exp(s_ctx - m)
        inv = 1.0 / (jnp.sum(p_loc, axis=-1, keepdims=True) + jnp.sum(p_ctx, axis=-1, keepdims=True))
        probs.append(((p_loc * inv).astype(BF16), (p_ctx * inv).astype(BF16)))
    for (q_rows, lanes, sl, _, _), (p_loc, p_ctx) in zip(units, probs):
        o = jnp.dot(p_loc, v_ref[sl, lanes], preferred_element_type=F32)
        o = o + jnp.dot(p_ctx, vc_ref[:, lanes], preferred_element_type=F32)
        o_ref[q_rows, lanes] = _fold_heads(o, GRID_W, NA_HEADS_PER_GROUP)


def _natten(qn, kn, vv, bias_tab, dims):
    n_lat, t_len, n_batch, d_a, d_b, d_c = dims
    n_all = qn.shape[0]
    l_ctx = (n_all - n_lat) // n_batch
    n_rows = t_len // GRID_W
    kr = min(WIN_R, n_rows)
    ctx0 = n_lat // l_ctx
    rb = NA_ROWS_PER_STEP
    assert n_rows % rb == 0
    n_steps = n_rows // rb

    def cfg(rr):
        def index(b, i):
            r = i * rb + rr
            r0 = jnp.clip(r - WIN_R // 2, 0, n_rows - kr)
            return (r - r0, 0, 0)
        return index

    kern = functools.partial(_natten_kernel, n_rows=n_rows)
    return pl.pallas_call(
        kern,
        name="natten",
        out_shape=jax.ShapeDtypeStruct((n_lat, d_c), F32),
        grid=(n_batch, n_steps),
        in_specs=[pl.BlockSpec((rb * GRID_W, d_c), lambda b, i: (b * n_steps + i, 0)),
                  pl.BlockSpec((t_len, d_c), lambda b, i: (b, 0)),
                  pl.BlockSpec((t_len, d_c), lambda b, i: (b, 0)),
                  pl.BlockSpec((l_ctx, d_c), lambda b, i: (ctx0 + b, 0)),
                  pl.BlockSpec((l_ctx, d_c), lambda b, i: (ctx0 + b, 0))]
                 + [pl.BlockSpec((1,) + bias_tab.shape[1:], cfg(rr)) for rr in range(rb)],
        out_specs=pl.BlockSpec((rb * GRID_W, d_c), lambda b, i: (b * n_steps + i, 0)),
        compiler_params=_cparams(("parallel", "arbitrary")),
    )(qn, kn, vv, kn, vv, *([bias_tab] * rb))


def _ctx_attn_kernel(q_ref, k_ref, v_ref, o_ref):
    n = q_ref.shape[-1]
    n_rep = n // HEAD_DIM
    qs = _stack_heads(q_ref[...], n_rep)
    s = lax.dot_general(qs, k_ref[...], (((1,), (1,)), ((), ())), preferred_element_type=F32)
    p = jnp.exp(s - jnp.max(s, axis=-1, keepdims=True))
    p = p * (1.0 / jnp.sum(p, axis=-1, keepdims=True))
    o = jnp.dot(p.astype(BF16), v_ref[...], preferred_element_type=F32)
    o_ref[...] = _fold_heads(o, q_ref.shape[0], n_rep)


def _ctx_attn(qn, kn, vv, dims):
    n_lat, t_len, n_batch, d_a, d_b, d_c = dims
    n_all = qn.shape[0]
    l_ctx = (n_all - n_lat) // n_batch
    tq = 64
    per = l_ctx // tq
    q0 = n_lat // tq
    c0 = n_lat // l_ctx
    return pl.pallas_call(
        _ctx_attn_kernel,
        name="ctx_attn",
        out_shape=jax.ShapeDtypeStruct((n_all - n_lat, d_c), F32),
        grid=(n_batch, per),
        in_specs=[pl.BlockSpec((tq, d_c), lambda b, i: (q0 + b * per + i, 0)),
                  pl.BlockSpec((l_ctx, d_c), lambda b, i: (c0 + b, 0)),
                  pl.BlockSpec((l_ctx, d_c), lambda b, i: (c0 + b, 0))],
        out_specs=pl.BlockSpec((tq, d_c), lambda b, i: (b * per + i, 0)),
        compiler_params=_cparams(("parallel", "arbitrary")),
    )(qn, kn, vv)


def _natten_bias(rpb, n_rows):
    n_h = rpb.shape[0]
    kr = min(WIN_R, n_rows)
    cols = jnp.arange(GRID_W)
    c0 = jnp.clip(cols - WIN_C // 2, 0, GRID_W - WIN_C)
    kc = jnp.arange(GRID_W)
    in_win = (kc[None, :] >= c0[:, None]) & (kc[None, :] < c0[:, None] + WIN_C)
    per = 2 * GRID_W
    rp = rpb.astype(F32)
    u = jnp.concatenate([rp[..., WIN_C - 1:], jnp.zeros(rp.shape[:-1] + (per - 2 * WIN_C + 1,), F32),
                         rp[..., :WIN_C - 1]], axis=-1)
    rel = jnp.tile(u, (1, 1, GRID_W))[..., :GRID_W * (per - 1)]
    rel = rel.reshape(rp.shape[:-1] + (GRID_W, per - 1))[..., :GRID_W]
    toe = jnp.where(in_win[None, None], rel, NEG_BIG)
    tabs = jnp.stack([toe[:, WIN_R - 1 - delta:WIN_R - 1 - delta + kr] for delta in range(kr)])
    return tabs.transpose(0, 1, 3, 2, 4).reshape(kr, n_h * GRID_W, kr * GRID_W)


def _outproj_kernel(x_ref, m_ref, oaf_ref, oab_ref, ga_ref, obf_ref, obb_ref, gb_ref, oc_ref, na_ref, nb_ref, w_ref,
                    o_ref, *, d_a, d_b):
    oa = oaf_ref[...] + oab_ref[...]
    ya = oa * lax.rsqrt(_group_mean_sq(oa) + EPS) * na_ref[...] * _silu(ga_ref[...])
    ob = obf_ref[...] + obb_ref[...]
    yb = ob * lax.rsqrt(_group_mean_sq(ob) + EPS) * nb_ref[...] * _silu(gb_ref[...])
    acc = _dot(ya, w_ref[0:d_a, :])
    acc = acc + _dot(yb, w_ref[d_a:d_a + d_b, :])
    acc = acc + _dot(oc_ref[...], w_ref[d_a + d_b:, :])
    o_ref[...] = x_ref[...] + m_ref[0, 2:3, :] * acc


def _outproj(xall, mod, oaf, oab, pa, obf, obb, pb, oc, na_g, nb_g, w_out, n_rows_out, dims):
    n_lat, t_len, n_batch, d_a, d_b, d_c = dims
    d = xall.shape[1]
    tm = ROW_TILE
    row = lambda i: (i, 0)
    const = lambda i: (0, 0)
    grp = lambda i: (_mod_group(i, tm, n_lat, t_len, n_batch), 0, 0)
    kern = functools.partial(_outproj_kernel, d_a=d_a, d_b=d_b)
    return pl.pallas_call(
        kern,
        name="outproj",
        out_shape=jax.ShapeDtypeStruct((n_rows_out, d), F32),
        grid=(n_rows_out // tm,),
        in_specs=[pl.BlockSpec((tm, d), row),
                  pl.BlockSpec((1, 6, d), grp),
                  pl.BlockSpec((tm, d_a), row),
                  pl.BlockSpec((tm, d_a), row),
                  pl.BlockSpec((tm, d_a), lambda i: (i, 4)),
                  pl.BlockSpec((tm, d_b), row),
                  pl.BlockSpec((tm, d_b), row),
                  pl.BlockSpec((tm, d_b), lambda i: (i, 3)),
                  pl.BlockSpec((tm, d_c), row),
                  pl.BlockSpec((1, d_a), const),
                  pl.BlockSpec((1, d_b), const),
                  pl.BlockSpec(w_out.shape, const)],
        out_specs=pl.BlockSpec((tm, d), row),
        compiler_params=_cparams(("parallel",)),
    )(xall, mod, oaf, oab, pa, obf, obb, pb, oc, na_g, nb_g, w_out)


def _ffn_kernel(x_ref, m_ref, g_ref, w1_ref, w3_ref, w2_ref, o_ref, h_ref, acc_ref):
    j = pl.program_id(1)

    @pl.when(j == 0)
    def _():
        h_ref[...] = _modulated_norm(x_ref[...], g_ref[...], m_ref[0, 3:4, :], m_ref[0, 4:5, :]).astype(BF16)
        acc_ref[...] = jnp.zeros_like(acc_ref)

    h = h_ref[...]
    a = jnp.dot(h, w1_ref[...], preferred_element_type=F32)
    b = jnp.dot(h, w3_ref[...], preferred_element_type=F32)
    acc_ref[...] += _dot(_silu(a) * b, w2_ref[...])

    @pl.when(j == pl.num_programs(1) - 1)
    def _():
        o_ref[...] = x_ref[...] + m_ref[0, 5:6, :] * acc_ref[...]


def _ffn(xall, mod, norm_g, w1, w3, w2, dims):
    n_lat, t_len, n_batch, d_a, d_b, d_c = dims
    n_rows, d = xall.shape
    d_ff = w1.shape[1]
    tm, tf = FFN_ROWS, FF_TILE
    assert t_len % tm == 0
    grp = lambda i, j: (_mod_group(i, tm, n_lat, t_len, n_batch), 0, 0)
    return pl.pallas_call(
        _ffn_kernel,
        name="ffn",
        out_shape=jax.ShapeDtypeStruct((n_rows, d), F32),
        grid=(pl.cdiv(n_rows, tm), d_ff // tf),
        in_specs=[pl.BlockSpec((tm, d), lambda i, j: (i, 0)),
                  pl.BlockSpec((1, 6, d), grp),
                  pl.BlockSpec((1, d), lambda i, j: (0, 0)),
                  pl.BlockSpec((d, tf), lambda i, j: (0, j)),
                  pl.BlockSpec((d, tf), lambda i, j: (0, j)),
                  pl.BlockSpec((tf, d), lambda i, j: (j, 0))],
        out_specs=pl.BlockSpec((tm, d), lambda i, j: (i, 0)),
        scratch_shapes=[pltpu.VMEM((tm, d), BF16), pltpu.VMEM((tm, d), F32)],
        compiler_params=_cparams(("parallel", "arbitrary")),
    )(xall, mod, norm_g, w1, w3, w2)


def _rows_to_tiles(x):
    r, d = x.shape
    slabs = jnp.stack([x[:, s * LANES:(s + 1) * LANES].reshape(r // SUBLANES, SUBLANES, LANES)
                       for s in range(d // LANES)], axis=1)
    return jnp.swapaxes(slabs, 1, 2).reshape(r, d // LANES, LANES)


def _tiles_to_slabs(x3):
    r, n_tile, _ = x3.shape
    y = jnp.swapaxes(x3.reshape(r // SUBLANES, SUBLANES, n_tile, LANES), 1, 2)
    return [y[:, s].reshape(r, LANES) for s in range(n_tile)]


def _route_kernel(x_ref, m_ref, g_ref, wr_ref, br_ref, h_ref, e_ref, gt_ref):
    h = _modulated_norm(x_ref[...], g_ref[...], m_ref[0, 3:4, :], m_ref[0, 4:5, :])
    h_ref[...] = _rows_to_tiles(h)
    lane = _iota((h.shape[0], LANES), 1)
    logits = jnp.where(lane < N_EXPERTS, _dot_hi(h, wr_ref[...]) + br_ref[...], -jnp.inf)
    m1 = jnp.max(logits, axis=-1, keepdims=True)
    lane_f = lane.astype(F32)
    i1 = jnp.min(jnp.where(logits == m1, lane_f, float(LANES)), axis=-1, keepdims=True).astype(jnp.int32)
    rest = jnp.where(lane == i1, -jnp.inf, logits)
    m2 = jnp.max(rest, axis=-1, keepdims=True)
    i2 = jnp.min(jnp.where(rest == m2, lane_f, float(LANES)), axis=-1, keepdims=True).astype(jnp.int32)
    e2 = jnp.exp(m2 - m1)
    g1 = 1.0 / (1.0 + e2)
    g2 = e2 / (1.0 + e2)
    e_ref[...] = jnp.where(lane == 0, i1, jnp.where(lane == 1, i2, 0))
    gt_ref[...] = jnp.where(lane == 0, g1, jnp.where(lane == 1, g2, 0.0))


def _route(x, mod, norm_g, wr_pad, br_pad, dims):
    n_lat, t_len, n_batch, d_a, d_b, d_c = dims
    n, d = x.shape
    tm = ROW_TILE
    row = lambda i: (i, 0)
    const = lambda i: (0, 0)
    grp = lambda i: (_mod_group(i, tm, n_lat, t_len, n_batch), 0, 0)
    return pl.pallas_call(
        _route_kernel,
        name="moe_route",
        out_shape=(jax.ShapeDtypeStruct((n, d // LANES, LANES), F32), jax.ShapeDtypeStruct((n, LANES), jnp.int32),
                   jax.ShapeDtypeStruct((n, LANES), F32)),
        grid=(n // tm,),
        in_specs=[pl.BlockSpec((tm, d), row), pl.BlockSpec((1, 6, d), grp), pl.BlockSpec((1, d), const),
                  pl.BlockSpec((d, LANES), const), pl.BlockSpec((1, LANES), const)],
        out_specs=(pl.BlockSpec((tm, d // LANES, LANES), lambda i: (i, 0, 0)), pl.BlockSpec((tm, LANES), row),
                   pl.BlockSpec((tm, LANES), row)),
        compiler_params=_cparams(("parallel",)),
    )(x, mod, norm_g, wr_pad, br_pad)


def _expert_kernel(be_ref, nv_ref, idx0_ref, idxn_ref, dstp_ref, dstc_ref, h_hbm, w1_ref, w3_ref, w2_ref, y_hbm,
                   xbuf_ref, hb_ref, acc_ref, out_ref, sem_in, sem_out, *, rows_per_step, n_pairs):
    i = pl.program_id(0)
    j = pl.program_id(1)
    n_blk = pl.num_programs(0)
    n_ff = pl.num_programs(1)
    bm = hb_ref.shape[0]
    rps = rows_per_step
    n_issue = xbuf_ref.shape[1]
    n_tile = xbuf_ref.shape[2]
    live = i < nv_ref[0]
    cur = i % 2
    nxt = 1 - cur

    def in_copy(tok, slot, r):
        return pltpu.make_async_copy(h_hbm.at[pl.ds(tok, 1)], xbuf_ref.at[slot, pl.ds(r, 1)], sem_in)

    def out_copy(slot, r, dst):
        return pltpu.make_async_copy(out_ref.at[slot, pl.ds(r, 1)], y_hbm.at[pl.ds(dst, 1)], sem_out)

    def wait_in(slot):
        for c in range(n_issue // rps):
            pltpu.make_async_copy(h_hbm.at[pl.ds(0, rps)], xbuf_ref.at[slot, pl.ds(c * rps, rps)], sem_in).wait()

    def wait_out(slot):
        for c in range(n_issue // rps):
            pltpu.make_async_copy(out_ref.at[slot, pl.ds(c * rps, rps)], y_hbm.at[pl.ds(0, rps)], sem_out).wait()

    @pl.when((i == 0) & (j == 0))
    def _():
        out_ref[...] = jnp.zeros_like(out_ref)

        def start(g, carry):
            for u in range(GATHER_PARTS):
                r = g * GATHER_PARTS + u
                in_copy(idx0_ref[0, 0, r], 0, r).start()
            return carry

        lax.fori_loop(0, n_issue // GATHER_PARTS, start, 0)

    @pl.when(j == 0)
    def _():
        wait_in(cur)

    @pl.when(live & (j == 0))
    def _():
        for s, slab in enumerate(_tiles_to_slabs(xbuf_ref[cur, 0:bm])):
            hb_ref[:, s * LANES:(s + 1) * LANES] = slab.astype(BF16)
        acc_ref[...] = jnp.zeros_like(acc_ref)

    def issue(part):
        per = rps // GATHER_PARTS
        for t in range(part * per, (part + 1) * per):
            r = j * rps + t
            in_copy(idxn_ref[0, 0, r], nxt, r).start(priority=0)
            dst = jnp.where(i == 0, n_pairs + r, dstp_ref[0, 0, r])
            out_copy(nxt, r, dst).start(priority=1)

    def compute(with_issue):
        h = hb_ref[...]
        a = _dot(h, w1_ref[0])
        if with_issue:
            issue(0)
        b = _dot(h, w3_ref[0])
        if with_issue:
            issue(1)
        g = (_silu(a) * b).astype(BF16)
        if with_issue:
            issue(2)
        acc_ref[...] += jnp.dot(g, w2_ref[0].astype(BF16), preferred_element_type=F32)
        if with_issue:
            issue(3)

    last = j == n_ff - 1
    not_last = jnp.logical_not(last)

    @pl.when(live & not_last)
    def _():
        compute(True)

    @pl.when(live & last)
    def _():
        compute(False)

    @pl.when(jnp.logical_not(live) & not_last)
    def _():
        for part in range(GATHER_PARTS):
            issue(part)

    @pl.when(last)
    def _():
        wait_out(nxt)

    @pl.when(live & last)
    def _():
        out_ref[cur, 0:bm] = _rows_to_tiles(acc_ref[...])

    @pl.when((i == n_blk - 1) & last)
    def _():
        wait_in(nxt)

        def start(g, carry):
            for u in range(GATHER_PARTS):
                r = g * GATHER_PARTS + u
                out_copy(cur, r, dstc_ref[0, 0, r]).start()
            return carry

        lax.fori_loop(0, n_issue // GATHER_PARTS, start, 0)
        wait_out(cur)


def _experts(h, slot_tok, slot_pair, block_e, n_live, w1, w3, w2):
    n, n_tile, _ = h.shape
    d = n_tile * LANES
    d_ff = w1.shape[2]
    bm, tf = MOE_ROWS, FF_TILE
    n_blk = slot_tok.shape[0] // bm
    n_ff = d_ff // tf
    n_pairs = 2 * n
    rps = -(-bm // ((n_ff - 1) * SUBLANES)) * SUBLANES
    assert rps % GATHER_PARTS == 0
    n_issue = rps * (n_ff - 1)
    extra = n_issue - bm
    idx = jnp.pad(slot_tok.reshape(n_blk, 1, bm), ((0, 0), (0, 0), (0, extra)))
    dump = n_pairs + jnp.arange(n_issue, dtype=jnp.int32)
    dst = jnp.concatenate([slot_pair.reshape(n_blk, 1, bm), jnp.broadcast_to(dump[bm:], (n_blk, 1, extra))], axis=2)
    smem = lambda f: pl.BlockSpec((1, 1, n_issue), f, memory_space=pltpu.SMEM)
    grid_spec = pltpu.PrefetchScalarGridSpec(
        num_scalar_prefetch=2,
        grid=(n_blk, n_ff),
        in_specs=[smem(lambda i, j, be, nv: (0, 0, 0)),
                  smem(lambda i, j, be, nv: (jnp.minimum(i + 1, n_blk - 1), 0, 0)),
                  smem(lambda i, j, be, nv: (jnp.maximum(i - 1, 0), 0, 0)),
                  smem(lambda i, j, be, nv: (i, 0, 0)),
                  pl.BlockSpec(memory_space=pl.ANY),
                  pl.BlockSpec((1, d, tf), lambda i, j, be, nv: (be[i], 0, j)),
                  pl.BlockSpec((1, d, tf), lambda i, j, be, nv: (be[i], 0, j)),
                  pl.BlockSpec((1, tf, d), lambda i, j, be, nv: (be[i], j, 0))],
        out_specs=pl.BlockSpec(memory_space=pl.ANY),
        scratch_shapes=[pltpu.VMEM((2, n_issue, n_tile, LANES), F32), pltpu.VMEM((bm, d), BF16),
                        pltpu.VMEM((bm, d), F32), pltpu.VMEM((2, n_issue, n_tile, LANES), F32),
                        pltpu.SemaphoreType.DMA(()), pltpu.SemaphoreType.DMA(())])
    return pl.pallas_call(
        functools.partial(_expert_kernel, rows_per_step=rps, n_pairs=n_pairs),
        name="moe_experts",
        out_shape=jax.ShapeDtypeStruct((n_pairs + n_issue, n_tile, LANES), F32),
        grid_spec=grid_spec,
        compiler_params=_cparams(("arbitrary", "arbitrary")),
    )(block_e, n_live, idx, idx, dst, dst, h, w1, w3, w2)


def _combine_kernel(x_ref, m_ref, gt_ref, y0_ref, y1_ref, o_ref):
    gt = gt_ref[...]
    for s, (y0, y1) in enumerate(zip(_tiles_to_slabs(y0_ref[...]), _tiles_to_slabs(y1_ref[...]))):
        cols = slice(s * LANES, (s + 1) * LANES)
        y = gt[:, 0:1] * y0 + gt[:, 1:2] * y1
        o_ref[:, cols] = x_ref[:, cols] + m_ref[0, 5:6, cols] * y


def _combine(x, mod, gates, y, dims):
    n_lat, t_len, n_batch, d_a, d_b, d_c = dims
    n, d = x.shape
    tm = ROW_TILE
    n_tile = y.shape[1]
    grp = lambda i: (_mod_group(i, tm, n_lat, t_len, n_batch), 0, 0)
    return pl.pallas_call(
        _combine_kernel,
        name="moe_combine",
        out_shape=jax.ShapeDtypeStruct((n, d), F32),
        grid=(n // tm,),
        in_specs=[pl.BlockSpec((tm, d), lambda i: (i, 0)),
                  pl.BlockSpec((1, 6, d), grp),
                  pl.BlockSpec((tm, LANES), lambda i: (i, 0)),
                  pl.BlockSpec((tm, n_tile, LANES), lambda i: (i, 0, 0)),
                  pl.BlockSpec((tm, n_tile, LANES), lambda i: (i + n // tm, 0, 0))],
        out_specs=pl.BlockSpec((tm, d), lambda i: (i, 0)),
        compiler_params=_cparams(("parallel",)),
    )(x, mod, gates, y, y)


def _moe(x, mod, norm_g, w_router, b_router, w1, w3, w2, dims):
    n, d = x.shape
    wr_pad = jnp.zeros((d, LANES), F32).at[:, :N_EXPERTS].set(w_router.astype(F32))
    br_pad = jnp.zeros((1, LANES), F32).at[0, :N_EXPERTS].set(b_router.astype(F32))
    h, e_tile, g_tile = _route(x, mod, norm_g, wr_pad, br_pad, dims)
    bm = MOE_ROWS
    e_flat = e_tile[:, :2].reshape(-1)
    onehot = (e_flat[:, None] == jnp.arange(N_EXPERTS, dtype=jnp.int32)[None, :]).astype(jnp.int32)
    csum = jnp.cumsum(onehot, axis=0)
    counts = csum[-1]
    rank = jnp.sum(csum * onehot, axis=1) - 1
    padded = (counts + bm - 1) // bm * bm
    pad_end = jnp.cumsum(padded)
    pad_start = pad_end - padded
    slot = pad_start[e_flat] + rank
    n_blocks = (2 * n) // bm + N_EXPERTS
    pair_flat = jnp.arange(2 * n, dtype=jnp.int32)
    slot_ids = jnp.arange(n_blocks * bm, dtype=jnp.int32)
    slot_pair = (2 * n + slot_ids % bm).at[slot].set((pair_flat % 2) * n + pair_flat // 2)
    slot_tok = jnp.where(slot_pair < 2 * n, slot_pair % n, slot_ids % n)
    blk_start = jnp.arange(n_blocks, dtype=jnp.int32) * bm
    block_e = jnp.minimum(jnp.sum((pad_end[None, :] <= blk_start[:, None]).astype(jnp.int32), axis=1),
                          N_EXPERTS - 1).astype(jnp.int32)
    n_live = (pad_end[-1] // bm).astype(jnp.int32).reshape(1)
    y = _experts(h, slot_tok, slot_pair, block_e, n_live, w1, w3, w2)
    return _combine(x, mod, g_tile, y, dims)


def _rope_tables(t_len, n_heads):
    t = jnp.arange(t_len)
    row = (t // GRID_W).astype(F32)
    col = (t % GRID_W).astype(F32)
    n_freq = HEAD_DIM // 4
    inv = ROPE_BASE ** (-jnp.arange(n_freq, dtype=F32) / n_freq)
    ang = jnp.concatenate([row[:, None] * inv, col[:, None] * inv], axis=-1)
    cos, sin = jnp.cos(ang), jnp.sin(ang)
    cos_h = jnp.concatenate([cos, cos], axis=-1)
    sin_h = jnp.concatenate([-sin, sin], axis=-1)
    return jnp.tile(cos_h, (1, n_heads)), jnp.tile(sin_h, (1, n_heads))


def kernel(x, c, ctx, c_ctx, ada_w, ada_b, norm1_g, norm2_g, w_in, w_out, hgrn_lb_raw, hgrn_onorm_g,
           gdn_conv_w, gdn_a_log, gdn_dt_bias, gdn_onorm_g, na_qnorm_g, na_knorm_g, na_rpb, ffn_w1, ffn_w3,
           ffn_w2, moe_router_w, moe_router_b, moe_w1, moe_w3, moe_w2):
    n_batch, t_len, d = x.shape
    l_ctx = ctx.shape[1]
    depth = w_in.shape[0]
    d_a = hgrn_lb_raw.shape[-1]
    d_b = gdn_conv_w.shape[-1] // 3
    n_hb = gdn_a_log.shape[-1]
    n_hc = na_rpb.shape[1]
    d_c = n_hc * HEAD_DIM
    n_lat = n_batch * t_len
    n_all = n_lat + n_batch * l_ctx
    dims = (n_lat, t_len, n_batch, d_a, d_b, d_c)
    assert d_a % HEAD_DIM == 0 and d_b == n_hb * HEAD_DIM
    assert t_len % ROW_TILE == 0 and (n_batch * l_ctx) % ROW_TILE == 0 and l_ctx % TOK_BLK == 0
    assert t_len % GRID_W == 0 and 4 * n_hb <= LANES

    cvec = jnp.zeros((8, d), F32).at[:n_batch].set(c.astype(F32)).at[n_batch].set(c_ctx.astype(F32))
    mod_all = _adaln(cvec, ada_w, ada_b)

    lb_soft = jax.nn.softmax(hgrn_lb_raw.astype(F32), axis=1)
    lower_bound = jnp.cumsum(lb_soft, axis=1) - lb_soft[:, :1]
    cos_t, sin_t = _rope_tables(t_len, n_hb)
    n_gate = 4 * n_hb
    sizes_a, sizes_b = 5 * d_a, 4 * d_b

    xall = jnp.concatenate([x.reshape(n_lat, d), ctx.reshape(n_batch * l_ctx, d)], axis=0).astype(F32)
    for l in range(depth):
        last = l == depth - 1
        mod = mod_all[l, :n_batch + 1].reshape(n_batch + 1, 6, d)
        w = w_in[l]
        gate_cols = jnp.pad(w[:, sizes_a + sizes_b:sizes_a + sizes_b + n_gate], ((0, 0), (0, LANES - n_gate)))
        w_pad = jnp.concatenate([w[:, :sizes_a + sizes_b], gate_cols, w[:, sizes_a + sizes_b + n_gate:]],
                                axis=1).astype(BF16)
        qn_g = jnp.tile(na_qnorm_g[l].astype(F32), n_hc)[None]
        kn_g = jnp.tile(na_knorm_g[l].astype(F32), n_hc)[None]
        pa, pb, pg, qn, kn, vv = _inproj(xall, mod, norm1_g[l][None].astype(F32), w_pad, qn_g, kn_g, dims)

        oaf, oab = _hgrn_scan(pa, lower_bound[:, l][:, None, :], dims)
        conv_w = jnp.zeros((8, 3 * d_b), F32).at[:CONV_K].set(gdn_conv_w[l].astype(F32))
        alog_x = jnp.repeat(gdn_a_log[l].astype(F32), HEAD_DIM, axis=-1)[:, None, :]
        dtb_x = jnp.repeat(gdn_dt_bias[l].astype(F32), HEAD_DIM, axis=-1)[:, None, :]
        obf, obb = _gdn_scan(_gdn_prep(pb, pg, conv_w, alog_x, dtb_x, cos_t, sin_t, dims), dims)
        bias_tab = _natten_bias(na_rpb[l], t_len // GRID_W)
        oc = _natten(qn, kn, vv, bias_tab, dims)
        n_out = n_lat if last else n_all
        if not last:
            oc = jnp.concatenate([oc, _ctx_attn(qn, kn, vv, dims)], axis=0)
        na_g = jnp.tile(hgrn_onorm_g[l].astype(F32), d_a // HEAD_DIM)[None]
        nb_g = jnp.tile(gdn_onorm_g[l].astype(F32), n_hb)[None]
        xall_mid = _outproj(xall, mod, oaf, oab, pa, obf, obb, pb, oc, na_g, nb_g, w_out[l].astype(BF16), n_out,
                            dims)
        i = l // 2
        if l % 2 == 0:
            xall = _ffn(xall_mid, mod, norm2_g[l][None].astype(F32), ffn_w1[i].astype(BF16),
                        ffn_w3[i].astype(BF16), ffn_w2[i].astype(BF16), dims)
        else:
            xall = _moe(xall_mid, mod, norm2_g[l][None].astype(F32), moe_router_w[i], moe_router_b[i],
                        moe_w1[i], moe_w3[i], moe_w2[i], dims)
    return xall[:n_lat].reshape(n_batch, t_len, d).astype(x.dtype)
```

```python
import functools
import math

import jax
import jax.numpy as jnp
import numpy as np
from jax import lax
from jax.experimental import pallas as pl
from jax.experimental.pallas import tpu as pltpu

F32 = jnp.float32
BF16 = jnp.bfloat16
HI = lax.Precision.HIGHEST

EPS = 1e-6
HEAD_DIM = 64
CHUNK = 64
TOK_BLK = 256
GRID_W = 64
WIN_R = 8
WIN_C = 16
CONV_K = 5
ROPE_BASE = 10000.0
N_EXPERTS = 8
LANES = 128
SUBLANES = 8
ROW_TILE = 512
FFN_ROWS = 1024
FF_TILE = 512
FFN_FF_TILE = 896
MOE_ROWS = 1024
GATHER_PARTS = 4
NA_HEADS_PER_GROUP = 4
NA_ROWS_PER_STEP = 4
VMEM_LIMIT = 56 * 1024 * 1024
MXU_DEPTH = 256
NEG_BIG = -1e30


def _cparams(sem):
    return pltpu.CompilerParams(dimension_semantics=sem, vmem_limit_bytes=VMEM_LIMIT)


def _dot(a, b):
    return jnp.dot(a.astype(BF16), b.astype(BF16), preferred_element_type=F32)


def _dot_nt(a, b):
    return lax.dot_general(a.astype(BF16), b.astype(BF16), (((1,), (1,)), ((), ())),
                           preferred_element_type=F32)


def _dot_tn(a, b):
    return lax.dot_general(a.astype(BF16), b.astype(BF16), (((0,), (0,)), ((), ())),
                           preferred_element_type=F32)


def _dot_hi(a, b):
    return jnp.dot(a, b, precision=HI, preferred_element_type=F32)


def _iota(shape, dim):
    return lax.broadcasted_iota(jnp.int32, shape, dim)


def _head_mask(n):
    return (_iota((n, n), 0) // HEAD_DIM) == (_iota((n, n), 1) // HEAD_DIM)


def _head_sums(x, scale):
    n = x.shape[-1]
    w = min(n, MXU_DEPTH)
    gm = jnp.where(_head_mask(w), scale, 0.0).astype(BF16)
    hi = x.astype(BF16)
    lo = (x - hi.astype(F32)).astype(BF16)
    dot = lambda a: jnp.dot(a, gm, preferred_element_type=F32)
    parts = [dot(lo[:, s:s + w]) + dot(hi[:, s:s + w]) for s in range(0, n, w)]
    return parts[0] if len(parts) == 1 else jnp.concatenate(parts, axis=1)


def _group_mean_sq(x):
    return _head_sums(x * x, 1.0 / HEAD_DIM)


def _silu(x):
    return x * jax.nn.sigmoid(x)


def _modulated_norm(x, g, shift, scale):
    ms = jnp.mean(x * x, axis=-1, keepdims=True)
    y = x * lax.rsqrt(ms + EPS) * g
    return y * (1.0 + scale) + shift


def _mod_group(i, tile, n_lat, t_len, n_batch):
    return jnp.where(i * tile < n_lat, (i * tile) // t_len, n_batch)


def _adaln_kernel(c_ref, w_ref, b_ref, o_ref):
    s = _silu(c_ref[...])
    o_ref[0] = _dot(s, w_ref[0]) + b_ref[0]


def _adaln(cvec, ada_w, ada_b):
    depth, d, d6 = ada_w.shape
    tn = 1024
    return pl.pallas_call(
        _adaln_kernel,
        name="adaln",
        out_shape=jax.ShapeDtypeStruct((depth, 8, d6), F32),
        grid=(depth, d6 // tn),
        in_specs=[pl.BlockSpec((8, d), lambda l, j: (0, 0)),
                  pl.BlockSpec((1, d, tn), lambda l, j: (l, 0, j)),
                  pl.BlockSpec((1, 1, tn), lambda l, j: (l, 0, j))],
        out_specs=pl.BlockSpec((1, 8, tn), lambda l, j: (l, 0, j)),
        compiler_params=_cparams(("parallel", "parallel")),
    )(cvec, ada_w, ada_b.reshape(depth, 1, d6))


def _inproj_kernel(x_ref, m_ref, g_ref, wab_ref, wg_ref, wc_ref, qg_ref, kg_ref,
                   pa_ref, pb_ref, pg_ref, qn_ref, kn_ref, vv_ref, *, d_a5, d_b4, d_c):
    h = _modulated_norm(x_ref[...], g_ref[...], m_ref[0, 0:1, :], m_ref[0, 1:2, :]).astype(BF16)
    pa_ref[...] = jnp.dot(h, wab_ref[:, 0:d_a5], preferred_element_type=F32)
    pb_ref[...] = jnp.dot(h, wab_ref[:, d_a5:d_a5 + d_b4], preferred_element_type=F32)
    pg_ref[...] = jnp.dot(h, wg_ref[...], preferred_element_type=F32)
    q = jnp.dot(h, wc_ref[:, 0:d_c], preferred_element_type=F32)
    q = q * lax.rsqrt(_group_mean_sq(q) + EPS) * qg_ref[...]
    qn_ref[...] = (q * HEAD_DIM ** -0.5).astype(BF16)
    k = jnp.dot(h, wc_ref[:, d_c:2 * d_c], preferred_element_type=F32)
    k = k * lax.rsqrt(_group_mean_sq(k) + EPS) * kg_ref[...]
    kn_ref[...] = k.astype(BF16)
    vv_ref[...] = jnp.dot(h, wc_ref[:, 2 * d_c:3 * d_c], preferred_element_type=F32).astype(BF16)


def _inproj(xall, mod, norm_g, w_ab, w_g, w_c, qn_g, kn_g, dims):
    n_all, d = xall.shape
    n_lat, t_len, n_batch, d_a, d_b, d_c = dims
    tm = ROW_TILE
    row = lambda i: (i, 0)
    const = lambda i: (0, 0)
    grp = lambda i: (_mod_group(i, tm, n_lat, t_len, n_batch), 0, 0)
    kern = functools.partial(_inproj_kernel, d_a5=5 * d_a, d_b4=4 * d_b, d_c=d_c)
    return pl.pallas_call(
        kern,
        name="inproj",
        out_shape=(jax.ShapeDtypeStruct((n_all, 5 * d_a), F32),
                   jax.ShapeDtypeStruct((n_all, 4 * d_b), F32),
                   jax.ShapeDtypeStruct((n_all, LANES), F32),
                   jax.ShapeDtypeStruct((n_all, d_c), BF16),
                   jax.ShapeDtypeStruct((n_all, d_c), BF16),
                   jax.ShapeDtypeStruct((n_all, d_c), BF16)),
        grid=(n_all // tm,),
        in_specs=[pl.BlockSpec((tm, d), row),
                  pl.BlockSpec((1, 6, d), grp),
                  pl.BlockSpec((1, d), const),
                  pl.BlockSpec(w_ab.shape, const),
                  pl.BlockSpec(w_g.shape, const),
                  pl.BlockSpec(w_c.shape, const),
                  pl.BlockSpec((1, d_c), const),
                  pl.BlockSpec((1, d_c), const)],
        out_specs=(pl.BlockSpec((tm, 5 * d_a), row), pl.BlockSpec((tm, 4 * d_b), row),
                   pl.BlockSpec((tm, LANES), row), pl.BlockSpec((tm, d_c), row),
                   pl.BlockSpec((tm, d_c), row), pl.BlockSpec((tm, d_c), row)),
        compiler_params=_cparams(("parallel",)),
    )(xall, mod, norm_g, w_ab, w_g, w_c, qn_g, kn_g)


def _scan_block_index(b, d, j, n_lat_blk, n_ctx_blk, n_batch):
    jc = jnp.where(d == 0, j, n_ctx_blk - 1 - j)
    jl = jnp.where(d == 0, j - n_ctx_blk, n_lat_blk - 1 - (j - n_ctx_blk))
    return jnp.where(j < n_ctx_blk, n_batch * n_lat_blk + b * n_ctx_blk + jc, b * n_lat_blk + jl)


def _block_diag(x, mask_bd, n_rep):
    xb = x.astype(BF16)
    return jnp.where(mask_bd, jnp.concatenate([xb] * n_rep, axis=0), jnp.zeros((), BF16))


def _scan_order(idx, d):
    return idx + d * (CHUNK - 1 - 2 * idx)


def _split3(x):
    hi = x.astype(BF16)
    r = x - hi.astype(F32)
    mid = r.astype(BF16)
    lo = (r - mid.astype(F32)).astype(BF16)
    return hi, mid, lo


def _sel_dot(sel, x):
    hi, mid, lo = _split3(x)
    dot = lambda a: jnp.dot(sel, a, preferred_element_type=F32)
    return (dot(lo) + dot(mid)) + dot(hi)


def _dot_sel(x, sel):
    hi, mid, lo = _split3(x)
    dot = lambda a: jnp.dot(a, sel, preferred_element_type=F32)
    return (dot(lo) + dot(mid)) + dot(hi)


HGRN_LEVELS = tuple(CHUNK >> (i + 1) for i in range(int(math.log2(CHUNK))))


def _hgrn_kernel(qf_ref, zf_ref, vf_ref, qb_ref, zb_ref, vb_ref, lb_ref, of_ref, ob_ref, st_ref, *, n_chunk):
    ins = ((qf_ref, zf_ref, vf_ref, of_ref), (qb_ref, zb_ref, vb_ref, ob_ref))
    j = pl.program_id(1)
    n = qf_ref.shape[-1]
    n_rep = n // HEAD_DIM

    @pl.when(j == 0)
    def _():
        st_ref[...] = jnp.zeros_like(st_ref)

    mask_bd = _head_mask(n)
    gones = mask_bd.astype(BF16)
    bd = lambda a: _block_diag(a, mask_bd, n_rep)

    work = []
    masks = []
    for dd in range(2):
        q_ref, z_ref, v_ref, _ = ins[dd]
        lb = lb_ref[dd]
        s_t = _scan_order(_iota((CHUNK, CHUNK), 0), dd)
        s_u = _scan_order(_iota((CHUNK, CHUNK), 1), dd)
        sets = [s_u <= s_t, s_u > s_t]
        for m in HGRN_LEVELS:
            ref = (s_t // (2 * m)) * (2 * m) + m - 1
            sets.append((s_u > jnp.minimum(s_t, ref)) & (s_u <= jnp.maximum(s_t, ref)))
        sel = jnp.concatenate([jnp.where(a, 1.0, 0.0).astype(BF16) for a in sets], axis=0)
        s_row = _scan_order(_iota((CHUNK, n), 0), dd)
        s_col = _scan_order(_iota((CHUNK, n), 1) % HEAD_DIM, dd)
        masks.append((s_row == s_col, [(s_row // m) % 2 == 1 for m in HGRN_LEVELS],
                      [(s_row // (2 * m)) == (s_col // (2 * m)) for m in HGRN_LEVELS]))
        for c in range(n_chunk):
            cc = c if dd == 0 else n_chunk - 1 - c
            r = slice(cc * CHUNK, (cc + 1) * CHUNK)
            q = q_ref[r, :] * HEAD_DIM ** -0.5
            z = z_ref[r, :]
            v = v_ref[r, :]
            log_f = (jnp.minimum(z, 0.0) - jnp.log1p(jnp.exp(-jnp.abs(z)))) + jnp.log1p(lb * jnp.exp(-z))
            k = (1.0 - lb) * jax.nn.sigmoid(-z)
            work.append((dd, r, q, k, v, _sel_dot(sel, log_f)))
    atts = [jnp.where(masks[dd][0], jnp.dot((q * k).astype(BF16), gones, preferred_element_type=F32), 0.0)
            for dd, _, q, k, _, _ in work]
    for lvl, m in enumerate(HGRN_LEVELS):
        for i, (dd, _, q, k, _, sums) in enumerate(work):
            second = masks[dd][1][lvl]
            e = jnp.exp(sums[(2 + lvl) * CHUNK:(3 + lvl) * CHUNK])
            a_m = _dot_nt(jnp.where(second, q * e, 0.0), bd(jnp.where(second, 0.0, k * e)))
            if 2 * m < CHUNK:
                a_m = jnp.where(masks[dd][2][lvl], a_m, 0.0)
            atts[i] = atts[i] + a_m
    pre = []
    for (dd, r, q, k, v, sums), att in zip(work, atts):
        b_incl = sums[0:CHUNK]
        b_after = sums[CHUNK:2 * CHUNK]
        e_end = jnp.exp(b_incl[0:1, :] + b_after[0:1, :])
        upd = jnp.where(mask_bd, _dot_tn(v, k * jnp.exp(b_after)), 0.0)
        pre.append((dd, r, _dot(att, bd(v)), (q * jnp.exp(b_incl)).astype(BF16), e_end, upd))

    sts = [st_ref[0], st_ref[1]]
    for c in range(n_chunk):
        for dd in range(2):
            _, r, o_intra, q_hat, e_end, upd = pre[dd * n_chunk + c]
            ins[dd][3][r, :] = o_intra + _dot_nt(q_hat, sts[dd])
            sts[dd] = sts[dd] * e_end + upd
    st_ref[0] = sts[0]
    st_ref[1] = sts[1]


def _hgrn_scan(pa, lb2, dims):
    n_all = pa.shape[0]
    n_lat, t_len, n_batch, d_a, d_b, d_c = dims
    n_lat_blk = t_len // TOK_BLK
    n_ctx_blk = (n_all - n_lat) // n_batch // TOK_BLK
    blk = functools.partial(_scan_block_index, n_lat_blk=n_lat_blk, n_ctx_blk=n_ctx_blk, n_batch=n_batch)
    kern = functools.partial(_hgrn_kernel, n_chunk=TOK_BLK // CHUNK)
    in_specs = []
    for dd in range(2):
        in_specs += [pl.BlockSpec((TOK_BLK, d_a), lambda b, j, dd=dd: (blk(b, dd, j), 0)),
                     pl.BlockSpec((TOK_BLK, d_a), lambda b, j, dd=dd: (blk(b, dd, j), 1 + dd)),
                     pl.BlockSpec((TOK_BLK, d_a), lambda b, j, dd=dd: (blk(b, dd, j), 3))]
    in_specs.append(pl.BlockSpec((2, 1, d_a), lambda b, j: (0, 0, 0)))
    out_specs = tuple(pl.BlockSpec((TOK_BLK, d_a), lambda b, j, dd=dd: (blk(b, dd, j), 0)) for dd in range(2))
    return pl.pallas_call(
        kern,
        name="hgrn_scan",
        out_shape=(jax.ShapeDtypeStruct((n_all, d_a), F32), jax.ShapeDtypeStruct((n_all, d_a), F32)),
        grid=(n_batch, n_lat_blk + n_ctx_blk),
        in_specs=in_specs,
        out_specs=out_specs,
        scratch_shapes=[pltpu.VMEM((2, d_a, d_a), F32)],
        compiler_params=_cparams(("parallel", "arbitrary")),
    )(pa, pa, pa, pa, pa, pa, lb2)


def _gdn_prep_kernel(x_ref, prev_ref, next_ref, pg_ref, cw_ref, alog_ref, dtb_ref, cos_ref, sin_ref,
                     w_ref, u0_ref, qh_ref, qg_ref, kh_ref, ee_ref,
                     *, n_lat_blk, n_ctx_blk, n_lat_blks_total, d_b):
    i = pl.program_id(0)
    is_lat = i < n_lat_blks_total
    pos = jnp.where(is_lat, i % n_lat_blk, (i - n_lat_blks_total) % n_ctx_blk)
    n_seq_blk = jnp.where(is_lat, n_lat_blk, n_ctx_blk)
    first = pos == 0
    last = pos == n_seq_blk - 1
    x = x_ref[...]
    halo = prev_ref.shape[0]
    prev = jnp.where(first, 0.0, prev_ref[...])
    nxt = jnp.where(last, 0.0, next_ref[...])
    rows = _iota(x.shape, 0)
    tb = x.shape[0]
    half = CONV_K // 2
    acc = x * cw_ref[half:half + 1, :]
    for s in range(1, half + 1):
        xs = pltpu.roll(x, s, 0)
        for r in range(s):
            xs = jnp.where(rows == r, prev[halo - s + r:halo - s + r + 1, :], xs)
        acc = acc + xs * cw_ref[half - s:half - s + 1, :]
        xs = pltpu.roll(x, tb - s, 0)
        for r in range(s):
            xs = jnp.where(rows == tb - s + r, nxt[r:r + 1, :], xs)
        acc = acc + xs * cw_ref[half + s:half + s + 1, :]
    y = _silu(acc)
    def l2n(a):
        return a * lax.rsqrt(_head_sums(a * a, 1.0) + EPS)

    lane = _iota((tb, d_b), 1) % HEAD_DIM
    lo = lane < HEAD_DIM // 2

    def rope(a):
        partner = jnp.where(lo, pltpu.roll(a, d_b - HEAD_DIM // 2, 1), pltpu.roll(a, HEAD_DIM // 2, 1))
        return jnp.where(is_lat, a * cos_ref[...] + partner * sin_ref[...], a)

    q_all = rope(l2n(y[:, 0:d_b])) * HEAD_DIM ** -0.5
    k_all = rope(l2n(y[:, d_b:2 * d_b]))
    v_all = y[:, 2 * d_b:3 * d_b]
    n_h = d_b // HEAD_DIM
    g = pg_ref[...]
    e_r = _iota((LANES, d_b), 0)
    e_c = _iota((LANES, d_b), 1) // HEAD_DIM
    mask_bd = _head_mask(d_b)
    bd = lambda a: _block_diag(a, mask_bd, n_h)
    ones_cc = jnp.ones((CHUNK, CHUNK), BF16)
    ee_ref[...] = jnp.zeros_like(ee_ref)
    work = []
    t_invs = []
    for dd in range(2):
        a_x = _dot_sel(g, jnp.where(e_r == dd * n_h + e_c, 1.0, 0.0).astype(BF16))
        b_x = _dot_sel(g, jnp.where(e_r == (2 + dd) * n_h + e_c, 1.0, 0.0).astype(BF16))
        t = a_x + dtb_ref[dd]
        softplus = jnp.maximum(t, 0.0) + jnp.log1p(jnp.exp(-jnp.abs(t)))
        la_all = -jnp.exp(alog_ref[dd]) * softplus
        be_all = jax.nn.sigmoid(b_x)
        s_t = _scan_order(_iota((CHUNK, CHUNK), 0), dd)
        s_u = _scan_order(_iota((CHUNK, CHUNK), 1), dd)
        m_incl = jnp.where(s_u <= s_t, 1.0, 0.0).astype(BF16)
        s_row = _scan_order(_iota((CHUNK, d_b), 0), dd)
        s_col = _scan_order(_iota((CHUNK, d_b), 1) % HEAD_DIM, dd)
        incl = s_col <= s_row
        strict = s_col < s_row
        m_before = jnp.where(s_row <= s_col, 1.0, 0.0)
        eye_f = jnp.where(s_col == s_row, 1.0, 0.0)
        pair = (s_row // 2) == (s_col // 2)
        levels = []
        m = 4
        while m <= CHUNK:
            levels.append(((s_row // m) == (s_col // m)) & ((s_row // (m // 2)) != (s_col // (m // 2))))
            m *= 2
        end_row = CHUNK - 1 if dd == 0 else 0
        for c in range(tb // CHUNK):
            r = slice(c * CHUNK, (c + 1) * CHUNK)
            q, k, v, la, be = q_all[r], k_all[r], v_all[r], la_all[r], be_all[r]
            g_t = _sel_dot(m_incl, la)
            g_s = _sel_dot(ones_cc, la * m_before)
            gam = jnp.where(incl, jnp.exp(jnp.minimum(g_t - g_s, 0.0)), 0.0)
            k_bd = bd(k)
            kk = _dot_nt(k, k_bd)
            qk = _dot_nt(q, k_bd)
            a = jnp.where(strict, be * kk * gam, 0.0)
            eg = jnp.exp(g_t)
            g_end = g_t[end_row:end_row + 1, :]
            qh_ref[dd, r, :] = (q * eg).astype(BF16)
            qg_ref[dd, r, :] = (qk * gam).astype(BF16)
            kh_ref[dd, r, :] = (k * jnp.exp(g_end - g_t)).astype(BF16)
            ee_ref[dd, 0, c:c + 1, :] = jnp.exp(g_end)
            work.append((dd, r, a, bd(be * eg * k), bd(be * v), levels))
            t_invs.append(eye_f - jnp.where(pair, a, 0.0))
    for lv in range(int(math.log2(CHUNK)) - 1):
        xs = [_dot(jnp.where(wk[5][lv], wk[2], 0.0), bd(t)) for wk, t in zip(work, t_invs)]
        t_invs = [t - _dot(t, bd(x)) for t, x in zip(t_invs, xs)]
    for (dd, r, _, wk_bd, vb_bd, _), t in zip(work, t_invs):
        w_ref[dd, r, :] = _dot(t, wk_bd).astype(BF16)
        u0_ref[dd, r, :] = _dot(t, vb_bd)


def _gdn_prep(pb, pg, conv_w, alog_x, dtb_x, cos_t, sin_t, dims):
    n_all = pb.shape[0]
    n_lat, t_len, n_batch, d_a, d_b, d_c = dims
    n_lat_blk = t_len // TOK_BLK
    n_ctx_blk = (n_all - n_lat) // n_batch // TOK_BLK
    n_blk = n_all // TOK_BLK
    halo = 8
    per = TOK_BLK // halo
    kern = functools.partial(_gdn_prep_kernel, n_lat_blk=n_lat_blk, n_ctx_blk=n_ctx_blk,
                             n_lat_blks_total=n_lat // TOK_BLK, d_b=d_b)
    row = lambda i: (i, 0)
    const2 = lambda i: (0, 0)
    const3 = lambda i: (0, 0, 0)
    tab = lambda i: (jnp.where(i < n_lat // TOK_BLK, i % n_lat_blk, 0), 0)
    both = pl.BlockSpec((2, TOK_BLK, d_b), lambda i: (0, i, 0))
    sds = jax.ShapeDtypeStruct
    return pl.pallas_call(
        kern,
        name="gdn_prep",
        out_shape=(sds((2, n_all, d_b), BF16), sds((2, n_all, d_b), F32), sds((2, n_all, d_b), BF16),
                   sds((2, n_all, d_b), BF16), sds((2, n_all, d_b), BF16), sds((2, n_blk, 8, d_b), F32)),
        grid=(n_blk,),
        in_specs=[pl.BlockSpec((TOK_BLK, 3 * d_b), row),
                  pl.BlockSpec((halo, 3 * d_b), lambda i: (jnp.maximum(i * per - 1, 0), 0)),
                  pl.BlockSpec((halo, 3 * d_b), lambda i: (jnp.minimum((i + 1) * per, n_blk * per - 1), 0)),
                  pl.BlockSpec((TOK_BLK, LANES), row),
                  pl.BlockSpec((8, 3 * d_b), const2),
                  pl.BlockSpec((2, 1, d_b), const3),
                  pl.BlockSpec((2, 1, d_b), const3),
                  pl.BlockSpec((TOK_BLK, d_b), tab),
                  pl.BlockSpec((TOK_BLK, d_b), tab)],
        out_specs=(both, both, both, both, both, pl.BlockSpec((2, 1, 8, d_b), lambda i: (0, i, 0, 0))),
        compiler_params=_cparams(("parallel",)),
    )(pb, pb, pb, pg, conv_w, alog_x, dtb_x, cos_t, sin_t)


def _gdn_kernel(*refs, n_chunk, n_batch, blk):
    n_chain = 2 * n_batch
    ins = [refs[6 * c:6 * c + 6] for c in range(n_chain)]
    o_hbm = refs[6 * n_chain:6 * n_chain + 2]
    st_ref, stage_ref, sem = refs[6 * n_chain + 2:]
    j = pl.program_id(0)
    n = stage_ref.shape[-1]
    n_rep = n // HEAD_DIM

    def out_copy(c, step):
        row0 = pl.multiple_of(blk(c // 2, c % 2, step) * TOK_BLK, TOK_BLK)
        return pltpu.make_async_copy(stage_ref.at[c], o_hbm[c % 2].at[pl.ds(row0, TOK_BLK)], sem.at[c])

    @pl.when(j == 0)
    def _():
        st_ref[...] = jnp.zeros_like(st_ref)

    @pl.when(j > 0)
    def _():
        for c in range(n_chain):
            out_copy(c, j - 1).wait()

    mask_bd = _head_mask(n)
    bd = lambda a: _block_diag(a, mask_bd, n_rep)
    sts = [st_ref[c] for c in range(n_chain)]
    for ck in range(n_chunk):
        ccs = [ck if c % 2 == 0 else n_chunk - 1 - ck for c in range(n_chain)]
        rs = [slice(cc * CHUNK, (cc + 1) * CHUNK) for cc in ccs]
        us = [ins[c][1][0, rs[c], :] - _dot_nt(ins[c][0][0, rs[c], :], sts[c]) for c in range(n_chain)]
        o_state = [_dot_nt(ins[c][2][0, rs[c], :], sts[c]) for c in range(n_chain)]
        u_bds = [bd(u) for u in us]
        for c in range(n_chain):
            stage_ref[c, rs[c], :] = o_state[c] + _dot(ins[c][3][0, rs[c], :], u_bds[c])
        upds = [jnp.where(mask_bd, _dot_tn(us[c], ins[c][4][0, rs[c], :]), 0.0) for c in range(n_chain)]
        sts = [sts[c] * ins[c][5][0, 0, ccs[c]:ccs[c] + 1, :] + upds[c] for c in range(n_chain)]
    for c in range(n_chain):
        st_ref[c] = sts[c]
        out_copy(c, j).start()

    @pl.when(j == pl.num_programs(0) - 1)
    def _():
        for c in range(n_chain):
            out_copy(c, j).wait()


def _gdn_scan(wy, dims):
    n_all = wy[0].shape[1]
    n_lat, t_len, n_batch, d_a, d_b, d_c = dims
    n_lat_blk = t_len // TOK_BLK
    n_ctx_blk = (n_all - n_lat) // n_batch // TOK_BLK
    blk = functools.partial(_scan_block_index, n_lat_blk=n_lat_blk, n_ctx_blk=n_ctx_blk, n_batch=n_batch)
    kern = functools.partial(_gdn_kernel, n_chunk=TOK_BLK // CHUNK, n_batch=n_batch, blk=blk)
    in_specs = []
    for b in range(n_batch):
        for dd in range(2):
            tok = pl.BlockSpec((1, TOK_BLK, d_b), lambda j, b=b, dd=dd: (dd, blk(b, dd, j), 0))
            in_specs += [tok] * 5 + [pl.BlockSpec((1, 1, 8, d_b), lambda j, b=b, dd=dd: (dd, blk(b, dd, j), 0, 0))]
    n_chain = 2 * n_batch
    return pl.pallas_call(
        kern,
        name="gdn_scan",
        out_shape=(jax.ShapeDtypeStruct((n_all, d_b), F32), jax.ShapeDtypeStruct((n_all, d_b), F32)),
        grid=(n_lat_blk + n_ctx_blk,),
        in_specs=in_specs,
        out_specs=(pl.BlockSpec(memory_space=pl.ANY), pl.BlockSpec(memory_space=pl.ANY)),
        scratch_shapes=[pltpu.VMEM((n_chain, d_b, d_b), F32), pltpu.VMEM((n_chain, TOK_BLK, d_b), F32),
                        pltpu.SemaphoreType.DMA((n_chain,))],
        compiler_params=_cparams(("arbitrary",)),
    )(*(list(wy) * n_chain))


def _stack_heads(q, n_rep):
    m, n = q.shape
    keep = (_iota((n_rep * m, n), 0) // m) == (_iota((n_rep * m, n), 1) // HEAD_DIM)
    return jnp.where(keep, jnp.concatenate([q] * n_rep, axis=0), jnp.zeros((), q.dtype))


def _fold_heads(o, m, n_rep):
    n = o.shape[1]
    lane_h = _iota((m, n), 1) // HEAD_DIM
    acc = jnp.zeros((m, n), F32)
    for h in range(n_rep):
        acc = acc + jnp.where(lane_h == h, o[h * m:(h + 1) * m, :], 0.0)
    return acc


def _natten_kernel(*refs, n_rows):
    q_ref, k_ref, v_ref, kc_ref, vc_ref = refs[:5]
    bias_refs = refs[5:-1]
    o_ref = refs[-1]
    step = pl.program_id(1)
    n = q_ref.shape[-1]
    kr = min(WIN_R, n_rows)
    dn = (((1,), (1,)), ((), ()))
    gw = NA_HEADS_PER_GROUP * HEAD_DIM
    n_stack = NA_HEADS_PER_GROUP * GRID_W
    units = []
    for rr, bias_ref in enumerate(bias_refs):
        r = step * len(bias_refs) + rr
        r0 = jnp.clip(r - WIN_R // 2, 0, n_rows - kr)
        sl = pl.ds(pl.multiple_of(r0 * GRID_W, GRID_W), kr * GRID_W)
        q_rows = slice(rr * GRID_W, (rr + 1) * GRID_W)
        for g in range(n // gw):
            lanes = slice(g * gw, (g + 1) * gw)
            qs = _stack_heads(q_ref[q_rows, lanes], NA_HEADS_PER_GROUP)
            s_loc = lax.dot_general(qs, k_ref[sl, lanes], dn, preferred_element_type=F32)
            s_loc = s_loc + bias_ref[0, g * n_stack:(g + 1) * n_stack, :]
            s_ctx = lax.dot_general(qs, kc_ref[:, lanes], dn, preferred_element_type=F32)
            units.append((q_rows, lanes, sl, s_loc, s_ctx))
    probs = []
    for q_rows, lanes, sl, s_loc, s_ctx in units:
        m = jnp.maximum(jnp.max(s_loc, axis=-1, keepdims=True), jnp.max(s_ctx, axis=-1, keepdims=True))
        p_loc = jnp.exp(s_loc - m)
        p_ctx = jnp.exp(s_ctx - m)
        inv = 1.0 / (jnp.sum(p_loc, axis=-1, keepdims=True) + jnp.sum(p_ctx, axis=-1, keepdims=True))
        probs.append(((p_loc * inv).astype(BF16), (p_ctx * inv).astype(BF16)))
    for (q_rows, lanes, sl, _, _), (p_loc, p_ctx) in zip(units, probs):
        o = jnp.dot(p_loc, v_ref[sl, lanes], preferred_element_type=F32)
        o = o + jnp.dot(p_ctx, vc_ref[:, lanes], preferred_element_type=F32)
        o_ref[q_rows, lanes] = _fold_heads(o, GRID_W, NA_HEADS_PER_GROUP)


def _natten(qn, kn, vv, bias_tab, dims):
    n_lat, t_len, n_batch, d_a, d_b, d_c = dims
    n_all = qn.shape[0]
    l_ctx = (n_all - n_lat) // n_batch
    n_rows = t_len // GRID_W
    kr = min(WIN_R, n_rows)
    ctx0 = n_lat // l_ctx
    rb = NA_ROWS_PER_STEP
    assert n_rows % rb == 0
    n_steps = n_rows // rb

    def cfg(rr):
        def index(b, i):
            r = i * rb + rr
            r0 = jnp.clip(r - WIN_R // 2, 0, n_rows - kr)
            return (r - r0, 0, 0)
        return index

    kern = functools.partial(_natten_kernel, n_rows=n_rows)
    return pl.pallas_call(
        kern,
        name="natten",
        out_shape=jax.ShapeDtypeStruct((n_lat, d_c), F32),
        grid=(n_batch, n_steps),
        in_specs=[pl.BlockSpec((rb * GRID_W, d_c), lambda b, i: (b * n_steps + i, 0)),
                  pl.BlockSpec((t_len, d_c), lambda b, i: (b, 0)),
                  pl.BlockSpec((t_len, d_c), lambda b, i: (b, 0)),
                  pl.BlockSpec((l_ctx, d_c), lambda b, i: (ctx0 + b, 0)),
                  pl.BlockSpec((l_ctx, d_c), lambda b, i: (ctx0 + b, 0))]
                 + [pl.BlockSpec((1,) + bias_tab.shape[1:], cfg(rr)) for rr in range(rb)],
        out_specs=pl.BlockSpec((rb * GRID_W, d_c), lambda b, i: (b * n_steps + i, 0)),
        compiler_params=_cparams(("parallel", "arbitrary")),
    )(qn, kn, vv, kn, vv, *([bias_tab] * rb))


def _ctx_attn_kernel(q_ref, k_ref, v_ref, o_ref):
    n = q_ref.shape[-1]
    n_rep = n // HEAD_DIM
    qs = _stack_heads(q_ref[...], n_rep)
    s = lax.dot_general(qs, k_ref[...], (((1,), (1,)), ((), ())), preferred_element_type=F32)
    p = jnp.exp(s - jnp.max(s, axis=-1, keepdims=True))
    p = p * (1.0 / jnp.sum(p, axis=-1, keepdims=True))
    o = jnp.dot(p.astype(BF16), v_ref[...], preferred_element_type=F32)
    o_ref[...] = _fold_heads(o, q_ref.shape[0], n_rep)


def _ctx_attn(qn, kn, vv, dims):
    n_lat, t_len, n_batch, d_a, d_b, d_c = dims
    n_all = qn.shape[0]
    l_ctx = (n_all - n_lat) // n_batch
    tq = 64
    per = l_ctx // tq
    q0 = n_lat // tq
    c0 = n_lat // l_ctx
    return pl.pallas_call(
        _ctx_attn_kernel,
        name="ctx_attn",
        out_shape=jax.ShapeDtypeStruct((n_all - n_lat, d_c), F32),
        grid=(n_batch, per),
        in_specs=[pl.BlockSpec((tq, d_c), lambda b, i: (q0 + b * per + i, 0)),
                  pl.BlockSpec((l_ctx, d_c), lambda b, i: (c0 + b, 0)),
                  pl.BlockSpec((l_ctx, d_c), lambda b, i: (c0 + b, 0))],
        out_specs=pl.BlockSpec((tq, d_c), lambda b, i: (b * per + i, 0)),
        compiler_params=_cparams(("parallel", "arbitrary")),
    )(qn, kn, vv)


def _natten_bias(rpb, n_rows):
    n_h = rpb.shape[0]
    kr = min(WIN_R, n_rows)
    cols = jnp.arange(GRID_W)
    c0 = jnp.clip(cols - WIN_C // 2, 0, GRID_W - WIN_C)
    kc = jnp.arange(GRID_W)
    in_win = (kc[None, :] >= c0[:, None]) & (kc[None, :] < c0[:, None] + WIN_C)
    per = 2 * GRID_W
    rp = rpb.astype(F32)
    u = jnp.concatenate([rp[..., WIN_C - 1:], jnp.zeros(rp.shape[:-1] + (per - 2 * WIN_C + 1,), F32),
                         rp[..., :WIN_C - 1]], axis=-1)
    rel = jnp.tile(u, (1, 1, GRID_W))[..., :GRID_W * (per - 1)]
    rel = rel.reshape(rp.shape[:-1] + (GRID_W, per - 1))[..., :GRID_W]
    toe = jnp.where(in_win[None, None], rel, NEG_BIG)
    tabs = jnp.stack([toe[:, WIN_R - 1 - delta:WIN_R - 1 - delta + kr] for delta in range(kr)])
    return tabs.transpose(0, 1, 3, 2, 4).reshape(kr, n_h * GRID_W, kr * GRID_W)


def _outproj_kernel(x_ref, m_ref, oaf_ref, oab_ref, ga_ref, obf_ref, obb_ref, gb_ref, oc_ref, na_ref, nb_ref, w_ref,
                    o_ref, *, d_a, d_b):
    oa = oaf_ref[...] + oab_ref[...]
    ya = oa * lax.rsqrt(_group_mean_sq(oa) + EPS) * na_ref[...] * _silu(ga_ref[...])
    ob = obf_ref[...] + obb_ref[...]
    yb = ob * lax.rsqrt(_group_mean_sq(ob) + EPS) * nb_ref[...] * _silu(gb_ref[...])
    acc = _dot(ya, w_ref[0:d_a, :])
    acc = acc + _dot(yb, w_ref[d_a:d_a + d_b, :])
    acc = acc + _dot(oc_ref[...], w_ref[d_a + d_b:, :])
    o_ref[...] = x_ref[...] + m_ref[0, 2:3, :] * acc


def _outproj(xall, mod, oaf, oab, pa, obf, obb, pb, oc, na_g, nb_g, w_out, n_rows_out, dims):
    n_lat, t_len, n_batch, d_a, d_b, d_c = dims
    d = xall.shape[1]
    tm = ROW_TILE
    row = lambda i: (i, 0)
    const = lambda i: (0, 0)
    grp = lambda i: (_mod_group(i, tm, n_lat, t_len, n_batch), 0, 0)
    kern = functools.partial(_outproj_kernel, d_a=d_a, d_b=d_b)
    return pl.pallas_call(
        kern,
        name="outproj",
        out_shape=jax.ShapeDtypeStruct((n_rows_out, d), F32),
        grid=(n_rows_out // tm,),
        in_specs=[pl.BlockSpec((tm, d), row),
                  pl.BlockSpec((1, 6, d), grp),
                  pl.BlockSpec((tm, d_a), row),
                  pl.BlockSpec((tm, d_a), row),
                  pl.BlockSpec((tm, d_a), lambda i: (i, 4)),
                  pl.BlockSpec((tm, d_b), row),
                  pl.BlockSpec((tm, d_b), row),
                  pl.BlockSpec((tm, d_b), lambda i: (i, 3)),
                  pl.BlockSpec((tm, d_c), row),
                  pl.BlockSpec((1, d_a), const),
                  pl.BlockSpec((1, d_b), const),
                  pl.BlockSpec(w_out.shape, const)],
        out_specs=pl.BlockSpec((tm, d), row),
        compiler_params=_cparams(("parallel",)),
    )(xall, mod, oaf, oab, pa, obf, obb, pb, oc, na_g, nb_g, w_out)


def _ffn_kernel(x_ref, m_ref, g_ref, w1_ref, w3_ref, w2_ref, o_ref, h_ref, acc_ref):
    j = pl.program_id(1)

    @pl.when(j == 0)
    def _():
        h_ref[...] = _modulated_norm(x_ref[...], g_ref[...], m_ref[0, 3:4, :], m_ref[0, 4:5, :]).astype(BF16)
        acc_ref[...] = jnp.zeros_like(acc_ref)

    h = h_ref[...]
    a = jnp.dot(h, w1_ref[...], preferred_element_type=F32)
    b = jnp.dot(h, w3_ref[...], preferred_element_type=F32)
    acc_ref[...] += _dot(_silu(a) * b, w2_ref[...])

    @pl.when(j == pl.num_programs(1) - 1)
    def _():
        o_ref[...] = x_ref[...] + m_ref[0, 5:6, :] * acc_ref[...]


def _ffn(xall, mod, norm_g, w1, w3, w2, dims):
    n_lat, t_len, n_batch, d_a, d_b, d_c = dims
    n_rows, d = xall.shape
    d_ff = w1.shape[1]
    tm, tf = FFN_ROWS, FFN_FF_TILE
    assert d_ff % tf == 0
    assert t_len % tm == 0
    grp = lambda i, j: (_mod_group(i, tm, n_lat, t_len, n_batch), 0, 0)
    return pl.pallas_call(
        _ffn_kernel,
        name="ffn",
        out_shape=jax.ShapeDtypeStruct((n_rows, d), F32),
        grid=(pl.cdiv(n_rows, tm), d_ff // tf),
        in_specs=[pl.BlockSpec((tm, d), lambda i, j: (i, 0)),
                  pl.BlockSpec((1, 6, d), grp),
                  pl.BlockSpec((1, d), lambda i, j: (0, 0)),
                  pl.BlockSpec((d, tf), lambda i, j: (0, j)),
                  pl.BlockSpec((d, tf), lambda i, j: (0, j)),
                  pl.BlockSpec((tf, d), lambda i, j: (j, 0))],
        out_specs=pl.BlockSpec((tm, d), lambda i, j: (i, 0)),
        scratch_shapes=[pltpu.VMEM((tm, d), BF16), pltpu.VMEM((tm, d), F32)],
        compiler_params=_cparams(("parallel", "arbitrary")),
    )(xall, mod, norm_g, w1, w3, w2)


def _rows_to_tiles(x):
    r, d = x.shape
    slabs = jnp.stack([x[:, s * LANES:(s + 1) * LANES].reshape(r // SUBLANES, SUBLANES, LANES)
                       for s in range(d // LANES)], axis=1)
    return jnp.swapaxes(slabs, 1, 2).reshape(r, d // LANES, LANES)


def _tiles_to_slabs(x3):
    r, n_tile, _ = x3.shape
    y = jnp.swapaxes(x3.reshape(r // SUBLANES, SUBLANES, n_tile, LANES), 1, 2)
    return [y[:, s].reshape(r, LANES) for s in range(n_tile)]


def _route_kernel(x_ref, m_ref, g_ref, wr_ref, br_ref, h_ref, e_ref, gt_ref):
    h = _modulated_norm(x_ref[...], g_ref[...], m_ref[0, 3:4, :], m_ref[0, 4:5, :])
    h_ref[...] = _rows_to_tiles(h)
    lane = _iota((h.shape[0], LANES), 1)
    logits = jnp.where(lane < N_EXPERTS, _dot_hi(h, wr_ref[...]) + br_ref[...], -jnp.inf)
    m1 = jnp.max(logits, axis=-1, keepdims=True)
    lane_f = lane.astype(F32)
    i1 = jnp.min(jnp.where(logits == m1, lane_f, float(LANES)), axis=-1, keepdims=True).astype(jnp.int32)
    rest = jnp.where(lane == i1, -jnp.inf, logits)
    m2 = jnp.max(rest, axis=-1, keepdims=True)
    i2 = jnp.min(jnp.where(rest == m2, lane_f, float(LANES)), axis=-1, keepdims=True).astype(jnp.int32)
    e2 = jnp.exp(m2 - m1)
    g1 = 1.0 / (1.0 + e2)
    g2 = e2 / (1.0 + e2)
    e_ref[...] = jnp.where(lane == 0, i1, jnp.where(lane == 1, i2, 0))
    gt_ref[...] = jnp.where(lane == 0, g1, jnp.where(lane == 1, g2, 0.0))


def _route(x, mod, norm_g, wr_pad, br_pad, dims):
    n_lat, t_len, n_batch, d_a, d_b, d_c = dims
    n, d = x.shape
    tm = ROW_TILE
    row = lambda i: (i, 0)
    const = lambda i: (0, 0)
    grp = lambda i: (_mod_group(i, tm, n_lat, t_len, n_batch), 0, 0)
    return pl.pallas_call(
        _route_kernel,
        name="moe_route",
        out_shape=(jax.ShapeDtypeStruct((n, d // LANES, LANES), F32), jax.ShapeDtypeStruct((n, LANES), jnp.int32),
                   jax.ShapeDtypeStruct((n, LANES), F32)),
        grid=(n // tm,),
        in_specs=[pl.BlockSpec((tm, d), row), pl.BlockSpec((1, 6, d), grp), pl.BlockSpec((1, d), const),
                  pl.BlockSpec((d, LANES), const), pl.BlockSpec((1, LANES), const)],
        out_specs=(pl.BlockSpec((tm, d // LANES, LANES), lambda i: (i, 0, 0)), pl.BlockSpec((tm, LANES), row),
                   pl.BlockSpec((tm, LANES), row)),
        compiler_params=_cparams(("parallel",)),
    )(x, mod, norm_g, wr_pad, br_pad)


def _expert_kernel(be_ref, nv_ref, idx0_ref, idxn_ref, dstp_ref, dstc_ref, h_hbm, w1_ref, w3_ref, w2_ref, y_hbm,
                   xbuf_ref, hb_ref, acc_ref, out_ref, sem_in, sem_out, *, rows_per_step, n_pairs):
    i = pl.program_id(0)
    j = pl.program_id(1)
    n_blk = pl.num_programs(0)
    n_ff = pl.num_programs(1)
    bm = hb_ref.shape[0]
    rps = rows_per_step
    n_issue = xbuf_ref.shape[1]
    n_tile = xbuf_ref.shape[2]
    live = i < nv_ref[0]
    cur = i % 2
    nxt = 1 - cur

    def in_copy(tok, slot, r):
        return pltpu.make_async_copy(h_hbm.at[pl.ds(tok, 1)], xbuf_ref.at[slot, pl.ds(r, 1)], sem_in)

    def out_copy(slot, r, dst):
        return pltpu.make_async_copy(out_ref.at[slot, pl.ds(r, 1)], y_hbm.at[pl.ds(dst, 1)], sem_out)

    def wait_in(slot):
        for c in range(n_issue // rps):
            pltpu.make_async_copy(h_hbm.at[pl.ds(0, rps)], xbuf_ref.at[slot, pl.ds(c * rps, rps)], sem_in).wait()

    def wait_out(slot):
        for c in range(n_issue // rps):
            pltpu.make_async_copy(out_ref.at[slot, pl.ds(c * rps, rps)], y_hbm.at[pl.ds(0, rps)], sem_out).wait()

    @pl.when((i == 0) & (j == 0))
    def _():
        out_ref[...] = jnp.zeros_like(out_ref)

        def start(g, carry):
            for u in range(GATHER_PARTS):
                r = g * GATHER_PARTS + u
                in_copy(idx0_ref[0, 0, r], 0, r).start()
            return carry

        lax.fori_loop(0, n_issue // GATHER_PARTS, start, 0)

    @pl.when(j == 0)
    def _():
        wait_in(cur)

    @pl.when(live & (j == 0))
    def _():
        for s, slab in enumerate(_tiles_to_slabs(xbuf_ref[cur, 0:bm])):
            hb_ref[:, s * LANES:(s + 1) * LANES] = slab.astype(BF16)
        acc_ref[...] = jnp.zeros_like(acc_ref)

    def issue(part):
        per = rps // GATHER_PARTS
        for t in range(part * per, (part + 1) * per):
            r = j * rps + t
            in_copy(idxn_ref[0, 0, r], nxt, r).start(priority=0)
            dst = jnp.where(i == 0, n_pairs + r, dstp_ref[0, 0, r])
            out_copy(nxt, r, dst).start(priority=1)

    def compute(with_issue):
        h = hb_ref[...]
        a = _dot(h, w1_ref[0])
        if with_issue:
            issue(0)
        b = _dot(h, w3_ref[0])
        if with_issue:
            issue(1)
        g = (_silu(a) * b).astype(BF16)
        if with_issue:
            issue(2)
        acc_ref[...] += jnp.dot(g, w2_ref[0].astype(BF16), preferred_element_type=F32)
        if with_issue:
            issue(3)

    last = j == n_ff - 1
    not_last = jnp.logical_not(last)

    @pl.when(live & not_last)
    def _():
        compute(True)

    @pl.when(live & last)
    def _():
        compute(False)

    @pl.when(jnp.logical_not(live) & not_last)
    def _():
        for part in range(GATHER_PARTS):
            issue(part)

    @pl.when(last)
    def _():
        wait_out(nxt)

    @pl.when(live & last)
    def _():
        out_ref[cur, 0:bm] = _rows_to_tiles(acc_ref[...])

    @pl.when((i == n_blk - 1) & last)
    def _():
        wait_in(nxt)

        def start(g, carry):
            for u in range(GATHER_PARTS):
                r = g * GATHER_PARTS + u
                out_copy(cur, r, dstc_ref[0, 0, r]).start()
            return carry

        lax.fori_loop(0, n_issue // GATHER_PARTS, start, 0)
        wait_out(cur)


def _experts(h, slot_tok, slot_pair, block_e, n_live, w1, w3, w2):
    n, n_tile, _ = h.shape
    d = n_tile * LANES
    d_ff = w1.shape[2]
    bm, tf = MOE_ROWS, FF_TILE
    n_blk = slot_tok.shape[0] // bm
    n_ff = d_ff // tf
    n_pairs = 2 * n
    rps = -(-bm // ((n_ff - 1) * SUBLANES)) * SUBLANES
    assert rps % GATHER_PARTS == 0
    n_issue = rps * (n_ff - 1)
    extra = n_issue - bm
    idx = jnp.pad(slot_tok.reshape(n_blk, 1, bm), ((0, 0), (0, 0), (0, extra)))
    dump = n_pairs + jnp.arange(n_issue, dtype=jnp.int32)
    dst = jnp.concatenate([slot_pair.reshape(n_blk, 1, bm), jnp.broadcast_to(dump[bm:], (n_blk, 1, extra))], axis=2)
    smem = lambda f: pl.BlockSpec((1, 1, n_issue), f, memory_space=pltpu.SMEM)
    grid_spec = pltpu.PrefetchScalarGridSpec(
        num_scalar_prefetch=2,
        grid=(n_blk, n_ff),
        in_specs=[smem(lambda i, j, be, nv: (0, 0, 0)),
                  smem(lambda i, j, be, nv: (jnp.minimum(i + 1, n_blk - 1), 0, 0)),
                  smem(lambda i, j, be, nv: (jnp.maximum(i - 1, 0), 0, 0)),
                  smem(lambda i, j, be, nv: (i, 0, 0)),
                  pl.BlockSpec(memory_space=pl.ANY),
                  pl.BlockSpec((1, d, tf), lambda i, j, be, nv: (be[i], 0, j)),
                  pl.BlockSpec((1, d, tf), lambda i, j, be, nv: (be[i], 0, j)),
                  pl.BlockSpec((1, tf, d), lambda i, j, be, nv: (be[i], j, 0))],
        out_specs=pl.BlockSpec(memory_space=pl.ANY),
        scratch_shapes=[pltpu.VMEM((2, n_issue, n_tile, LANES), F32), pltpu.VMEM((bm, d), BF16),
                        pltpu.VMEM((bm, d), F32), pltpu.VMEM((2, n_issue, n_tile, LANES), F32),
                        pltpu.SemaphoreType.DMA(()), pltpu.SemaphoreType.DMA(())])
    return pl.pallas_call(
        functools.partial(_expert_kernel, rows_per_step=rps, n_pairs=n_pairs),
        name="moe_experts",
        out_shape=jax.ShapeDtypeStruct((n_pairs + n_issue, n_tile, LANES), F32),
        grid_spec=grid_spec,
        compiler_params=_cparams(("arbitrary", "arbitrary")),
    )(block_e, n_live, idx, idx, dst, dst, h, w1, w3, w2)


def _combine_kernel(x_ref, m_ref, gt_ref, y0_ref, y1_ref, o_ref):
    gt = gt_ref[...]
    for s, (y0, y1) in enumerate(zip(_tiles_to_slabs(y0_ref[...]), _tiles_to_slabs(y1_ref[...]))):
        cols = slice(s * LANES, (s + 1) * LANES)
        y = gt[:, 0:1] * y0 + gt[:, 1:2] * y1
        o_ref[:, cols] = x_ref[:, cols] + m_ref[0, 5:6, cols] * y


def _combine(x, mod, gates, y, dims):
    n_lat, t_len, n_batch, d_a, d_b, d_c = dims
    n, d = x.shape
    tm = ROW_TILE
    n_tile = y.shape[1]
    grp = lambda i: (_mod_group(i, tm, n_lat, t_len, n_batch), 0, 0)
    return pl.pallas_call(
        _combine_kernel,
        name="moe_combine",
        out_shape=jax.ShapeDtypeStruct((n, d), F32),
        grid=(n // tm,),
        in_specs=[pl.BlockSpec((tm, d), lambda i: (i, 0)),
                  pl.BlockSpec((1, 6, d), grp),
                  pl.BlockSpec((tm, LANES), lambda i: (i, 0)),
                  pl.BlockSpec((tm, n_tile, LANES), lambda i: (i, 0, 0)),
                  pl.BlockSpec((tm, n_tile, LANES), lambda i: (i + n // tm, 0, 0))],
        out_specs=pl.BlockSpec((tm, d), lambda i: (i, 0)),
        compiler_params=_cparams(("parallel",)),
    )(x, mod, gates, y, y)


def _moe(x, mod, norm_g, w_router, b_router, w1, w3, w2, dims):
    n, d = x.shape
    wr_pad = jnp.zeros((d, LANES), F32).at[:, :N_EXPERTS].set(w_router.astype(F32))
    br_pad = jnp.zeros((1, LANES), F32).at[0, :N_EXPERTS].set(b_router.astype(F32))
    h, e_tile, g_tile = _route(x, mod, norm_g, wr_pad, br_pad, dims)
    bm = MOE_ROWS
    e_flat = e_tile[:, :2].reshape(-1)
    onehot = (e_flat[:, None] == jnp.arange(N_EXPERTS, dtype=jnp.int32)[None, :]).astype(jnp.int32)
    csum = jnp.cumsum(onehot, axis=0)
    counts = csum[-1]
    rank = jnp.sum(csum * onehot, axis=1) - 1
    padded = (counts + bm - 1) // bm * bm
    pad_end = jnp.cumsum(padded)
    pad_start = pad_end - padded
    slot = pad_start[e_flat] + rank
    n_blocks = (2 * n) // bm + N_EXPERTS
    pair_flat = jnp.arange(2 * n, dtype=jnp.int32)
    slot_ids = jnp.arange(n_blocks * bm, dtype=jnp.int32)
    slot_pair = (2 * n + slot_ids % bm).at[slot].set((pair_flat % 2) * n + pair_flat // 2)
    slot_tok = jnp.where(slot_pair < 2 * n, slot_pair % n, slot_ids % n)
    blk_start = jnp.arange(n_blocks, dtype=jnp.int32) * bm
    block_e = jnp.minimum(jnp.sum((pad_end[None, :] <= blk_start[:, None]).astype(jnp.int32), axis=1),
                          N_EXPERTS - 1).astype(jnp.int32)
    n_live = (pad_end[-1] // bm).astype(jnp.int32).reshape(1)
    y = _experts(h, slot_tok, slot_pair, block_e, n_live, w1, w3, w2)
    return _combine(x, mod, g_tile, y, dims)


def _rope_tables(t_len, n_heads):
    t = jnp.arange(t_len)
    row = (t // GRID_W).astype(F32)
    col = (t % GRID_W).astype(F32)
    n_freq = HEAD_DIM // 4
    inv = ROPE_BASE ** (-jnp.arange(n_freq, dtype=F32) / n_freq)
    ang = jnp.concatenate([row[:, None] * inv, col[:, None] * inv], axis=-1)
    cos, sin = jnp.cos(ang), jnp.sin(ang)
    cos_h = jnp.concatenate([cos, cos], axis=-1)
    sin_h = jnp.concatenate([-sin, sin], axis=-1)
    return jnp.tile(cos_h, (1, n_heads)), jnp.tile(sin_h, (1, n_heads))


def kernel(x, c, ctx, c_ctx, ada_w, ada_b, norm1_g, norm2_g, w_in, w_out, hgrn_lb_raw, hgrn_onorm_g,
           gdn_conv_w, gdn_a_log, gdn_dt_bias, gdn_onorm_g, na_qnorm_g, na_knorm_g, na_rpb, ffn_w1, ffn_w3,
           ffn_w2, moe_router_w, moe_router_b, moe_w1, moe_w3, moe_w2):
    n_batch, t_len, d = x.shape
    l_ctx = ctx.shape[1]
    depth = w_in.shape[0]
    d_a = hgrn_lb_raw.shape[-1]
    d_b = gdn_conv_w.shape[-1] // 3
    n_hb = gdn_a_log.shape[-1]
    n_hc = na_rpb.shape[1]
    d_c = n_hc * HEAD_DIM
    n_lat = n_batch * t_len
    n_all = n_lat + n_batch * l_ctx
    dims = (n_lat, t_len, n_batch, d_a, d_b, d_c)
    assert d_a % HEAD_DIM == 0 and d_b == n_hb * HEAD_DIM
    assert t_len % ROW_TILE == 0 and (n_batch * l_ctx) % ROW_TILE == 0 and l_ctx % TOK_BLK == 0
    assert t_len % GRID_W == 0 and 4 * n_hb <= LANES

    cvec = jnp.zeros((8, d), F32).at[:n_batch].set(c.astype(F32)).at[n_batch].set(c_ctx.astype(F32))
    mod_all = _adaln(cvec, ada_w, ada_b)

    lb_soft = jax.nn.softmax(hgrn_lb_raw.astype(F32), axis=1)
    lower_bound = jnp.cumsum(lb_soft, axis=1) - lb_soft[:, :1]
    cos_t, sin_t = _rope_tables(t_len, n_hb)
    n_gate = 4 * n_hb
    sizes_a, sizes_b = 5 * d_a, 4 * d_b

    xall = jnp.concatenate([x.reshape(n_lat, d), ctx.reshape(n_batch * l_ctx, d)], axis=0).astype(F32)
    for l in range(depth):
        last = l == depth - 1
        mod = mod_all[l, :n_batch + 1].reshape(n_batch + 1, 6, d)
        w = w_in[l]
        w_ab = w[:, :sizes_a + sizes_b].astype(BF16)
        w_g = jnp.pad(w[:, sizes_a + sizes_b:sizes_a + sizes_b + n_gate], ((0, 0), (0, LANES - n_gate))).astype(BF16)
        w_c = w[:, sizes_a + sizes_b + n_gate:].astype(BF16)
        qn_g = jnp.tile(na_qnorm_g[l].astype(F32), n_hc)[None]
        kn_g = jnp.tile(na_knorm_g[l].astype(F32), n_hc)[None]
        pa, pb, pg, qn, kn, vv = _inproj(xall, mod, norm1_g[l][None].astype(F32), w_ab, w_g, w_c, qn_g, kn_g, dims)

        oaf, oab = _hgrn_scan(pa, lower_bound[:, l][:, None, :], dims)
        conv_w = jnp.zeros((8, 3 * d_b), F32).at[:CONV_K].set(gdn_conv_w[l].astype(F32))
        alog_x = jnp.repeat(gdn_a_log[l].astype(F32), HEAD_DIM, axis=-1)[:, None, :]
        dtb_x = jnp.repeat(gdn_dt_bias[l].astype(F32), HEAD_DIM, axis=-1)[:, None, :]
        obf, obb = _gdn_scan(_gdn_prep(pb, pg, conv_w, alog_x, dtb_x, cos_t, sin_t, dims), dims)
        bias_tab = _natten_bias(na_rpb[l], t_len // GRID_W)
        oc = _natten(qn, kn, vv, bias_tab, dims)
        n_out = n_lat if last else n_all
        if not last:
            oc = jnp.concatenate([oc, _ctx_attn(qn, kn, vv, dims)], axis=0)
        na_g = jnp.tile(hgrn_onorm_g[l].astype(F32), d_a // HEAD_DIM)[None]
        nb_g = jnp.tile(gdn_onorm_g[l].astype(F32), n_hb)[None]
        xall_mid = _outproj(xall, mod, oaf, oab, pa, obf, obb, pb, oc, na_g, nb_g, w_out[l].astype(BF16), n_out,
                            dims)
        i = l // 2
        if l % 2 == 0:
            xall = _ffn(xall_mid, mod, norm2_g[l][None].astype(F32), ffn_w1[i].astype(BF16),
                        ffn_w3[i].astype(BF16), ffn_w2[i].astype(BF16), dims)
        else:
            xall = _moe(xall_mid, mod, norm2_g[l][None].astype(F32), moe_router_w[i], moe_router_b[i],
                        moe_w1[i], moe_w3[i], moe_w2[i], dims)
    return xall[:n_lat].reshape(n_batch, t_len, d).astype(x.dtype)
```

```python
import functools
import math

import jax
import jax.numpy as jnp
import numpy as np
from jax import lax
from jax.experimental import pallas as pl
from jax.experimental.pallas import tpu as pltpu

F32 = jnp.float32
BF16 = jnp.bfloat16
HI = lax.Precision.HIGHEST

EPS = 1e-6
HEAD_DIM = 64
CHUNK = 64
TOK_BLK = 256
GRID_W = 64
WIN_R = 8
WIN_C = 16
CONV_K = 5
ROPE_BASE = 10000.0
N_EXPERTS = 8
LANES = 128
SUBLANES = 8
ROW_TILE = 512
FFN_ROWS = 1024
FF_TILE = 512
FFN_FF_TILE = 512
MOE_ROWS = 1024
GATHER_PARTS = 4
NA_HEADS_PER_GROUP = 4
NA_ROWS_PER_STEP = 4
VMEM_LIMIT = 56 * 1024 * 1024
MXU_DEPTH = 256
NEG_BIG = -1e30


def _cparams(sem):
    return pltpu.CompilerParams(dimension_semantics=sem, vmem_limit_bytes=VMEM_LIMIT)


def _dot(a, b):
    return jnp.dot(a.astype(BF16), b.astype(BF16), preferred_element_type=F32)


def _dot_nt(a, b):
    return lax.dot_general(a.astype(BF16), b.astype(BF16), (((1,), (1,)), ((), ())),
                           preferred_element_type=F32)


def _dot_tn(a, b):
    return lax.dot_general(a.astype(BF16), b.astype(BF16), (((0,), (0,)), ((), ())),
                           preferred_element_type=F32)


def _dot_hi(a, b):
    return jnp.dot(a, b, precision=HI, preferred_element_type=F32)


def _iota(shape, dim):
    return lax.broadcasted_iota(jnp.int32, shape, dim)


def _head_mask(n):
    return (_iota((n, n), 0) // HEAD_DIM) == (_iota((n, n), 1) // HEAD_DIM)


def _head_sums(x, scale):
    n = x.shape[-1]
    w = min(n, MXU_DEPTH)
    gm = jnp.where(_head_mask(w), scale, 0.0).astype(BF16)
    hi = x.astype(BF16)
    lo = (x - hi.astype(F32)).astype(BF16)
    dot = lambda a: jnp.dot(a, gm, preferred_element_type=F32)
    parts = [dot(lo[:, s:s + w]) + dot(hi[:, s:s + w]) for s in range(0, n, w)]
    return parts[0] if len(parts) == 1 else jnp.concatenate(parts, axis=1)


def _group_mean_sq(x):
    return _head_sums(x * x, 1.0 / HEAD_DIM)


def _silu(x):
    return x * jax.nn.sigmoid(x)


def _modulated_norm(x, g, shift, scale):
    ms = jnp.mean(x * x, axis=-1, keepdims=True)
    y = x * lax.rsqrt(ms + EPS) * g
    return y * (1.0 + scale) + shift


def _mod_group(i, tile, n_lat, t_len, n_batch):
    return jnp.where(i * tile < n_lat, (i * tile) // t_len, n_batch)


def _adaln_kernel(c_ref, w_ref, b_ref, o_ref):
    s = _silu(c_ref[...])
    o_ref[0] = _dot(s, w_ref[0]) + b_ref[0]


def _adaln(cvec, ada_w, ada_b):
    depth, d, d6 = ada_w.shape
    tn = 1024
    return pl.pallas_call(
        _adaln_kernel,
        name="adaln",
        out_shape=jax.ShapeDtypeStruct((depth, 8, d6), F32),
        grid=(depth, d6 // tn),
        in_specs=[pl.BlockSpec((8, d), lambda l, j: (0, 0)),
                  pl.BlockSpec((1, d, tn), lambda l, j: (l, 0, j)),
                  pl.BlockSpec((1, 1, tn), lambda l, j: (l, 0, j))],
        out_specs=pl.BlockSpec((1, 8, tn), lambda l, j: (l, 0, j)),
        compiler_params=_cparams(("parallel", "parallel")),
    )(cvec, ada_w, ada_b.reshape(depth, 1, d6))


def _inproj_kernel(x_ref, m_ref, g_ref, wab_ref, wg_ref, wc_ref, qg_ref, kg_ref,
                   pa_ref, pb_ref, pg_ref, qn_ref, kn_ref, vv_ref, *, d_a5, d_b4, d_c):
    h = _modulated_norm(x_ref[...], g_ref[...], m_ref[0, 0:1, :], m_ref[0, 1:2, :]).astype(BF16)
    pa_ref[...] = jnp.dot(h, wab_ref[:, 0:d_a5], preferred_element_type=F32)
    pb_ref[...] = jnp.dot(h, wab_ref[:, d_a5:d_a5 + d_b4], preferred_element_type=F32)
    pg_ref[...] = jnp.dot(h, wg_ref[...], preferred_element_type=F32)
    q = jnp.dot(h, wc_ref[:, 0:d_c], preferred_element_type=F32)
    q = q * lax.rsqrt(_group_mean_sq(q) + EPS) * qg_ref[...]
    qn_ref[...] = (q * HEAD_DIM ** -0.5).astype(BF16)
    k = jnp.dot(h, wc_ref[:, d_c:2 * d_c], preferred_element_type=F32)
    k = k * lax.rsqrt(_group_mean_sq(k) + EPS) * kg_ref[...]
    kn_ref[...] = k.astype(BF16)
    vv_ref[...] = jnp.dot(h, wc_ref[:, 2 * d_c:3 * d_c], preferred_element_type=F32).astype(BF16)


def _inproj(xall, mod, norm_g, w_ab, w_g, w_c, qn_g, kn_g, dims):
    n_all, d = xall.shape
    n_lat, t_len, n_batch, d_a, d_b, d_c = dims
    tm = ROW_TILE
    row = lambda i: (i, 0)
    const = lambda i: (0, 0)
    grp = lambda i: (_mod_group(i, tm, n_lat, t_len, n_batch), 0, 0)
    kern = functools.partial(_inproj_kernel, d_a5=5 * d_a, d_b4=4 * d_b, d_c=d_c)
    return pl.pallas_call(
        kern,
        name="inproj",
        out_shape=(jax.ShapeDtypeStruct((n_all, 5 * d_a), F32),
                   jax.ShapeDtypeStruct((n_all, 4 * d_b), F32),
                   jax.ShapeDtypeStruct((n_all, LANES), F32),
                   jax.ShapeDtypeStruct((n_all, d_c), BF16),
                   jax.ShapeDtypeStruct((n_all, d_c), BF16),
                   jax.ShapeDtypeStruct((n_all, d_c), BF16)),
        grid=(n_all // tm,),
        in_specs=[pl.BlockSpec((tm, d), row),
                  pl.BlockSpec((1, 6, d), grp),
                  pl.BlockSpec((1, d), const),
                  pl.BlockSpec(w_ab.shape, const),
                  pl.BlockSpec(w_g.shape, const),
                  pl.BlockSpec(w_c.shape, const),
                  pl.BlockSpec((1, d_c), const),
                  pl.BlockSpec((1, d_c), const)],
        out_specs=(pl.BlockSpec((tm, 5 * d_a), row), pl.BlockSpec((tm, 4 * d_b), row),
                   pl.BlockSpec((tm, LANES), row), pl.BlockSpec((tm, d_c), row),
                   pl.BlockSpec((tm, d_c), row), pl.BlockSpec((tm, d_c), row)),
        compiler_params=_cparams(("parallel",)),
    )(xall, mod, norm_g, w_ab, w_g, w_c, qn_g, kn_g)


def _scan_block_index(b, d, j, n_lat_blk, n_ctx_blk, n_batch):
    jc = jnp.where(d == 0, j, n_ctx_blk - 1 - j)
    jl = jnp.where(d == 0, j - n_ctx_blk, n_lat_blk - 1 - (j - n_ctx_blk))
    return jnp.where(j < n_ctx_blk, n_batch * n_lat_blk + b * n_ctx_blk + jc, b * n_lat_blk + jl)


def _block_diag(x, mask_bd, n_rep):
    xb = x.astype(BF16)
    return jnp.where(mask_bd, jnp.concatenate([xb] * n_rep, axis=0), jnp.zeros((), BF16))


def _scan_order(idx, d):
    return idx + d * (CHUNK - 1 - 2 * idx)


def _split3(x):
    hi = x.astype(BF16)
    r = x - hi.astype(F32)
    mid = r.astype(BF16)
    lo = (r - mid.astype(F32)).astype(BF16)
    return hi, mid, lo


def _sel_dot(sel, x):
    hi, mid, lo = _split3(x)
    dot = lambda a: jnp.dot(sel, a, preferred_element_type=F32)
    return (dot(lo) + dot(mid)) + dot(hi)


def _dot_sel(x, sel):
    hi, mid, lo = _split3(x)
    dot = lambda a: jnp.dot(a, sel, preferred_element_type=F32)
    return (dot(lo) + dot(mid)) + dot(hi)


HGRN_LEVELS = tuple(CHUNK >> (i + 1) for i in range(int(math.log2(CHUNK))))


def _hgrn_kernel(qf_ref, zf_ref, vf_ref, qb_ref, zb_ref, vb_ref, lb_ref, of_ref, ob_ref, st_ref, *, n_chunk):
    ins = ((qf_ref, zf_ref, vf_ref, of_ref), (qb_ref, zb_ref, vb_ref, ob_ref))
    j = pl.program_id(1)
    n = qf_ref.shape[-1]
    n_rep = n // HEAD_DIM

    @pl.when(j == 0)
    def _():
        st_ref[...] = jnp.zeros_like(st_ref)

    mask_bd = _head_mask(n)
    gones = mask_bd.astype(BF16)
    bd = lambda a: _block_diag(a, mask_bd, n_rep)

    work = []
    masks = []
    for dd in range(2):
        q_ref, z_ref, v_ref, _ = ins[dd]
        lb = lb_ref[dd]
        s_t = _scan_order(_iota((CHUNK, CHUNK), 0), dd)
        s_u = _scan_order(_iota((CHUNK, CHUNK), 1), dd)
        sets = [s_u <= s_t, s_u > s_t]
        for m in HGRN_LEVELS:
            ref = (s_t // (2 * m)) * (2 * m) + m - 1
            sets.append((s_u > jnp.minimum(s_t, ref)) & (s_u <= jnp.maximum(s_t, ref)))
        sel = jnp.concatenate([jnp.where(a, 1.0, 0.0).astype(BF16) for a in sets], axis=0)
        s_row = _scan_order(_iota((CHUNK, n), 0), dd)
        s_col = _scan_order(_iota((CHUNK, n), 1) % HEAD_DIM, dd)
        masks.append((s_row == s_col, [(s_row // m) % 2 == 1 for m in HGRN_LEVELS],
                      [(s_row // (2 * m)) == (s_col // (2 * m)) for m in HGRN_LEVELS]))
        for c in range(n_chunk):
            cc = c if dd == 0 else n_chunk - 1 - c
            r = slice(cc * CHUNK, (cc + 1) * CHUNK)
            q = q_ref[r, :] * HEAD_DIM ** -0.5
            z = z_ref[r, :]
            v = v_ref[r, :]
            log_f = (jnp.minimum(z, 0.0) - jnp.log1p(jnp.exp(-jnp.abs(z)))) + jnp.log1p(lb * jnp.exp(-z))
            k = (1.0 - lb) * jax.nn.sigmoid(-z)
            work.append((dd, r, q, k, v, _sel_dot(sel, log_f)))
    atts = [jnp.where(masks[dd][0], jnp.dot((q * k).astype(BF16), gones, preferred_element_type=F32), 0.0)
            for dd, _, q, k, _, _ in work]
    for lvl, m in enumerate(HGRN_LEVELS):
        for i, (dd, _, q, k, _, sums) in enumerate(work):
            second = masks[dd][1][lvl]
            e = jnp.exp(sums[(2 + lvl) * CHUNK:(3 + lvl) * CHUNK])
            a_m = _dot_nt(jnp.where(second, q * e, 0.0), bd(jnp.where(second, 0.0, k * e)))
            if 2 * m < CHUNK:
                a_m = jnp.where(masks[dd][2][lvl], a_m, 0.0)
            atts[i] = atts[i] + a_m
    pre = []
    for (dd, r, q, k, v, sums), att in zip(work, atts):
        b_incl = sums[0:CHUNK]
        b_after = sums[CHUNK:2 * CHUNK]
        e_end = jnp.exp(b_incl[0:1, :] + b_after[0:1, :])
        upd = jnp.where(mask_bd, _dot_tn(v, k * jnp.exp(b_after)), 0.0)
        pre.append((dd, r, _dot(att, bd(v)), (q * jnp.exp(b_incl)).astype(BF16), e_end, upd))

    sts = [st_ref[0], st_ref[1]]
    for c in range(n_chunk):
        for dd in range(2):
            _, r, o_intra, q_hat, e_end, upd = pre[dd * n_chunk + c]
            ins[dd][3][r, :] = o_intra + _dot_nt(q_hat, sts[dd])
            sts[dd] = sts[dd] * e_end + upd
    st_ref[0] = sts[0]
    st_ref[1] = sts[1]


def _hgrn_scan(pa, lb2, dims):
    n_all = pa.shape[0]
    n_lat, t_len, n_batch, d_a, d_b, d_c = dims
    n_lat_blk = t_len // TOK_BLK
    n_ctx_blk = (n_all - n_lat) // n_batch // TOK_BLK
    blk = functools.partial(_scan_block_index, n_lat_blk=n_lat_blk, n_ctx_blk=n_ctx_blk, n_batch=n_batch)
    kern = functools.partial(_hgrn_kernel, n_chunk=TOK_BLK // CHUNK)
    in_specs = []
    for dd in range(2):
        in_specs += [pl.BlockSpec((TOK_BLK, d_a), lambda b, j, dd=dd: (blk(b, dd, j), 0)),
                     pl.BlockSpec((TOK_BLK, d_a), lambda b, j, dd=dd: (blk(b, dd, j), 1 + dd)),
                     pl.BlockSpec((TOK_BLK, d_a), lambda b, j, dd=dd: (blk(b, dd, j), 3))]
    in_specs.append(pl.BlockSpec((2, 1, d_a), lambda b, j: (0, 0, 0)))
    out_specs = tuple(pl.BlockSpec((TOK_BLK, d_a), lambda b, j, dd=dd: (blk(b, dd, j), 0)) for dd in range(2))
    return pl.pallas_call(
        kern,
        name="hgrn_scan",
        out_shape=(jax.ShapeDtypeStruct((n_all, d_a), F32), jax.ShapeDtypeStruct((n_all, d_a), F32)),
        grid=(n_batch, n_lat_blk + n_ctx_blk),
        in_specs=in_specs,
        out_specs=out_specs,
        scratch_shapes=[pltpu.VMEM((2, d_a, d_a), F32)],
        compiler_params=_cparams(("parallel", "arbitrary")),
    )(pa, pa, pa, pa, pa, pa, lb2)


def _gdn_prep_kernel(x_ref, prev_ref, next_ref, pg_ref, cw_ref, alog_ref, dtb_ref, cos_ref, sin_ref,
                     w_ref, u0_ref, qh_ref, qg_ref, kh_ref, ee_ref,
                     *, n_lat_blk, n_ctx_blk, n_lat_blks_total, d_b):
    i = pl.program_id(0)
    is_lat = i < n_lat_blks_total
    pos = jnp.where(is_lat, i % n_lat_blk, (i - n_lat_blks_total) % n_ctx_blk)
    n_seq_blk = jnp.where(is_lat, n_lat_blk, n_ctx_blk)
    first = pos == 0
    last = pos == n_seq_blk - 1
    x = x_ref[...]
    halo = prev_ref.shape[0]
    prev = jnp.where(first, 0.0, prev_ref[...])
    nxt = jnp.where(last, 0.0, next_ref[...])
    rows = _iota(x.shape, 0)
    tb = x.shape[0]
    half = CONV_K // 2
    acc = x * cw_ref[half:half + 1, :]
    for s in range(1, half + 1):
        xs = pltpu.roll(x, s, 0)
        for r in range(s):
            xs = jnp.where(rows == r, prev[halo - s + r:halo - s + r + 1, :], xs)
        acc = acc + xs * cw_ref[half - s:half - s + 1, :]
        xs = pltpu.roll(x, tb - s, 0)
        for r in range(s):
            xs = jnp.where(rows == tb - s + r, nxt[r:r + 1, :], xs)
        acc = acc + xs * cw_ref[half + s:half + s + 1, :]
    y = _silu(acc)
    def l2n(a):
        return a * lax.rsqrt(_head_sums(a * a, 1.0) + EPS)

    lane = _iota((tb, d_b), 1) % HEAD_DIM
    lo = lane < HEAD_DIM // 2

    def rope(a):
        partner = jnp.where(lo, pltpu.roll(a, d_b - HEAD_DIM // 2, 1), pltpu.roll(a, HEAD_DIM // 2, 1))
        return jnp.where(is_lat, a * cos_ref[...] + partner * sin_ref[...], a)

    q_all = rope(l2n(y[:, 0:d_b])) * HEAD_DIM ** -0.5
    k_all = rope(l2n(y[:, d_b:2 * d_b]))
    v_all = y[:, 2 * d_b:3 * d_b]
    n_h = d_b // HEAD_DIM
    g = pg_ref[...]
    e_r = _iota((LANES, d_b), 0)
    e_c = _iota((LANES, d_b), 1) // HEAD_DIM
    mask_bd = _head_mask(d_b)
    bd = lambda a: _block_diag(a, mask_bd, n_h)
    ones_cc = jnp.ones((CHUNK, CHUNK), BF16)
    ee_ref[...] = jnp.zeros_like(ee_ref)
    work = []
    t_invs = []
    for dd in range(2):
        a_x = _dot_sel(g, jnp.where(e_r == dd * n_h + e_c, 1.0, 0.0).astype(BF16))
        b_x = _dot_sel(g, jnp.where(e_r == (2 + dd) * n_h + e_c, 1.0, 0.0).astype(BF16))
        t = a_x + dtb_ref[dd]
        softplus = jnp.maximum(t, 0.0) + jnp.log1p(jnp.exp(-jnp.abs(t)))
        la_all = -jnp.exp(alog_ref[dd]) * softplus
        be_all = jax.nn.sigmoid(b_x)
        s_t = _scan_order(_iota((CHUNK, CHUNK), 0), dd)
        s_u = _scan_order(_iota((CHUNK, CHUNK), 1), dd)
        m_incl = jnp.where(s_u <= s_t, 1.0, 0.0).astype(BF16)
        s_row = _scan_order(_iota((CHUNK, d_b), 0), dd)
        s_col = _scan_order(_iota((CHUNK, d_b), 1) % HEAD_DIM, dd)
        incl = s_col <= s_row
        strict = s_col < s_row
        m_before = jnp.where(s_row <= s_col, 1.0, 0.0)
        eye_f = jnp.where(s_col == s_row, 1.0, 0.0)
        pair = (s_row // 2) == (s_col // 2)
        levels = []
        m = 4
        while m <= CHUNK:
            levels.append(((s_row // m) == (s_col // m)) & ((s_row // (m // 2)) != (s_col // (m // 2))))
            m *= 2
        end_row = CHUNK - 1 if dd == 0 else 0
        for c in range(tb // CHUNK):
            r = slice(c * CHUNK, (c + 1) * CHUNK)
            q, k, v, la, be = q_all[r], k_all[r], v_all[r], la_all[r], be_all[r]
            g_t = _sel_dot(m_incl, la)
            g_s = _sel_dot(ones_cc, la * m_before)
            gam = jnp.where(incl, jnp.exp(jnp.minimum(g_t - g_s, 0.0)), 0.0)
            k_bd = bd(k)
            kk = _dot_nt(k, k_bd)
            qk = _dot_nt(q, k_bd)
            a = jnp.where(strict, be * kk * gam, 0.0)
            eg = jnp.exp(g_t)
            g_end = g_t[end_row:end_row + 1, :]
            qh_ref[dd, r, :] = (q * eg).astype(BF16)
            qg_ref[dd, r, :] = (qk * gam).astype(BF16)
            kh_ref[dd, r, :] = (k * jnp.exp(g_end - g_t)).astype(BF16)
            ee_ref[dd, 0, c:c + 1, :] = jnp.exp(g_end)
            work.append((dd, r, a, bd(be * eg * k), bd(be * v), levels))
            t_invs.append(eye_f - jnp.where(pair, a, 0.0))
    for lv in range(int(math.log2(CHUNK)) - 1):
        xs = [_dot(jnp.where(wk[5][lv], wk[2], 0.0), bd(t)) for wk, t in zip(work, t_invs)]
        t_invs = [t - _dot(t, bd(x)) for t, x in zip(t_invs, xs)]
    for (dd, r, _, wk_bd, vb_bd, _), t in zip(work, t_invs):
        w_ref[dd, r, :] = _dot(t, wk_bd).astype(BF16)
        u0_ref[dd, r, :] = _dot(t, vb_bd)


def _gdn_prep(pb, pg, conv_w, alog_x, dtb_x, cos_t, sin_t, dims):
    n_all = pb.shape[0]
    n_lat, t_len, n_batch, d_a, d_b, d_c = dims
    n_lat_blk = t_len // TOK_BLK
    n_ctx_blk = (n_all - n_lat) // n_batch // TOK_BLK
    n_blk = n_all // TOK_BLK
    halo = 8
    per = TOK_BLK // halo
    kern = functools.partial(_gdn_prep_kernel, n_lat_blk=n_lat_blk, n_ctx_blk=n_ctx_blk,
                             n_lat_blks_total=n_lat // TOK_BLK, d_b=d_b)
    row = lambda i: (i, 0)
    const2 = lambda i: (0, 0)
    const3 = lambda i: (0, 0, 0)
    tab = lambda i: (jnp.where(i < n_lat // TOK_BLK, i % n_lat_blk, 0), 0)
    both = pl.BlockSpec((2, TOK_BLK, d_b), lambda i: (0, i, 0))
    sds = jax.ShapeDtypeStruct
    return pl.pallas_call(
        kern,
        name="gdn_prep",
        out_shape=(sds((2, n_all, d_b), BF16), sds((2, n_all, d_b), F32), sds((2, n_all, d_b), BF16),
                   sds((2, n_all, d_b), BF16), sds((2, n_all, d_b), BF16), sds((2, n_blk, 8, d_b), F32)),
        grid=(n_blk,),
        in_specs=[pl.BlockSpec((TOK_BLK, 3 * d_b), row),
                  pl.BlockSpec((halo, 3 * d_b), lambda i: (jnp.maximum(i * per - 1, 0), 0)),
                  pl.BlockSpec((halo, 3 * d_b), lambda i: (jnp.minimum((i + 1) * per, n_blk * per - 1), 0)),
                  pl.BlockSpec((TOK_BLK, LANES), row),
                  pl.BlockSpec((8, 3 * d_b), const2),
                  pl.BlockSpec((2, 1, d_b), const3),
                  pl.BlockSpec((2, 1, d_b), const3),
                  pl.BlockSpec((TOK_BLK, d_b), tab),
                  pl.BlockSpec((TOK_BLK, d_b), tab)],
        out_specs=(both, both, both, both, both, pl.BlockSpec((2, 1, 8, d_b), lambda i: (0, i, 0, 0))),
        compiler_params=_cparams(("parallel",)),
    )(pb, pb, pb, pg, conv_w, alog_x, dtb_x, cos_t, sin_t)


def _gdn_kernel(*refs, n_chunk, n_batch, blk):
    n_chain = 2 * n_batch
    ins = [refs[6 * c:6 * c + 6] for c in range(n_chain)]
    o_hbm = refs[6 * n_chain:6 * n_chain + 2]
    st_ref, stage_ref, sem = refs[6 * n_chain + 2:]
    j = pl.program_id(0)
    n = stage_ref.shape[-1]
    n_rep = n // HEAD_DIM

    def out_copy(c, step):
        row0 = pl.multiple_of(blk(c // 2, c % 2, step) * TOK_BLK, TOK_BLK)
        return pltpu.make_async_copy(stage_ref.at[c], o_hbm[c % 2].at[pl.ds(row0, TOK_BLK)], sem.at[c])

    @pl.when(j == 0)
    def _():
        st_ref[...] = jnp.zeros_like(st_ref)

    @pl.when(j > 0)
    def _():
        for c in range(n_chain):
            out_copy(c, j - 1).wait()

    mask_bd = _head_mask(n)
    bd = lambda a: _block_diag(a, mask_bd, n_rep)
    sts = [st_ref[c] for c in range(n_chain)]
    for ck in range(n_chunk):
        ccs = [ck if c % 2 == 0 else n_chunk - 1 - ck for c in range(n_chain)]
        rs = [slice(cc * CHUNK, (cc + 1) * CHUNK) for cc in ccs]
        us = [ins[c][1][0, rs[c], :] - _dot_nt(ins[c][0][0, rs[c], :], sts[c]) for c in range(n_chain)]
        o_state = [_dot_nt(ins[c][2][0, rs[c], :], sts[c]) for c in range(n_chain)]
        u_bds = [bd(u) for u in us]
        for c in range(n_chain):
            stage_ref[c, rs[c], :] = o_state[c] + _dot(ins[c][3][0, rs[c], :], u_bds[c])
        upds = [jnp.where(mask_bd, _dot_tn(us[c], ins[c][4][0, rs[c], :]), 0.0) for c in range(n_chain)]
        sts = [sts[c] * ins[c][5][0, 0, ccs[c]:ccs[c] + 1, :] + upds[c] for c in range(n_chain)]
    for c in range(n_chain):
        st_ref[c] = sts[c]
        out_copy(c, j).start()

    @pl.when(j == pl.num_programs(0) - 1)
    def _():
        for c in range(n_chain):
            out_copy(c, j).wait()


def _gdn_scan(wy, dims):
    n_all = wy[0].shape[1]
    n_lat, t_len, n_batch, d_a, d_b, d_c = dims
    n_lat_blk = t_len // TOK_BLK
    n_ctx_blk = (n_all - n_lat) // n_batch // TOK_BLK
    blk = functools.partial(_scan_block_index, n_lat_blk=n_lat_blk, n_ctx_blk=n_ctx_blk, n_batch=n_batch)
    kern = functools.partial(_gdn_kernel, n_chunk=TOK_BLK // CHUNK, n_batch=n_batch, blk=blk)
    in_specs = []
    for b in range(n_batch):
        for dd in range(2):
            tok = pl.BlockSpec((1, TOK_BLK, d_b), lambda j, b=b, dd=dd: (dd, blk(b, dd, j), 0))
            in_specs += [tok] * 5 + [pl.BlockSpec((1, 1, 8, d_b), lambda j, b=b, dd=dd: (dd, blk(b, dd, j), 0, 0))]
    n_chain = 2 * n_batch
    return pl.pallas_call(
        kern,
        name="gdn_scan",
        out_shape=(jax.ShapeDtypeStruct((n_all, d_b), F32), jax.ShapeDtypeStruct((n_all, d_b), F32)),
        grid=(n_lat_blk + n_ctx_blk,),
        in_specs=in_specs,
        out_specs=(pl.BlockSpec(memory_space=pl.ANY), pl.BlockSpec(memory_space=pl.ANY)),
        scratch_shapes=[pltpu.VMEM((n_chain, d_b, d_b), F32), pltpu.VMEM((n_chain, TOK_BLK, d_b), F32),
                        pltpu.SemaphoreType.DMA((n_chain,))],
        compiler_params=_cparams(("arbitrary",)),
    )(*(list(wy) * n_chain))


def _stack_heads(q, n_rep):
    m, n = q.shape
    keep = (_iota((n_rep * m, n), 0) // m) == (_iota((n_rep * m, n), 1) // HEAD_DIM)
    return jnp.where(keep, jnp.concatenate([q] * n_rep, axis=0), jnp.zeros((), q.dtype))


def _fold_heads(o, m, n_rep):
    n = o.shape[1]
    lane_h = _iota((m, n), 1) // HEAD_DIM
    acc = jnp.zeros((m, n), F32)
    for h in range(n_rep):
        acc = acc + jnp.where(lane_h == h, o[h * m:(h + 1) * m, :], 0.0)
    return acc


def _natten_kernel(*refs, n_rows):
    q_ref, k_ref, v_ref, kc_ref, vc_ref = refs[:5]
    bias_refs = refs[5:-1]
    o_ref = refs[-1]
    step = pl.program_id(1)
    n = q_ref.shape[-1]
    kr = min(WIN_R, n_rows)
    dn = (((1,), (1,)), ((), ()))
    gw = NA_HEADS_PER_GROUP * HEAD_DIM
    n_stack = NA_HEADS_PER_GROUP * GRID_W
    units = []
    for rr, bias_ref in enumerate(bias_refs):
        r = step * len(bias_refs) + rr
        r0 = jnp.clip(r - WIN_R // 2, 0, n_rows - kr)
        sl = pl.ds(pl.multiple_of(r0 * GRID_W, GRID_W), kr * GRID_W)
        q_rows = slice(rr * GRID_W, (rr + 1) * GRID_W)
        for g in range(n // gw):
            lanes = slice(g * gw, (g + 1) * gw)
            qs = _stack_heads(q_ref[q_rows, lanes], NA_HEADS_PER_GROUP)
            s_loc = lax.dot_general(qs, k_ref[sl, lanes], dn, preferred_element_type=F32)
            s_loc = s_loc + bias_ref[0, g * n_stack:(g + 1) * n_stack, :]
            s_ctx = lax.dot_general(qs, kc_ref[:, lanes], dn, preferred_element_type=F32)
            units.append((q_rows, lanes, sl, s_loc, s_ctx))
    probs = []
    for q_rows, lanes, sl, s_loc, s_ctx in units:
        m = jnp.maximum(jnp.max(s_loc, axis=-1, keepdims=True), jnp.max(s_ctx, axis=-1, keepdims=True))
        p_loc = jnp.exp(s_loc - m)
        p_ctx = jnp.exp(s_ctx - m)
        inv = 1.0 / (jnp.sum(p_loc, axis=-1, keepdims=True) + jnp.sum(p_ctx, axis=-1, keepdims=True))
        probs.append(((p_loc * inv).astype(BF16), (p_ctx * inv).astype(BF16)))
    for (q_rows, lanes, sl, _, _), (p_loc, p_ctx) in zip(units, probs):
        o = jnp.dot(p_loc, v_ref[sl, lanes], preferred_element_type=F32)
        o = o + jnp.dot(p_ctx, vc_ref[:, lanes], preferred_element_type=F32)
        o_ref[q_rows, lanes] = _fold_heads(o, GRID_W, NA_HEADS_PER_GROUP)


def _natten(qn, kn, vv, bias_tab, dims):
    n_lat, t_len, n_batch, d_a, d_b, d_c = dims
    n_all = qn.shape[0]
    l_ctx = (n_all - n_lat) // n_batch
    n_rows = t_len // GRID_W
    kr = min(WIN_R, n_rows)
    ctx0 = n_lat // l_ctx
    rb = NA_ROWS_PER_STEP
    assert n_rows % rb == 0
    n_steps = n_rows // rb

    def cfg(rr):
        def index(b, i):
            r = i * rb + rr
            r0 = jnp.clip(r - WIN_R // 2, 0, n_rows - kr)
            return (r - r0, 0, 0)
        return index

    kern = functools.partial(_natten_kernel, n_rows=n_rows)
    return pl.pallas_call(
        kern,
        name="natten",
        out_shape=jax.ShapeDtypeStruct((n_lat, d_c), F32),
        grid=(n_batch, n_steps),
        in_specs=[pl.BlockSpec((rb * GRID_W, d_c), lambda b, i: (b * n_steps + i, 0)),
                  pl.BlockSpec((t_len, d_c), lambda b, i: (b, 0)),
                  pl.BlockSpec((t_len, d_c), lambda b, i: (b, 0)),
                  pl.BlockSpec((l_ctx, d_c), lambda b, i: (ctx0 + b, 0)),
                  pl.BlockSpec((l_ctx, d_c), lambda b, i: (ctx0 + b, 0))]
                 + [pl.BlockSpec((1,) + bias_tab.shape[1:], cfg(rr)) for rr in range(rb)],
        out_specs=pl.BlockSpec((rb * GRID_W, d_c), lambda b, i: (b * n_steps + i, 0)),
        compiler_params=_cparams(("parallel", "arbitrary")),
    )(qn, kn, vv, kn, vv, *([bias_tab] * rb))


def _ctx_attn_kernel(q_ref, k_ref, v_ref, o_ref):
    n = q_ref.shape[-1]
    n_rep = n // HEAD_DIM
    qs = _stack_heads(q_ref[...], n_rep)
    s = lax.dot_general(qs, k_ref[...], (((1,), (1,)), ((), ())), preferred_element_type=F32)
    p = jnp.exp(s - jnp.max(s, axis=-1, keepdims=True))
    p = p * (1.0 / jnp.sum(p, axis=-1, keepdims=True))
    o = jnp.dot(p.astype(BF16), v_ref[...], preferred_element_type=F32)
    o_ref[...] = _fold_heads(o, q_ref.shape[0], n_rep)


def _ctx_attn(qn, kn, vv, dims):
    n_lat, t_len, n_batch, d_a, d_b, d_c = dims
    n_all = qn.shape[0]
    l_ctx = (n_all - n_lat) // n_batch
    tq = 64
    per = l_ctx // tq
    q0 = n_lat // tq
    c0 = n_lat // l_ctx
    return pl.pallas_call(
        _ctx_attn_kernel,
        name="ctx_attn",
        out_shape=jax.ShapeDtypeStruct((n_all - n_lat, d_c), F32),
        grid=(n_batch, per),
        in_specs=[pl.BlockSpec((tq, d_c), lambda b, i: (q0 + b * per + i, 0)),
                  pl.BlockSpec((l_ctx, d_c), lambda b, i: (c0 + b, 0)),
                  pl.BlockSpec((l_ctx, d_c), lambda b, i: (c0 + b, 0))],
        out_specs=pl.BlockSpec((tq, d_c), lambda b, i: (b * per + i, 0)),
        compiler_params=_cparams(("parallel", "arbitrary")),
    )(qn, kn, vv)


def _natten_bias(rpb, n_rows):
    n_h = rpb.shape[0]
    kr = min(WIN_R, n_rows)
    cols = jnp.arange(GRID_W)
    c0 = jnp.clip(cols - WIN_C // 2, 0, GRID_W - WIN_C)
    kc = jnp.arange(GRID_W)
    in_win = (kc[None, :] >= c0[:, None]) & (kc[None, :] < c0[:, None] + WIN_C)
    per = 2 * GRID_W
    rp = rpb.astype(F32)
    u = jnp.concatenate([rp[..., WIN_C - 1:], jnp.zeros(rp.shape[:-1] + (per - 2 * WIN_C + 1,), F32),
                         rp[..., :WIN_C - 1]], axis=-1)
    rel = jnp.tile(u, (1, 1, GRID_W))[..., :GRID_W * (per - 1)]
    rel = rel.reshape(rp.shape[:-1] + (GRID_W, per - 1))[..., :GRID_W]
    toe = jnp.where(in_win[None, None], rel, NEG_BIG)
    tabs = jnp.stack([toe[:, WIN_R - 1 - delta:WIN_R - 1 - delta + kr] for delta in range(kr)])
    return tabs.transpose(0, 1, 3, 2, 4).reshape(kr, n_h * GRID_W, kr * GRID_W)


def _outproj_kernel(x_ref, m_ref, oaf_ref, oab_ref, ga_ref, obf_ref, obb_ref, gb_ref, oc_ref, na_ref, nb_ref, w_ref,
                    o_ref, *, d_a, d_b):
    oa = oaf_ref[...] + oab_ref[...]
    ya = oa * lax.rsqrt(_group_mean_sq(oa) + EPS) * na_ref[...] * _silu(ga_ref[...])
    ob = obf_ref[...] + obb_ref[...]
    yb = ob * lax.rsqrt(_group_mean_sq(ob) + EPS) * nb_ref[...] * _silu(gb_ref[...])
    acc = _dot(ya, w_ref[0:d_a, :])
    acc = acc + _dot(yb, w_ref[d_a:d_a + d_b, :])
    acc = acc + _dot(oc_ref[...], w_ref[d_a + d_b:, :])
    o_ref[...] = x_ref[...] + m_ref[0, 2:3, :] * acc


def _outproj(xall, mod, oaf, oab, pa, obf, obb, pb, oc, na_g, nb_g, w_out, n_rows_out, dims):
    n_lat, t_len, n_batch, d_a, d_b, d_c = dims
    d = xall.shape[1]
    tm = ROW_TILE
    row = lambda i: (i, 0)
    const = lambda i: (0, 0)
    grp = lambda i: (_mod_group(i, tm, n_lat, t_len, n_batch), 0, 0)
    kern = functools.partial(_outproj_kernel, d_a=d_a, d_b=d_b)
    return pl.pallas_call(
        kern,
        name="outproj",
        out_shape=jax.ShapeDtypeStruct((n_rows_out, d), F32),
        grid=(n_rows_out // tm,),
        in_specs=[pl.BlockSpec((tm, d), row),
                  pl.BlockSpec((1, 6, d), grp),
                  pl.BlockSpec((tm, d_a), row),
                  pl.BlockSpec((tm, d_a), row),
                  pl.BlockSpec((tm, d_a), lambda i: (i, 4)),
                  pl.BlockSpec((tm, d_b), row),
                  pl.BlockSpec((tm, d_b), row),
                  pl.BlockSpec((tm, d_b), lambda i: (i, 3)),
                  pl.BlockSpec((tm, d_c), row),
                  pl.BlockSpec((1, d_a), const),
                  pl.BlockSpec((1, d_b), const),
                  pl.BlockSpec(w_out.shape, const)],
        out_specs=pl.BlockSpec((tm, d), row),
        compiler_params=_cparams(("parallel",)),
    )(xall, mod, oaf, oab, pa, obf, obb, pb, oc, na_g, nb_g, w_out)


def _ffn_kernel(x_ref, m_ref, g_ref, w1_ref, w3_ref, w2_ref, o_ref, h_ref, acc_ref):
    j = pl.program_id(1)

    @pl.when(j == 0)
    def _():
        h_ref[...] = _modulated_norm(x_ref[...], g_ref[...], m_ref[0, 3:4, :], m_ref[0, 4:5, :]).astype(BF16)
        acc_ref[...] = jnp.zeros_like(acc_ref)

    h = h_ref[...]
    a = jnp.dot(h, w1_ref[...], preferred_element_type=F32)
    b = jnp.dot(h, w3_ref[...], preferred_element_type=F32)
    acc_ref[...] += _dot(_silu(a) * b, w2_ref[...])

    @pl.when(j == pl.num_programs(1) - 1)
    def _():
        o_ref[...] = x_ref[...] + m_ref[0, 5:6, :] * acc_ref[...]


def _ffn(xall, mod, norm_g, w1, w3, w2, dims):
    n_lat, t_len, n_batch, d_a, d_b, d_c = dims
    n_rows, d = xall.shape
    d_ff = w1.shape[1]
    tm, tf = FFN_ROWS, FFN_FF_TILE
    assert d_ff % tf == 0
    assert t_len % tm == 0
    grp = lambda i, j: (_mod_group(i, tm, n_lat, t_len, n_batch), 0, 0)
    return pl.pallas_call(
        _ffn_kernel,
        name="ffn",
        out_shape=jax.ShapeDtypeStruct((n_rows, d), F32),
        grid=(pl.cdiv(n_rows, tm), d_ff // tf),
        in_specs=[pl.BlockSpec((tm, d), lambda i, j: (i, 0)),
                  pl.BlockSpec((1, 6, d), grp),
                  pl.BlockSpec((1, d), lambda i, j: (0, 0)),
                  pl.BlockSpec((d, tf), lambda i, j: (0, j)),
                  pl.BlockSpec((d, tf), lambda i, j: (0, j)),
                  pl.BlockSpec((tf, d), lambda i, j: (j, 0))],
        out_specs=pl.BlockSpec((tm, d), lambda i, j: (i, 0)),
        scratch_shapes=[pltpu.VMEM((tm, d), BF16), pltpu.VMEM((tm, d), F32)],
        compiler_params=_cparams(("parallel", "arbitrary")),
    )(xall, mod, norm_g, w1, w3, w2)


def _rows_to_tiles(x):
    r, d = x.shape
    slabs = jnp.stack([x[:, s * LANES:(s + 1) * LANES].reshape(r // SUBLANES, SUBLANES, LANES)
                       for s in range(d // LANES)], axis=1)
    return jnp.swapaxes(slabs, 1, 2).reshape(r, d // LANES, LANES)


def _tiles_to_slabs(x3):
    r, n_tile, _ = x3.shape
    y = jnp.swapaxes(x3.reshape(r // SUBLANES, SUBLANES, n_tile, LANES), 1, 2)
    return [y[:, s].reshape(r, LANES) for s in range(n_tile)]


def _route_kernel(x_ref, m_ref, g_ref, wr_ref, br_ref, h_ref, e_ref, gt_ref):
    h = _modulated_norm(x_ref[...], g_ref[...], m_ref[0, 3:4, :], m_ref[0, 4:5, :])
    h_ref[...] = _rows_to_tiles(h)
    lane = _iota((h.shape[0], LANES), 1)
    logits = jnp.where(lane < N_EXPERTS, _dot_hi(h, wr_ref[...]) + br_ref[...], -jnp.inf)
    m1 = jnp.max(logits, axis=-1, keepdims=True)
    lane_f = lane.astype(F32)
    i1 = jnp.min(jnp.where(logits == m1, lane_f, float(LANES)), axis=-1, keepdims=True).astype(jnp.int32)
    rest = jnp.where(lane == i1, -jnp.inf, logits)
    m2 = jnp.max(rest, axis=-1, keepdims=True)
    i2 = jnp.min(jnp.where(rest == m2, lane_f, float(LANES)), axis=-1, keepdims=True).astype(jnp.int32)
    e2 = jnp.exp(m2 - m1)
    g1 = 1.0 / (1.0 + e2)
    g2 = e2 / (1.0 + e2)
    e_ref[...] = jnp.where(lane == 0, i1, jnp.where(lane == 1, i2, 0))
    gt_ref[...] = jnp.where(lane == 0, g1, jnp.where(lane == 1, g2, 0.0))


def _route(x, mod, norm_g, wr_pad, br_pad, dims):
    n_lat, t_len, n_batch, d_a, d_b, d_c = dims
    n, d = x.shape
    tm = ROW_TILE
    row = lambda i: (i, 0)
    const = lambda i: (0, 0)
    grp = lambda i: (_mod_group(i, tm, n_lat, t_len, n_batch), 0, 0)
    return pl.pallas_call(
        _route_kernel,
        name="moe_route",
        out_shape=(jax.ShapeDtypeStruct((n, d // LANES, LANES), F32), jax.ShapeDtypeStruct((n, LANES), jnp.int32),
                   jax.ShapeDtypeStruct((n, LANES), F32)),
        grid=(n // tm,),
        in_specs=[pl.BlockSpec((tm, d), row), pl.BlockSpec((1, 6, d), grp), pl.BlockSpec((1, d), const),
                  pl.BlockSpec((d, LANES), const), pl.BlockSpec((1, LANES), const)],
        out_specs=(pl.BlockSpec((tm, d // LANES, LANES), lambda i: (i, 0, 0)), pl.BlockSpec((tm, LANES), row),
                   pl.BlockSpec((tm, LANES), row)),
        compiler_params=_cparams(("parallel",)),
    )(x, mod, norm_g, wr_pad, br_pad)


def _expert_kernel(be_ref, nv_ref, idx0_ref, idxn_ref, dstp_ref, dstc_ref, h_hbm, w1_ref, w3_ref, w2_ref, y_hbm,
                   xbuf_ref, hb_ref, acc_ref, out_ref, sem_in, sem_out, *, rows_per_step, n_pairs):
    i = pl.program_id(0)
    j = pl.program_id(1)
    n_blk = pl.num_programs(0)
    n_ff = pl.num_programs(1)
    bm = hb_ref.shape[0]
    rps = rows_per_step
    n_issue = xbuf_ref.shape[1]
    n_tile = xbuf_ref.shape[2]
    live = i < nv_ref[0]
    cur = i % 2
    nxt = 1 - cur

    def in_copy(tok, slot, r):
        return pltpu.make_async_copy(h_hbm.at[pl.ds(tok, 1)], xbuf_ref.at[slot, pl.ds(r, 1)], sem_in)

    def out_copy(slot, r, dst):
        return pltpu.make_async_copy(out_ref.at[slot, pl.ds(r, 1)], y_hbm.at[pl.ds(dst, 1)], sem_out)

    def wait_in(slot):
        for c in range(n_issue // rps):
            pltpu.make_async_copy(h_hbm.at[pl.ds(0, rps)], xbuf_ref.at[slot, pl.ds(c * rps, rps)], sem_in).wait()

    def wait_out(slot):
        for c in range(n_issue // rps):
            pltpu.make_async_copy(out_ref.at[slot, pl.ds(c * rps, rps)], y_hbm.at[pl.ds(0, rps)], sem_out).wait()

    @pl.when((i == 0) & (j == 0))
    def _():
        out_ref[...] = jnp.zeros_like(out_ref)

        def start(g, carry):
            for u in range(GATHER_PARTS):
                r = g * GATHER_PARTS + u
                in_copy(idx0_ref[0, 0, r], 0, r).start()
            return carry

        lax.fori_loop(0, n_issue // GATHER_PARTS, start, 0)

    @pl.when(j == 0)
    def _():
        wait_in(cur)

    @pl.when(live & (j == 0))
    def _():
        for s, slab in enumerate(_tiles_to_slabs(xbuf_ref[cur, 0:bm])):
            hb_ref[:, s * LANES:(s + 1) * LANES] = slab.astype(BF16)
        acc_ref[...] = jnp.zeros_like(acc_ref)

    def issue(part):
        per = rps // GATHER_PARTS
        for t in range(part * per, (part + 1) * per):
            r = j * rps + t
            in_copy(idxn_ref[0, 0, r], nxt, r).start(priority=0)
            dst = jnp.where(i == 0, n_pairs + r, dstp_ref[0, 0, r])
            out_copy(nxt, r, dst).start(priority=1)

    def compute(with_issue):
        h = hb_ref[...]
        a = _dot(h, w1_ref[0])
        if with_issue:
            issue(0)
        b = _dot(h, w3_ref[0])
        if with_issue:
            issue(1)
        g = (_silu(a) * b).astype(BF16)
        if with_issue:
            issue(2)
        acc_ref[...] += jnp.dot(g, w2_ref[0].astype(BF16), preferred_element_type=F32)
        if with_issue:
            issue(3)

    last = j == n_ff - 1
    not_last = jnp.logical_not(last)

    @pl.when(live & not_last)
    def _():
        compute(True)

    @pl.when(live & last)
    def _():
        compute(False)

    @pl.when(jnp.logical_not(live) & not_last)
    def _():
        for part in range(GATHER_PARTS):
            issue(part)

    @pl.when(last)
    def _():
        wait_out(nxt)

    @pl.when(live & last)
    def _():
        out_ref[cur, 0:bm] = _rows_to_tiles(acc_ref[...])

    @pl.when((i == n_blk - 1) & last)
    def _():
        wait_in(nxt)

        def start(g, carry):
            for u in range(GATHER_PARTS):
                r = g * GATHER_PARTS + u
                out_copy(cur, r, dstc_ref[0, 0, r]).start()
            return carry

        lax.fori_loop(0, n_issue // GATHER_PARTS, start, 0)
        wait_out(cur)


def _experts(h, slot_tok, slot_pair, block_e, n_live, w1, w3, w2):
    n, n_tile, _ = h.shape
    d = n_tile * LANES
    d_ff = w1.shape[2]
    bm, tf = MOE_ROWS, FF_TILE
    n_blk = slot_tok.shape[0] // bm
    n_ff = d_ff // tf
    n_pairs = 2 * n
    rps = -(-bm // ((n_ff - 1) * SUBLANES)) * SUBLANES
    assert rps % GATHER_PARTS == 0
    n_issue = rps * (n_ff - 1)
    extra = n_issue - bm
    idx = jnp.pad(slot_tok.reshape(n_blk, 1, bm), ((0, 0), (0, 0), (0, extra)))
    dump = n_pairs + jnp.arange(n_issue, dtype=jnp.int32)
    dst = jnp.concatenate([slot_pair.reshape(n_blk, 1, bm), jnp.broadcast_to(dump[bm:], (n_blk, 1, extra))], axis=2)
    smem = lambda f: pl.BlockSpec((1, 1, n_issue), f, memory_space=pltpu.SMEM)
    w_e = lambda i, be, nv: be[jnp.minimum(i, jnp.maximum(nv[0], 1) - 1)]
    w_j = lambda i, j, nv: jnp.where(i < nv[0], j, n_ff - 1)
    grid_spec = pltpu.PrefetchScalarGridSpec(
        num_scalar_prefetch=2,
        grid=(n_blk, n_ff),
        in_specs=[smem(lambda i, j, be, nv: (0, 0, 0)),
                  smem(lambda i, j, be, nv: (jnp.minimum(i + 1, n_blk - 1), 0, 0)),
                  smem(lambda i, j, be, nv: (jnp.maximum(i - 1, 0), 0, 0)),
                  smem(lambda i, j, be, nv: (i, 0, 0)),
                  pl.BlockSpec(memory_space=pl.ANY),
                  pl.BlockSpec((1, d, tf), lambda i, j, be, nv: (w_e(i, be, nv), 0, w_j(i, j, nv))),
                  pl.BlockSpec((1, d, tf), lambda i, j, be, nv: (w_e(i, be, nv), 0, w_j(i, j, nv))),
                  pl.BlockSpec((1, tf, d), lambda i, j, be, nv: (w_e(i, be, nv), w_j(i, j, nv), 0))],
        out_specs=pl.BlockSpec(memory_space=pl.ANY),
        scratch_shapes=[pltpu.VMEM((2, n_issue, n_tile, LANES), F32), pltpu.VMEM((bm, d), BF16),
                        pltpu.VMEM((bm, d), F32), pltpu.VMEM((2, n_issue, n_tile, LANES), F32),
                        pltpu.SemaphoreType.DMA(()), pltpu.SemaphoreType.DMA(())])
    return pl.pallas_call(
        functools.partial(_expert_kernel, rows_per_step=rps, n_pairs=n_pairs),
        name="moe_experts",
        out_shape=jax.ShapeDtypeStruct((n_pairs + n_issue, n_tile, LANES), F32),
        grid_spec=grid_spec,
        compiler_params=_cparams(("arbitrary", "arbitrary")),
    )(block_e, n_live, idx, idx, dst, dst, h, w1, w3, w2)


def _combine_kernel(x_ref, m_ref, gt_ref, y0_ref, y1_ref, o_ref):
    gt = gt_ref[...]
    for s, (y0, y1) in enumerate(zip(_tiles_to_slabs(y0_ref[...]), _tiles_to_slabs(y1_ref[...]))):
        cols = slice(s * LANES, (s + 1) * LANES)
        y = gt[:, 0:1] * y0 + gt[:, 1:2] * y1
        o_ref[:, cols] = x_ref[:, cols] + m_ref[0, 5:6, cols] * y


def _combine(x, mod, gates, y, dims):
    n_lat, t_len, n_batch, d_a, d_b, d_c = dims
    n, d = x.shape
    tm = ROW_TILE
    n_tile = y.shape[1]
    grp = lambda i: (_mod_group(i, tm, n_lat, t_len, n_batch), 0, 0)
    return pl.pallas_call(
        _combine_kernel,
        name="moe_combine",
        out_shape=jax.ShapeDtypeStruct((n, d), F32),
        grid=(n // tm,),
        in_specs=[pl.BlockSpec((tm, d), lambda i: (i, 0)),
                  pl.BlockSpec((1, 6, d), grp),
                  pl.BlockSpec((tm, LANES), lambda i: (i, 0)),
                  pl.BlockSpec((tm, n_tile, LANES), lambda i: (i, 0, 0)),
                  pl.BlockSpec((tm, n_tile, LANES), lambda i: (i + n // tm, 0, 0))],
        out_specs=pl.BlockSpec((tm, d), lambda i: (i, 0)),
        compiler_params=_cparams(("parallel",)),
    )(x, mod, gates, y, y)


def _moe(x, mod, norm_g, w_router, b_router, w1, w3, w2, dims):
    n, d = x.shape
    wr_pad = jnp.zeros((d, LANES), F32).at[:, :N_EXPERTS].set(w_router.astype(F32))
    br_pad = jnp.zeros((1, LANES), F32).at[0, :N_EXPERTS].set(b_router.astype(F32))
    h, e_tile, g_tile = _route(x, mod, norm_g, wr_pad, br_pad, dims)
    bm = MOE_ROWS
    e_flat = e_tile[:, :2].reshape(-1)
    onehot = (e_flat[:, None] == jnp.arange(N_EXPERTS, dtype=jnp.int32)[None, :]).astype(jnp.int32)
    csum = jnp.cumsum(onehot, axis=0)
    counts = csum[-1]
    rank = jnp.sum(csum * onehot, axis=1) - 1
    padded = (counts + bm - 1) // bm * bm
    pad_end = jnp.cumsum(padded)
    pad_start = pad_end - padded
    slot = pad_start[e_flat] + rank
    n_blocks = (2 * n) // bm + N_EXPERTS
    pair_flat = jnp.arange(2 * n, dtype=jnp.int32)
    slot_ids = jnp.arange(n_blocks * bm, dtype=jnp.int32)
    slot_pair = (2 * n + slot_ids % bm).at[slot].set((pair_flat % 2) * n + pair_flat // 2)
    slot_tok = jnp.where(slot_pair < 2 * n, slot_pair % n, slot_ids % n)
    blk_start = jnp.arange(n_blocks, dtype=jnp.int32) * bm
    block_e = jnp.minimum(jnp.sum((pad_end[None, :] <= blk_start[:, None]).astype(jnp.int32), axis=1),
                          N_EXPERTS - 1).astype(jnp.int32)
    n_live = (pad_end[-1] // bm).astype(jnp.int32).reshape(1)
    y = _experts(h, slot_tok, slot_pair, block_e, n_live, w1, w3, w2)
    return _combine(x, mod, g_tile, y, dims)


def _rope_tables(t_len, n_heads):
    t = jnp.arange(t_len)
    row = (t // GRID_W).astype(F32)
    col = (t % GRID_W).astype(F32)
    n_freq = HEAD_DIM // 4
    inv = ROPE_BASE ** (-jnp.arange(n_freq, dtype=F32) / n_freq)
    ang = jnp.concatenate([row[:, None] * inv, col[:, None] * inv], axis=-1)
    cos, sin = jnp.cos(ang), jnp.sin(ang)
    cos_h = jnp.concatenate([cos, cos], axis=-1)
    sin_h = jnp.concatenate([-sin, sin], axis=-1)
    return jnp.tile(cos_h, (1, n_heads)), jnp.tile(sin_h, (1, n_heads))


def kernel(x, c, ctx, c_ctx, ada_w, ada_b, norm1_g, norm2_g, w_in, w_out, hgrn_lb_raw, hgrn_onorm_g,
           gdn_conv_w, gdn_a_log, gdn_dt_bias, gdn_onorm_g, na_qnorm_g, na_knorm_g, na_rpb, ffn_w1, ffn_w3,
           ffn_w2, moe_router_w, moe_router_b, moe_w1, moe_w3, moe_w2):
    n_batch, t_len, d = x.shape
    l_ctx = ctx.shape[1]
    depth = w_in.shape[0]
    d_a = hgrn_lb_raw.shape[-1]
    d_b = gdn_conv_w.shape[-1] // 3
    n_hb = gdn_a_log.shape[-1]
    n_hc = na_rpb.shape[1]
    d_c = n_hc * HEAD_DIM
    n_lat = n_batch * t_len
    n_all = n_lat + n_batch * l_ctx
    dims = (n_lat, t_len, n_batch, d_a, d_b, d_c)
    assert d_a % HEAD_DIM == 0 and d_b == n_hb * HEAD_DIM
    assert t_len % ROW_TILE == 0 and (n_batch * l_ctx) % ROW_TILE == 0 and l_ctx % TOK_BLK == 0
    assert t_len % GRID_W == 0 and 4 * n_hb <= LANES

    cvec = jnp.zeros((8, d), F32).at[:n_batch].set(c.astype(F32)).at[n_batch].set(c_ctx.astype(F32))
    mod_all = _adaln(cvec, ada_w, ada_b)

    lb_soft = jax.nn.softmax(hgrn_lb_raw.astype(F32), axis=1)
    lower_bound = jnp.cumsum(lb_soft, axis=1) - lb_soft[:, :1]
    cos_t, sin_t = _rope_tables(t_len, n_hb)
    n_gate = 4 * n_hb
    sizes_a, sizes_b = 5 * d_a, 4 * d_b

    xall = jnp.concatenate([x.reshape(n_lat, d), ctx.reshape(n_batch * l_ctx, d)], axis=0).astype(F32)
    for l in range(depth):
        last = l == depth - 1
        mod = mod_all[l, :n_batch + 1].reshape(n_batch + 1, 6, d)
        w = w_in[l]
        w_ab = w[:, :sizes_a + sizes_b].astype(BF16)
        w_g = jnp.pad(w[:, sizes_a + sizes_b:sizes_a + sizes_b + n_gate], ((0, 0), (0, LANES - n_gate))).astype(BF16)
        w_c = w[:, sizes_a + sizes_b + n_gate:].astype(BF16)
        qn_g = jnp.tile(na_qnorm_g[l].astype(F32), n_hc)[None]
        kn_g = jnp.tile(na_knorm_g[l].astype(F32), n_hc)[None]
        pa, pb, pg, qn, kn, vv = _inproj(xall, mod, norm1_g[l][None].astype(F32), w_ab, w_g, w_c, qn_g, kn_g, dims)

        oaf, oab = _hgrn_scan(pa, lower_bound[:, l][:, None, :], dims)
        conv_w = jnp.zeros((8, 3 * d_b), F32).at[:CONV_K].set(gdn_conv_w[l].astype(F32))
        alog_x = jnp.repeat(gdn_a_log[l].astype(F32), HEAD_DIM, axis=-1)[:, None, :]
        dtb_x = jnp.repeat(gdn_dt_bias[l].astype(F32), HEAD_DIM, axis=-1)[:, None, :]
        obf, obb = _gdn_scan(_gdn_prep(pb, pg, conv_w, alog_x, dtb_x, cos_t, sin_t, dims), dims)
        bias_tab = _natten_bias(na_rpb[l], t_len // GRID_W)
        oc = _natten(qn, kn, vv, bias_tab, dims)
        n_out = n_lat if last else n_all
        if not last:
            oc = jnp.concatenate([oc, _ctx_attn(qn, kn, vv, dims)], axis=0)
        na_g = jnp.tile(hgrn_onorm_g[l].astype(F32), d_a // HEAD_DIM)[None]
        nb_g = jnp.tile(gdn_onorm_g[l].astype(F32), n_hb)[None]
        xall_mid = _outproj(xall, mod, oaf, oab, pa, obf, obb, pb, oc, na_g, nb_g, w_out[l].astype(BF16), n_out,
                            dims)
        i = l // 2
        if l % 2 == 0:
            xall = _ffn(xall_mid, mod, norm2_g[l][None].astype(F32), ffn_w1[i].astype(BF16),
                        ffn_w3[i].astype(BF16), ffn_w2[i].astype(BF16), dims)
        else:
            xall = _moe(xall_mid, mod, norm2_g[l][None].astype(F32), moe_router_w[i], moe_router_b[i],
                        moe_w1[i], moe_w3[i], moe_w2[i], dims)
    return xall[:n_lat].reshape(n_batch, t_len, d).astype(x.dtype)
```
